```python
import jax, jax.numpy as jnp
from jax import lax
import numpy as np

D_MODEL = 2048
BATCH = 2
SEQ = 8192
DEPTH = 1

HEAD_DIM = 128
N_HEADS_DIL = 8
N_HEADS_FOX = 8
D_DIL = N_HEADS_DIL * HEAD_DIM
D_FOX = N_HEADS_FOX * HEAD_DIM
DIL_PATTERNS = ((128, 1), (512, 4), (2048, 16))
MAX_WINDOW = 2048
Q_BLOCK = 128
ROPE_THETA = 500000.0
ROPE_DIM = HEAD_DIM // 4
D_FF = 5632
NORM_EPS = 1e-6
IN_SPLITS = (D_DIL, D_DIL, D_DIL, D_FOX, D_FOX, D_FOX, N_HEADS_FOX, D_MODEL, D_MODEL)
IN_COLS = sum(IN_SPLITS)

kernel_name = "hybrid_dilated_fox_gated_macaron"


def rmsnorm(x, g):
    xf = x.astype(jnp.float32)
    y = xf * lax.rsqrt(jnp.mean(xf * xf, axis=-1, keepdims=True) + NORM_EPS)
    return (y * g.astype(jnp.float32)).astype(x.dtype)


def swiglu(x, w_gate, w_up, w_down):
    return (jax.nn.silu(x @ w_gate) * (x @ w_up)) @ w_down


def split_heads(t, n_heads):
    b, s, _ = t.shape
    return t.reshape(b, s, n_heads, HEAD_DIM).transpose(0, 2, 1, 3)


def merge_heads(t):
    b, h, s, d = t.shape
    return t.transpose(0, 2, 1, 3).reshape(b, s, h * d)


def partial_rope(t):
    s = t.shape[2]
    half = ROPE_DIM // 2
    pos = jnp.arange(s, dtype=jnp.float32)
    inv_freq = ROPE_THETA ** (-jnp.arange(0, ROPE_DIM, 2, dtype=jnp.float32) / ROPE_DIM)
    ang = pos[:, None] * inv_freq[None, :]
    cos, sin = jnp.cos(ang), jnp.sin(ang)
    rot = t[..., :ROPE_DIM].astype(jnp.float32)
    x1, x2 = rot[..., :half], rot[..., half:]
    rotated = jnp.concatenate([x1 * cos - x2 * sin, x2 * cos + x1 * sin], axis=-1)
    return jnp.concatenate([rotated.astype(t.dtype), t[..., ROPE_DIM:]], axis=-1)


def dilated_attention(q, k, v):
    b, h, s, dh = q.shape
    n_blocks = s // Q_BLOCK
    pad = ((0, 0), (0, 0), (MAX_WINDOW, 0), (0, 0))
    kp = jnp.pad(k, pad)
    vp = jnp.pad(v, pad)
    qb = q.reshape(b, h, n_blocks, Q_BLOCK, dh).transpose(2, 0, 1, 3, 4)

    def block(args):
        qblk, blk = args
        t0 = blk * Q_BLOCK
        nums, dens, maxs = [], [], []
        for w, d in DIL_PATTERNS:
            length = w + Q_BLOCK
            rows = length // d
            nq = Q_BLOCK // d
            start = t0 + MAX_WINDOW - w
            kr = lax.dynamic_slice_in_dim(kp, start, length, axis=2).reshape(b, h, rows, d, dh)
            vr = lax.dynamic_slice_in_dim(vp, start, length, axis=2).reshape(b, h, rows, d, dh)
            qr = qblk.reshape(b, h, nq, d, dh)
            sc = jnp.einsum('bhicd,bhrcd->bhcir', qr, kr, preferred_element_type=jnp.float32)
            r = jnp.arange(rows)
            ip = jnp.arange(nq)
            cc = jnp.arange(d)
            band = (r[None, :] >= ip[:, None]) & (r[None, :] <= ip[:, None] + w // d)
            valid = (t0 - w + r[None, :] * d + cc[:, None]) >= 0
            mask = band[None, :, :] & valid[:, None, :]
            sc = jnp.where(mask, sc, -jnp.inf)
            m = jnp.max(sc, axis=-1, keepdims=True)
            p = jnp.exp(sc - m)
            den = jnp.sum(p, axis=-1)
            num = jnp.einsum('bhcir,bhrcd->bhicd', p, vr.astype(jnp.float32))
            nums.append(num.reshape(b, h, Q_BLOCK, dh))
            dens.append(den.transpose(0, 1, 3, 2).reshape(b, h, Q_BLOCK))
            maxs.append(m[..., 0].transpose(0, 1, 3, 2).reshape(b, h, Q_BLOCK))
        num = jnp.stack(nums)
        den = jnp.stack(dens)
        mx = jnp.stack(maxs)
        wgt = jnp.exp(mx - jnp.max(mx, axis=0, keepdims=True))
        out = jnp.sum(num * wgt[..., None], axis=0) / jnp.sum(den * wgt, axis=0)[..., None]
        return out.astype(q.dtype)

    out = lax.map(block, (qb, jnp.arange(n_blocks)))
    return out.transpose(1, 2, 0, 3, 4).reshape(b, h, s, dh)


def forgetting_attention(q, k, v, log_f):
    b, h, s, dh = q.shape
    n_blocks = s // Q_BLOCK
    c = jnp.cumsum(log_f, axis=-1)
    qb = q.reshape(b, h, n_blocks, Q_BLOCK, dh).transpose(2, 0, 1, 3, 4)
    cb = c.reshape(b, h, n_blocks, Q_BLOCK).transpose(2, 0, 1, 3)
    kpos = jnp.arange(s)

    def block(args):
        qblk, cq, blk = args
        sc = jnp.einsum('bhqd,bhkd->bhqk', qblk, k, preferred_element_type=jnp.float32)
        sc = sc + cq[..., :, None] - c[:, :, None, :]
        qpos = blk * Q_BLOCK + jnp.arange(Q_BLOCK)
        sc = jnp.where(kpos[None, :] <= qpos[:, None], sc, -jnp.inf)
        p = jax.nn.softmax(sc, axis=-1)
        return jnp.einsum('bhqk,bhkd->bhqd', p.astype(v.dtype), v)

    out = lax.map(block, (qb, cb, jnp.arange(n_blocks)))
    return out.transpose(1, 2, 0, 3, 4).reshape(b, h, s, dh)


def setup_inputs(seed: int = 0) -> dict:
    key = jax.random.key(seed)
    ks = jax.random.split(key, 20)
    f32 = jnp.float32

    def w(k, shape, fan_in):
        return jax.random.normal(k, shape, f32) * fan_in ** -0.5

    def gain(k):
        return 1.0 + 0.02 * jax.random.normal(k, (DEPTH, D_MODEL), f32)

    return {
        "x": jax.random.normal(ks[0], (BATCH, SEQ, D_MODEL), f32),
        "ffn1_norm": gain(ks[1]),
        "ffn1_w_gate": w(ks[2], (DEPTH, D_MODEL, D_FF), D_MODEL),
        "ffn1_w_up": w(ks[3], (DEPTH, D_MODEL, D_FF), D_MODEL),
        "ffn1_w_down": w(ks[4], (DEPTH, D_FF, D_MODEL), D_FF),
        "mix_norm": gain(ks[5]),
        "w_in": w(ks[6], (DEPTH, D_MODEL, IN_COLS), D_MODEL),
        "b_forget": 4.0 + jax.random.normal(ks[7], (DEPTH, N_HEADS_FOX), f32),
        "b_gate_dil": 0.02 * jax.random.normal(ks[8], (DEPTH, D_MODEL), f32),
        "b_gate_fox": 0.02 * jax.random.normal(ks[9], (DEPTH, D_MODEL), f32),
        "w_proj_dil": w(ks[10], (DEPTH, D_DIL, D_MODEL), D_DIL),
        "w_proj_fox": w(ks[11], (DEPTH, D_FOX, D_MODEL), D_FOX),
        "w_out": w(ks[12], (DEPTH, D_MODEL, D_MODEL), D_MODEL),
        "ffn2_norm": gain(ks[13]),
        "ffn2_w_gate": w(ks[14], (DEPTH, D_MODEL, D_FF), D_MODEL),
        "ffn2_w_up": w(ks[15], (DEPTH, D_MODEL, D_FF), D_MODEL),
        "ffn2_w_down": w(ks[16], (DEPTH, D_FF, D_MODEL), D_FF),
        "final_norm": 1.0 + 0.02 * jax.random.normal(ks[17], (D_MODEL,), f32),
    }


def reference(x, ffn1_norm, ffn1_w_gate, ffn1_w_up, ffn1_w_down, mix_norm, w_in,
              b_forget, b_gate_dil, b_gate_fox, w_proj_dil, w_proj_fox, w_out,
              ffn2_norm, ffn2_w_gate, ffn2_w_up, ffn2_w_down, final_norm):
    split_points = list(np.cumsum(IN_SPLITS)[:-1])
    scale = HEAD_DIM ** -0.5
    for l in range(DEPTH):
        x = x + 0.5 * swiglu(rmsnorm(x, ffn1_norm[l]), ffn1_w_gate[l], ffn1_w_up[l], ffn1_w_down[l])

        h = rmsnorm(x, mix_norm[l])
        proj = h @ w_in[l]
        q_d, k_d, v_d, q_f, k_f, v_f, f_logit, g_d, g_f = jnp.split(proj, split_points, axis=-1)

        qa = partial_rope(split_heads(q_d, N_HEADS_DIL)) * scale
        ka = partial_rope(split_heads(k_d, N_HEADS_DIL))
        va = split_heads(v_d, N_HEADS_DIL)
        y_dil = merge_heads(dilated_attention(qa, ka, va))

        log_f = jax.nn.log_sigmoid((f_logit + b_forget[l]).astype(jnp.float32))
        log_f = log_f.transpose(0, 2, 1)
        qb_ = split_heads(q_f, N_HEADS_FOX) * scale
        kb_ = split_heads(k_f, N_HEADS_FOX)
        vb_ = split_heads(v_f, N_HEADS_FOX)
        y_fox = merge_heads(forgetting_attention(qb_, kb_, vb_, log_f))

        merged = (jax.nn.sigmoid(g_d + b_gate_dil[l]) * (y_dil @ w_proj_dil[l])
                  + jax.nn.sigmoid(g_f + b_gate_fox[l]) * (y_fox @ w_proj_fox[l]))
        x = x + merged @ w_out[l]

        x = x + 0.5 * swiglu(rmsnorm(x, ffn2_norm[l]), ffn2_w_gate[l], ffn2_w_up[l], ffn2_w_down[l])
    return rmsnorm(x, final_norm)
```

```python
import functools

import numpy as np
import jax
import jax.numpy as jnp
from jax import lax
from jax.experimental import pallas as pl
from jax.experimental.pallas import tpu as pltpu

D_MODEL = 2048
BATCH = 2
SEQ = 8192
DEPTH = 1
HEAD_DIM = 128
N_HEADS = 8
D_ATT = N_HEADS * HEAD_DIM
DIL_PATTERNS = ((128, 1), (512, 4), (2048, 16))
MAX_WINDOW = 2048
ROPE_THETA = 500000.0
ROPE_DIM = HEAD_DIM // 4
D_FF = 5632
NORM_EPS = 1e-6
N_TOK = BATCH * SEQ
LANES = 128

NEG_BIG = -1e30
MIB = 1024 * 1024

F32 = jnp.float32
BF16 = jnp.bfloat16


def _rmsnorm(x, gain):
    ms = jnp.mean(x * x, axis=-1, keepdims=True)
    return x * lax.rsqrt(ms + NORM_EPS) * gain


FFN_TM = 512
FFN_TF = 512


def _ffn_kernel(x_ref, gain_ref, wg_ref, wu_ref, wd_ref, fgain_ref, o_ref, h_ref, acc_ref,
                *, final_norm):
    j = pl.program_id(1)

    @pl.when(j == 0)
    def _():
        h_ref[...] = _rmsnorm(x_ref[...], gain_ref[...]).astype(BF16)
        acc_ref[...] = jnp.zeros_like(acc_ref)

    h = h_ref[...]
    g = jnp.dot(h, wg_ref[...], preferred_element_type=F32)
    u = jnp.dot(h, wu_ref[...], preferred_element_type=F32)
    a = (g * jax.nn.sigmoid(g)) * u
    acc_ref[...] += jnp.dot(a.astype(BF16), wd_ref[...], preferred_element_type=F32)

    @pl.when(j == pl.num_programs(1) - 1)
    def _():
        y = x_ref[...] + 0.5 * acc_ref[...]
        if final_norm:
            y = _rmsnorm(y, fgain_ref[...])
        o_ref[...] = y


def _ffn(x, gain, wg, wu, wd, fgain, final_norm):
    grid = (N_TOK // FFN_TM, D_FF // FFN_TF)
    return pl.pallas_call(
        functools.partial(_ffn_kernel, final_norm=final_norm),
        grid=grid,
        in_specs=[
            pl.BlockSpec((FFN_TM, D_MODEL), lambda i, j: (i, 0)),
            pl.BlockSpec((1, D_MODEL), lambda i, j: (0, 0)),
            pl.BlockSpec((D_MODEL, FFN_TF), lambda i, j: (0, j)),
            pl.BlockSpec((D_MODEL, FFN_TF), lambda i, j: (0, j)),
            pl.BlockSpec((FFN_TF, D_MODEL), lambda i, j: (j, 0)),
            pl.BlockSpec((1, D_MODEL), lambda i, j: (0, 0)),
        ],
        out_specs=pl.BlockSpec((FFN_TM, D_MODEL), lambda i, j: (i, 0)),
        out_shape=jax.ShapeDtypeStruct((N_TOK, D_MODEL), F32),
        scratch_shapes=[pltpu.VMEM((FFN_TM, D_MODEL), BF16), pltpu.VMEM((FFN_TM, D_MODEL), F32)],
        compiler_params=pltpu.CompilerParams(
            dimension_semantics=("parallel", "arbitrary"), vmem_limit_bytes=48 * MIB),
        name="ffn",
    )(x, gain, wg, wu, wd, fgain)


QKV_TM = 1024
QKV_TN = 512
HEADS_PER_TILE = QKV_TN // HEAD_DIM
N_QKV = 6
Q_DIL, K_DIL, V_DIL, Q_FOX, K_FOX, V_FOX = range(N_QKV)


def _rope(x, cos, sin_signed):
    lane = lax.broadcasted_iota(jnp.int32, x.shape, 1)
    half = ROPE_DIM // 2
    swapped = jnp.where(lane < half, pltpu.roll(x, LANES - half, 1), pltpu.roll(x, half, 1))
    return x * cos + swapped * sin_signed


def _qkv_kernel(x_ref, gain_ref, w_ref, wf_ref, bf_ref, cos_ref, sin_ref,
                qkv_ref, logf_ref, h_ref):
    n = pl.program_id(1)
    scale = HEAD_DIM ** -0.5

    @pl.when(n == 0)
    def _():
        h = _rmsnorm(x_ref[...], gain_ref[...])
        h_ref[...] = h.astype(BF16)
        f = jnp.dot(h, wf_ref[...], preferred_element_type=F32,
                    precision=lax.Precision.HIGHEST) + bf_ref[...]
        logf_ref[...] = jnp.minimum(f, 0.0) - jnp.log1p(jnp.exp(-jnp.abs(f)))

    r = jnp.dot(h_ref[...], w_ref[...], preferred_element_type=F32)
    which = n // (D_ATT // QKV_TN)

    def store(fn):
        for hh in range(HEADS_PER_TILE):
            qkv_ref[hh] = fn(r[:, hh * HEAD_DIM:(hh + 1) * HEAD_DIM]).astype(BF16)

    @pl.when(which == Q_DIL)
    def _():
        store(lambda t: _rope(t, cos_ref[...], sin_ref[...]) * scale)

    @pl.when(which == K_DIL)
    def _():
        store(lambda t: _rope(t, cos_ref[...], sin_ref[...]))

    @pl.when(which == Q_FOX)
    def _():
        store(lambda t: t * scale)

    @pl.when((which == V_DIL) | (which == K_FOX) | (which == V_FOX))
    def _():
        store(lambda t: t)


def _qkv_proj(x, gain, w_qkv, w_f, b_f, cos, sin_signed):
    tiles_per_which = D_ATT // QKV_TN
    s_tiles = SEQ // QKV_TM
    grid = (N_TOK // QKV_TM, N_QKV * tiles_per_which)
    return pl.pallas_call(
        _qkv_kernel,
        grid=grid,
        in_specs=[
            pl.BlockSpec((QKV_TM, D_MODEL), lambda m, n: (m, 0)),
            pl.BlockSpec((1, D_MODEL), lambda m, n: (0, 0)),
            pl.BlockSpec((D_MODEL, QKV_TN), lambda m, n: (0, n)),
            pl.BlockSpec((D_MODEL, LANES), lambda m, n: (0, 0)),
            pl.BlockSpec((1, LANES), lambda m, n: (0, 0)),
            pl.BlockSpec((QKV_TM, LANES), lambda m, n: (m % s_tiles, 0)),
            pl.BlockSpec((QKV_TM, LANES), lambda m, n: (m % s_tiles, 0)),
        ],
        out_specs=[
            pl.BlockSpec((None, None, HEADS_PER_TILE, QKV_TM, HEAD_DIM),
                         lambda m, n: (n // tiles_per_which, m // s_tiles, n % tiles_per_which,
                                       m % s_tiles, 0)),
            pl.BlockSpec((QKV_TM, LANES), lambda m, n: (m, 0)),
        ],
        out_shape=[
            jax.ShapeDtypeStruct((N_QKV, BATCH, N_HEADS, SEQ, HEAD_DIM), BF16),
            jax.ShapeDtypeStruct((N_TOK, LANES), F32),
        ],
        scratch_shapes=[pltpu.VMEM((QKV_TM, D_MODEL), BF16)],
        compiler_params=pltpu.CompilerParams(
            dimension_semantics=("parallel", "arbitrary"), vmem_limit_bytes=48 * MIB),
        name="qkv_proj",
    )(x, gain, w_qkv, w_f, b_f, cos, sin_signed)


CUM_TS = 1024
CUM_CHUNK = 128


def _cumsum_kernel(lf_ref, ccol_ref, crow_ref, carry_ref):
    @pl.when(pl.program_id(1) == 0)
    def _():
        carry_ref[...] = jnp.zeros_like(carry_ref)

    r = lax.broadcasted_iota(jnp.int32, (CUM_CHUNK, CUM_CHUNK), 0)
    c = lax.broadcasted_iota(jnp.int32, (CUM_CHUNK, CUM_CHUNK), 1)
    tri = (c <= r).astype(F32)
    carry = carry_ref[...]
    for i in range(CUM_TS // CUM_CHUNK):
        rows = slice(i * CUM_CHUNK, (i + 1) * CUM_CHUNK)
        cs = jnp.dot(tri, lf_ref[rows, :], preferred_element_type=F32,
                     precision=lax.Precision.HIGHEST) + carry
        ccol_ref[rows, :] = cs
        crow_ref[:, rows] = cs.T[0:N_HEADS, :]
        carry = cs[CUM_CHUNK - 1:CUM_CHUNK, :]
    carry_ref[...] = carry


def _cumsum(logf):
    return pl.pallas_call(
        _cumsum_kernel,
        grid=(BATCH, SEQ // CUM_TS),
        in_specs=[pl.BlockSpec((None, CUM_TS, LANES), lambda b, s: (b, s, 0))],
        out_specs=[
            pl.BlockSpec((None, CUM_TS, LANES), lambda b, s: (b, s, 0)),
            pl.BlockSpec((None, N_HEADS, CUM_TS), lambda b, s: (b, 0, s)),
        ],
        out_shape=[
            jax.ShapeDtypeStruct((BATCH, SEQ, LANES), F32),
            jax.ShapeDtypeStruct((BATCH, N_HEADS, SEQ), F32),
        ],
        scratch_shapes=[pltpu.VMEM((1, LANES), F32)],
        compiler_params=pltpu.CompilerParams(dimension_semantics=("parallel", "arbitrary")),
        name="gate_cumsum",
    )(logf)


FOX_T = 512


def _fox_kernel(q_ref, k_ref, v_ref, ccol_ref, crow_ref, o_ref):
    head = pl.program_id(0) % N_HEADS
    qi = pl.program_id(1)
    q = q_ref[...]
    lane = lax.broadcasted_iota(jnp.int32, (FOX_T, LANES), 1)
    cq = jnp.sum(jnp.where(lane == head, ccol_ref[...], 0.0), axis=-1, keepdims=True)

    def step(kb, carry, diagonal):
        m, l, acc = carry
        ks = pl.multiple_of(kb * FOX_T, FOX_T)
        k = k_ref[pl.ds(ks, FOX_T), :]
        v = v_ref[pl.ds(ks, FOX_T), :]
        s = lax.dot_general(q, k, (((1,), (1,)), ((), ())), preferred_element_type=F32)
        s = s + (cq - crow_ref[:, pl.ds(ks, FOX_T)])
        if diagonal:
            row = lax.broadcasted_iota(jnp.int32, s.shape, 0)
            col = lax.broadcasted_iota(jnp.int32, s.shape, 1)
            s = jnp.where(col <= row, s, NEG_BIG)
        m_new = jnp.maximum(m, jnp.max(s, axis=-1, keepdims=True))
        alpha = jnp.exp(m - m_new)
        p = jnp.exp(s - m_new)
        l = alpha * l + jnp.sum(p, axis=-1, keepdims=True)
        acc = alpha * acc + jnp.dot(p.astype(BF16), v, preferred_element_type=F32)
        return m_new, l, acc

    init = (jnp.full((FOX_T, 1), NEG_BIG, F32), jnp.zeros((FOX_T, 1), F32),
            jnp.zeros((FOX_T, HEAD_DIM), F32))
    carry = lax.fori_loop(0, qi, lambda kb, c: step(kb, c, False), init)
    _, l, acc = step(qi, carry, True)
    o_ref[...] = (acc / l).astype(BF16)


def _fox_attn(qkv, c_col, c_row):
    q_tiles = SEQ // FOX_T
    return pl.pallas_call(
        _fox_kernel,
        grid=(BATCH * N_HEADS, q_tiles),
        in_specs=[
            pl.BlockSpec((None, None, None, FOX_T, HEAD_DIM),
                         lambda bh, qi: (Q_FOX, bh // N_HEADS, bh % N_HEADS, qi, 0)),
            pl.BlockSpec((None, None, None, SEQ, HEAD_DIM),
                         lambda bh, qi: (K_FOX, bh // N_HEADS, bh % N_HEADS, 0, 0)),
            pl.BlockSpec((None, None, None, SEQ, HEAD_DIM),
                         lambda bh, qi: (V_FOX, bh // N_HEADS, bh % N_HEADS, 0, 0)),
            pl.BlockSpec((None, FOX_T, LANES), lambda bh, qi: (bh // N_HEADS, qi, 0)),
            pl.BlockSpec((None, 1, SEQ), lambda bh, qi: (bh, 0, 0)),
        ],
        out_specs=pl.BlockSpec((FOX_T, HEAD_DIM),
                               lambda bh, qi: ((bh // N_HEADS) * q_tiles + qi, bh % N_HEADS)),
        out_shape=jax.ShapeDtypeStruct((N_TOK, D_ATT), BF16),
        compiler_params=pltpu.CompilerParams(
            dimension_semantics=("parallel", "arbitrary"), vmem_limit_bytes=48 * MIB),
        name="fox_attn",
    )(qkv, qkv, qkv, c_col, c_row)


DIL_TQ = 256
DIL_WIN = MAX_WINDOW + DIL_TQ


def _dil_bias_table():
    i = np.arange(DIL_TQ)[:, None]
    jj = np.arange(MAX_WINDOW + DIL_WIN)[None, :]
    delta = MAX_WINDOW + i - jj
    mult = np.zeros(delta.shape, np.int64)
    for w, d in DIL_PATTERNS:
        mult += (delta >= 0) & (delta <= w) & (delta % d == 0)
    with np.errstate(divide="ignore"):
        bias = np.where(mult > 0, np.log(np.maximum(mult, 1).astype(np.float64)), NEG_BIG)
    return jnp.asarray(bias, F32)


def _dil_kernel(q_ref, k_ref, v_ref, bias_ref, o_ref):
    t0 = pl.program_id(1) * DIL_TQ
    ks = jnp.maximum(t0 - MAX_WINDOW, 0)
    off = pl.multiple_of(MAX_WINDOW - (t0 - ks), DIL_TQ)
    ks = pl.multiple_of(ks, DIL_TQ)
    k = k_ref[pl.ds(ks, DIL_WIN), :]
    v = v_ref[pl.ds(ks, DIL_WIN), :]
    s = lax.dot_general(q_ref[...], k, (((1,), (1,)), ((), ())), preferred_element_type=F32)
    s = s + bias_ref[:, pl.ds(off, DIL_WIN)]
    m = jnp.max(s, axis=-1, keepdims=True)
    p = jnp.exp(s - m)
    l = jnp.sum(p, axis=-1, keepdims=True)
    acc = jnp.dot(p.astype(BF16), v, preferred_element_type=F32)
    o_ref[...] = (acc / l).astype(BF16)


def _dil_attn(qkv, bias):
    q_tiles = SEQ // DIL_TQ
    return pl.pallas_call(
        _dil_kernel,
        grid=(BATCH * N_HEADS, q_tiles),
        in_specs=[
            pl.BlockSpec((None, None, None, DIL_TQ, HEAD_DIM),
                         lambda bh, qi: (Q_DIL, bh // N_HEADS, bh % N_HEADS, qi, 0)),
            pl.BlockSpec((None, None, None, SEQ, HEAD_DIM),
                         lambda bh, qi: (K_DIL, bh // N_HEADS, bh % N_HEADS, 0, 0)),
            pl.BlockSpec((None, None, None, SEQ, HEAD_DIM),
                         lambda bh, qi: (V_DIL, bh // N_HEADS, bh % N_HEADS, 0, 0)),
            pl.BlockSpec((DIL_TQ, MAX_WINDOW + DIL_WIN), lambda bh, qi: (0, 0)),
        ],
        out_specs=pl.BlockSpec((DIL_TQ, HEAD_DIM),
                               lambda bh, qi: ((bh // N_HEADS) * q_tiles + qi, bh % N_HEADS)),
        out_shape=jax.ShapeDtypeStruct((N_TOK, D_ATT), BF16),
        compiler_params=pltpu.CompilerParams(
            dimension_semantics=("parallel", "arbitrary"), vmem_limit_bytes=48 * MIB),
        name="dil_attn",
    )(qkv, qkv, qkv, bias)


MRG_TM = 512
MRG_TJ = 512


def _merge_kernel(x_ref, gain_ref, yd_ref, yf_ref, wgd_ref, wgf_ref, bgd_ref, bgf_ref,
                  wpd_ref, wpf_ref, wo_ref, o_ref, h_ref, acc_ref):
    j = pl.program_id(1)

    @pl.when(j == 0)
    def _():
        h_ref[...] = _rmsnorm(x_ref[...], gain_ref[...]).astype(BF16)
        acc_ref[...] = jnp.zeros_like(acc_ref)

    h = h_ref[...]
    gd = jnp.dot(h, wgd_ref[...], preferred_element_type=F32) + bgd_ref[...]
    gf = jnp.dot(h, wgf_ref[...], preferred_element_type=F32) + bgf_ref[...]
    pd = jnp.dot(yd_ref[...], wpd_ref[...], preferred_element_type=F32)
    pf = jnp.dot(yf_ref[...], wpf_ref[...], preferred_element_type=F32)
    merged = jax.nn.sigmoid(gd) * pd + jax.nn.sigmoid(gf) * pf
    acc_ref[...] += jnp.dot(merged.astype(BF16), wo_ref[...], preferred_element_type=F32)

    @pl.when(j == pl.num_programs(1) - 1)
    def _():
        o_ref[...] = x_ref[...] + acc_ref[...]


def _merge(x, gain, yd, yf, wgd, wgf, bgd, bgf, wpd, wpf, wo):
    grid = (N_TOK // MRG_TM, D_MODEL // MRG_TJ)
    return pl.pallas_call(
        _merge_kernel,
        grid=grid,
        in_specs=[
            pl.BlockSpec((MRG_TM, D_MODEL), lambda i, j: (i, 0)),
            pl.BlockSpec((1, D_MODEL), lambda i, j: (0, 0)),
            pl.BlockSpec((MRG_TM, D_ATT), lambda i, j: (i, 0)),
            pl.BlockSpec((MRG_TM, D_ATT), lambda i, j: (i, 0)),
            pl.BlockSpec((D_MODEL, MRG_TJ), lambda i, j: (0, j)),
            pl.BlockSpec((D_MODEL, MRG_TJ), lambda i, j: (0, j)),
            pl.BlockSpec((1, MRG_TJ), lambda i, j: (0, j)),
            pl.BlockSpec((1, MRG_TJ), lambda i, j: (0, j)),
            pl.BlockSpec((D_ATT, MRG_TJ), lambda i, j: (0, j)),
            pl.BlockSpec((D_ATT, MRG_TJ), lambda i, j: (0, j)),
            pl.BlockSpec((MRG_TJ, D_MODEL), lambda i, j: (j, 0)),
        ],
        out_specs=pl.BlockSpec((MRG_TM, D_MODEL), lambda i, j: (i, 0)),
        out_shape=jax.ShapeDtypeStruct((N_TOK, D_MODEL), F32),
        scratch_shapes=[pltpu.VMEM((MRG_TM, D_MODEL), BF16), pltpu.VMEM((MRG_TM, D_MODEL), F32)],
        compiler_params=pltpu.CompilerParams(
            dimension_semantics=("parallel", "arbitrary"), vmem_limit_bytes=56 * MIB),
        name="merge_proj",
    )(x, gain, yd, yf, wgd, wgf, bgd, bgf, wpd, wpf, wo)


def _rope_tables():
    half = ROPE_DIM // 2
    pos = jnp.arange(SEQ, dtype=F32)
    inv_freq = ROPE_THETA ** (-jnp.arange(0, ROPE_DIM, 2, dtype=F32) / ROPE_DIM)
    ang = pos[:, None] * inv_freq[None, :]
    cos, sin = jnp.cos(ang), jnp.sin(ang)
    rest = HEAD_DIM - ROPE_DIM
    cos_full = jnp.concatenate([cos, cos, jnp.ones((SEQ, rest), F32)], axis=-1)
    sin_signed = jnp.concatenate([-sin, sin, jnp.zeros((SEQ, rest), F32)], axis=-1)
    assert cos_full.shape == (SEQ, HEAD_DIM) and half * 2 == ROPE_DIM
    return cos_full, sin_signed


def kernel(x, ffn1_norm, ffn1_w_gate, ffn1_w_up, ffn1_w_down, mix_norm, w_in, b_forget, b_gate_dil, b_gate_fox, w_proj_dil, w_proj_fox, w_out, ffn2_norm, ffn2_w_gate, ffn2_w_up, ffn2_w_down, final_norm):
    assert x.shape == (BATCH, SEQ, D_MODEL) and x.dtype == F32
    cos_full, sin_signed = _rope_tables()
    dil_bias = _dil_bias_table()
    fgain = final_norm.reshape(1, D_MODEL)
    c_qkv = N_QKV * D_ATT
    c_gd = c_qkv + N_HEADS

    xt = x.reshape(N_TOK, D_MODEL)
    for l in range(DEPTH):
        last = l == DEPTH - 1
        w_in_l = w_in[l]
        w_qkv = w_in_l[:, :c_qkv].astype(BF16)
        w_f = jnp.pad(w_in_l[:, c_qkv:c_gd], ((0, 0), (0, LANES - N_HEADS)))
        b_f = jnp.pad(b_forget[l], (0, LANES - N_HEADS)).reshape(1, LANES)
        w_gd = w_in_l[:, c_gd:c_gd + D_MODEL].astype(BF16)
        w_gf = w_in_l[:, c_gd + D_MODEL:].astype(BF16)

        xt = _ffn(xt, ffn1_norm[l].reshape(1, D_MODEL), ffn1_w_gate[l].astype(BF16),
                  ffn1_w_up[l].astype(BF16), ffn1_w_down[l].astype(BF16), fgain, False)

        qkv, logf = _qkv_proj(xt, mix_norm[l].reshape(1, D_MODEL), w_qkv, w_f, b_f,
                              cos_full, sin_signed)
        c_col, c_row = _cumsum(logf.reshape(BATCH, SEQ, LANES))
        y_fox = _fox_attn(qkv, c_col, c_row.reshape(BATCH * N_HEADS, 1, SEQ))
        y_dil = _dil_attn(qkv, dil_bias)

        xt = _merge(xt, mix_norm[l].reshape(1, D_MODEL), y_dil, y_fox, w_gd, w_gf,
                    b_gate_dil[l].reshape(1, D_MODEL), b_gate_fox[l].reshape(1, D_MODEL),
                    w_proj_dil[l].astype(BF16), w_proj_fox[l].astype(BF16),
                    w_out[l].astype(BF16))

        xt = _ffn(xt, ffn2_norm[l].reshape(1, D_MODEL), ffn2_w_gate[l].astype(BF16),
                  ffn2_w_up[l].astype(BF16), ffn2_w_down[l].astype(BF16), fgain, last)
    return xt.reshape(BATCH, SEQ, D_MODEL)
```

```python
import functools

import numpy as np
import jax
import jax.numpy as jnp
from jax import lax
from jax.experimental import pallas as pl
from jax.experimental.pallas import tpu as pltpu

D_MODEL = 2048
BATCH = 2
SEQ = 8192
DEPTH = 1
HEAD_DIM = 128
N_HEADS = 8
D_ATT = N_HEADS * HEAD_DIM
DIL_PATTERNS = ((128, 1), (512, 4), (2048, 16))
MAX_WINDOW = 2048
ROPE_THETA = 500000.0
ROPE_DIM = HEAD_DIM // 4
D_FF = 5632
NORM_EPS = 1e-6
N_TOK = BATCH * SEQ
LANES = 128

NEG_BIG = -1e30
LOG2E = 1.4426950408889634
MIB = 1024 * 1024

F32 = jnp.float32
BF16 = jnp.bfloat16


def _rmsnorm(x, gain):
    ms = jnp.mean(x * x, axis=-1, keepdims=True)
    return x * lax.rsqrt(ms + NORM_EPS) * gain


FFN_TM = 512
FFN_TF = 512


def _ffn_kernel(x_ref, gain_ref, wg_ref, wu_ref, wd_ref, fgain_ref, o_ref, h_ref, acc_ref,
                *, final_norm):
    j = pl.program_id(1)

    @pl.when(j == 0)
    def _():
        h_ref[...] = _rmsnorm(x_ref[...], gain_ref[...]).astype(BF16)
        acc_ref[...] = jnp.zeros_like(acc_ref)

    h = h_ref[...]
    g = jnp.dot(h, wg_ref[...], preferred_element_type=F32)
    u = jnp.dot(h, wu_ref[...], preferred_element_type=F32)
    a = (g * jax.nn.sigmoid(g)) * u
    acc_ref[...] += jnp.dot(a.astype(BF16), wd_ref[...], preferred_element_type=F32)

    @pl.when(j == pl.num_programs(1) - 1)
    def _():
        y = x_ref[...] + 0.5 * acc_ref[...]
        if final_norm:
            y = _rmsnorm(y, fgain_ref[...])
        o_ref[...] = y


def _ffn(x, gain, wg, wu, wd, fgain, final_norm):
    grid = (N_TOK // FFN_TM, D_FF // FFN_TF)
    return pl.pallas_call(
        functools.partial(_ffn_kernel, final_norm=final_norm),
        grid=grid,
        in_specs=[
            pl.BlockSpec((FFN_TM, D_MODEL), lambda i, j: (i, 0)),
            pl.BlockSpec((1, D_MODEL), lambda i, j: (0, 0)),
            pl.BlockSpec((D_MODEL, FFN_TF), lambda i, j: (0, j)),
            pl.BlockSpec((D_MODEL, FFN_TF), lambda i, j: (0, j)),
            pl.BlockSpec((FFN_TF, D_MODEL), lambda i, j: (j, 0)),
            pl.BlockSpec((1, D_MODEL), lambda i, j: (0, 0)),
        ],
        out_specs=pl.BlockSpec((FFN_TM, D_MODEL), lambda i, j: (i, 0)),
        out_shape=jax.ShapeDtypeStruct((N_TOK, D_MODEL), F32),
        scratch_shapes=[pltpu.VMEM((FFN_TM, D_MODEL), BF16), pltpu.VMEM((FFN_TM, D_MODEL), F32)],
        compiler_params=pltpu.CompilerParams(
            dimension_semantics=("parallel", "arbitrary"), vmem_limit_bytes=48 * MIB),
        name="ffn",
    )(x, gain, wg, wu, wd, fgain)


QKV_TM = 1024
QKV_TN = 512
HEADS_PER_TILE = QKV_TN // HEAD_DIM
N_MIX_QKV = 3
QI, KI, VI = range(N_MIX_QKV)
Q_DIL, K_DIL, V_DIL, Q_FOX, K_FOX, V_FOX = range(2 * N_MIX_QKV)


def _rope(x, cos, sin_signed):
    lane = lax.broadcasted_iota(jnp.int32, x.shape, 1)
    half = ROPE_DIM // 2
    swapped = jnp.where(lane < half, pltpu.roll(x, LANES - half, 1), pltpu.roll(x, half, 1))
    return x * cos + swapped * sin_signed


def _qkv_kernel(x_ref, gain_ref, w_ref, wf_ref, bf_ref, cos_ref, sin_ref,
                dil_ref, fox_ref, logf_ref, h_ref):
    n = pl.program_id(1)
    scale = (HEAD_DIM ** -0.5) * LOG2E

    @pl.when(n == 0)
    def _():
        h = _rmsnorm(x_ref[...], gain_ref[...]).astype(BF16)
        h_ref[...] = h
        f = jnp.dot(h, wf_ref[...], preferred_element_type=F32) + bf_ref[...]
        logf_ref[...] = jnp.minimum(f, 0.0) - jnp.log1p(jnp.exp(-jnp.abs(f)))

    r = jnp.dot(h_ref[...], w_ref[...], preferred_element_type=F32)
    which = n // (D_ATT // QKV_TN)

    def store(out_ref, fn):
        for hh in range(HEADS_PER_TILE):
            out_ref[hh] = fn(r[:, hh * HEAD_DIM:(hh + 1) * HEAD_DIM]).astype(out_ref.dtype)

    @pl.when(which == Q_DIL)
    def _():
        store(dil_ref, lambda t: _rope(t, cos_ref[...], sin_ref[...]) * scale)

    @pl.when(which == K_DIL)
    def _():
        store(dil_ref, lambda t: _rope(t, cos_ref[...], sin_ref[...]))

    @pl.when(which == V_DIL)
    def _():
        store(dil_ref, lambda t: t)

    @pl.when(which == Q_FOX)
    def _():
        store(fox_ref, lambda t: t * scale)

    @pl.when((which == K_FOX) | (which == V_FOX))
    def _():
        store(fox_ref, lambda t: t)


def _qkv_proj(x, gain, w_qkv, w_f, b_f, cos, sin_signed):
    tiles_per_which = D_ATT // QKV_TN
    n_dil = N_MIX_QKV * tiles_per_which
    s_tiles = SEQ // QKV_TM
    grid = (N_TOK // QKV_TM, 2 * n_dil)

    def head_block(m, nn):
        return (nn // tiles_per_which, m // s_tiles, nn % tiles_per_which, m % s_tiles, 0)

    return pl.pallas_call(
        _qkv_kernel,
        grid=grid,
        in_specs=[
            pl.BlockSpec((QKV_TM, D_MODEL), lambda m, n: (m, 0)),
            pl.BlockSpec((1, D_MODEL), lambda m, n: (0, 0)),
            pl.BlockSpec((D_MODEL, QKV_TN), lambda m, n: (0, n)),
            pl.BlockSpec((D_MODEL, LANES), lambda m, n: (0, 0)),
            pl.BlockSpec((1, LANES), lambda m, n: (0, 0)),
            pl.BlockSpec((QKV_TM, LANES), lambda m, n: (m % s_tiles, 0)),
            pl.BlockSpec((QKV_TM, LANES), lambda m, n: (m % s_tiles, 0)),
        ],
        out_specs=[
            pl.BlockSpec((None, None, HEADS_PER_TILE, QKV_TM, HEAD_DIM),
                         lambda m, n: head_block(m, jnp.minimum(n, n_dil - 1))),
            pl.BlockSpec((None, None, HEADS_PER_TILE, QKV_TM, HEAD_DIM),
                         lambda m, n: head_block(m, jnp.maximum(n, n_dil) - n_dil)),
            pl.BlockSpec((QKV_TM, LANES), lambda m, n: (m, 0)),
        ],
        out_shape=[
            jax.ShapeDtypeStruct((N_MIX_QKV, BATCH, N_HEADS, SEQ, HEAD_DIM), F32),
            jax.ShapeDtypeStruct((N_MIX_QKV, BATCH, N_HEADS, SEQ, HEAD_DIM), BF16),
            jax.ShapeDtypeStruct((N_TOK, LANES), F32),
        ],
        scratch_shapes=[pltpu.VMEM((QKV_TM, D_MODEL), BF16)],
        compiler_params=pltpu.CompilerParams(
            dimension_semantics=("parallel", "arbitrary"), vmem_limit_bytes=48 * MIB),
        name="qkv_proj",
    )(x, gain, w_qkv, w_f, b_f, cos, sin_signed)


CUM_TS = 1024
CUM_CHUNK = 128


def _cumsum_kernel(lf_ref, c_ref, carry_ref):
    @pl.when(pl.program_id(1) == 0)
    def _():
        carry_ref[...] = jnp.zeros_like(carry_ref)

    r = lax.broadcasted_iota(jnp.int32, (CUM_CHUNK, CUM_CHUNK), 0)
    c = lax.broadcasted_iota(jnp.int32, (CUM_CHUNK, CUM_CHUNK), 1)
    tri = (c <= r).astype(F32)
    carry = carry_ref[...]
    for i in range(CUM_TS // CUM_CHUNK):
        rows = slice(i * CUM_CHUNK, (i + 1) * CUM_CHUNK)
        cs = jnp.dot(tri, lf_ref[rows, :], preferred_element_type=F32,
                     precision=lax.Precision.HIGHEST) + carry
        c_ref[rows, :] = cs
        carry = cs[CUM_CHUNK - 1:CUM_CHUNK, :]
    carry_ref[...] = carry


def _cumsum(logf):
    return pl.pallas_call(
        _cumsum_kernel,
        grid=(BATCH, SEQ // CUM_TS),
        in_specs=[pl.BlockSpec((None, CUM_TS, LANES), lambda b, s: (b, s, 0))],
        out_specs=pl.BlockSpec((None, CUM_TS, LANES), lambda b, s: (b, s, 0)),
        out_shape=jax.ShapeDtypeStruct((BATCH, SEQ, LANES), F32),
        scratch_shapes=[pltpu.VMEM((1, LANES), F32)],
        compiler_params=pltpu.CompilerParams(dimension_semantics=("parallel", "arbitrary")),
        name="gate_cumsum",
    )(logf)


FOX_TQ = 1024
FOX_TK = 512
BF16_ROWS = 16
FOX_VT_ROWS = HEAD_DIM + BF16_ROWS
N_SPLIT = 3


def _gate_columns(c_tile, head, key_side):
    lane = lax.broadcasted_iota(jnp.int32, c_tile.shape, 1)
    c = jnp.sum(jnp.where(lane == head, c_tile, 0.0), axis=-1, keepdims=True) * LOG2E
    if key_side:
        c = -c
    base = N_SPLIT if key_side else 0
    cols = jnp.where((lane >= N_SPLIT - base) & (lane < 2 * N_SPLIT - base), 1.0, 0.0)
    rest = c
    for t in range(N_SPLIT):
        term = rest.astype(BF16).astype(F32)
        cols = jnp.where(lane == base + t, term, cols)
        rest = rest - term
    return cols.astype(BF16)


def _fox_kernel(q_ref, k_ref, v_ref, c_ref, o_ref, kx_ref, vt_ref, st_ref, mx_ref, m_ref, acc_ref):
    head = pl.program_id(0) % N_HEADS
    qi = pl.program_id(1)

    @pl.when(qi == 0)
    def _():
        ones_tile = (lax.broadcasted_iota(jnp.int32, (BF16_ROWS, FOX_TK), 0) == 0).astype(BF16)

        def build(i, carry):
            rows = pl.ds(pl.multiple_of(i * FOX_TK, FOX_TK), FOX_TK)
            kx_ref[rows, :] = _gate_columns(c_ref[rows, :], head, True)
            vt_ref[0:HEAD_DIM, rows] = v_ref[rows, :].astype(F32).T.astype(BF16)
            vt_ref[HEAD_DIM:FOX_VT_ROWS, rows] = ones_tile
            return carry

        lax.fori_loop(0, SEQ // FOX_TK, build, 0)

    q_rows = pl.ds(pl.multiple_of(qi * FOX_TQ, FOX_TQ), FOX_TQ)
    q_aug = jnp.concatenate([q_ref[...], _gate_columns(c_ref[q_rows, :], head, False)], axis=-1)

    def key_rows(kb):
        return pl.ds(pl.multiple_of(kb * FOX_TK, FOX_TK), FOX_TK)

    def scores(kb, slot, q_lo=0):
        rows = key_rows(kb)
        k_aug = jnp.concatenate([k_ref[rows, :], kx_ref[rows, :]], axis=-1)
        st = lax.dot_general(k_aug, q_aug[q_lo:, :], (((1,), (1,)), ((), ())),
                             preferred_element_type=F32)
        st_ref[slot, :, q_lo:] = st
        mx_ref[slot, :, q_lo:] = jnp.max(st, axis=0, keepdims=True)

    def accumulate(kb, slot, q_lo=0, key_lo=None):
        st = st_ref[slot, :, q_lo:]
        if key_lo is None:
            st_max = mx_ref[slot, :, q_lo:]
        else:
            key = lax.broadcasted_iota(jnp.int32, st.shape, 0) + key_lo
            qry = lax.broadcasted_iota(jnp.int32, st.shape, 1) + q_lo
            st = jnp.where(key <= qry, st, NEG_BIG)
            st_max = jnp.max(st, axis=0, keepdims=True)
        m = m_ref[:, q_lo:]
        m_new = jnp.maximum(m, st_max)
        p = jnp.exp2(st - m_new).astype(BF16)
        pv = jnp.dot(vt_ref[:, key_rows(kb)], p, preferred_element_type=F32)
        acc_ref[:, q_lo:] = jnp.exp2(m - m_new) * acc_ref[:, q_lo:] + pv
        m_ref[:, q_lo:] = m_new

    m_ref[...] = jnp.full(m_ref.shape, NEG_BIG, F32)
    acc_ref[...] = jnp.zeros_like(acc_ref)
    scores(0, 0)

    def pair(j, carry):
        scores(2 * j + 1, 1)
        accumulate(2 * j, 0)
        scores(2 * j + 2, 0)
        accumulate(2 * j + 1, 1)
        return carry

    lax.fori_loop(0, qi, pair, 0)
    scores(2 * qi + 1, 1, q_lo=FOX_TK)
    accumulate(2 * qi, 0, key_lo=0)
    accumulate(2 * qi + 1, 1, q_lo=FOX_TK, key_lo=FOX_TK)

    out_t = acc_ref[0:HEAD_DIM, :] / acc_ref[HEAD_DIM:HEAD_DIM + 1, :]
    o_ref[...] = out_t.T.astype(BF16)


def _fox_attn(qkv, c_gate):
    assert FOX_TQ == 2 * FOX_TK
    q_tiles = SEQ // FOX_TQ
    return pl.pallas_call(
        _fox_kernel,
        grid=(BATCH * N_HEADS, q_tiles),
        in_specs=[
            pl.BlockSpec((None, None, None, FOX_TQ, HEAD_DIM),
                         lambda bh, qi: (QI, bh // N_HEADS, bh % N_HEADS, qi, 0)),
            pl.BlockSpec((None, None, None, SEQ, HEAD_DIM),
                         lambda bh, qi: (KI, bh // N_HEADS, bh % N_HEADS, 0, 0)),
            pl.BlockSpec((None, None, None, SEQ, HEAD_DIM),
                         lambda bh, qi: (VI, bh // N_HEADS, bh % N_HEADS, 0, 0)),
            pl.BlockSpec((None, SEQ, LANES), lambda bh, qi: (bh // N_HEADS, 0, 0)),
        ],
        out_specs=pl.BlockSpec((FOX_TQ, HEAD_DIM),
                               lambda bh, qi: ((bh // N_HEADS) * q_tiles + qi, bh % N_HEADS)),
        out_shape=jax.ShapeDtypeStruct((N_TOK, D_ATT), BF16),
        scratch_shapes=[
            pltpu.VMEM((SEQ, LANES), BF16),
            pltpu.VMEM((FOX_VT_ROWS, SEQ), BF16),
            pltpu.VMEM((2, FOX_TK, FOX_TQ), F32),
            pltpu.VMEM((2, 1, FOX_TQ), F32),
            pltpu.VMEM((1, FOX_TQ), F32),
            pltpu.VMEM((FOX_VT_ROWS, FOX_TQ), F32),
        ],
        compiler_params=pltpu.CompilerParams(
            dimension_semantics=("parallel", "arbitrary"), vmem_limit_bytes=48 * MIB),
        name="fox_attn",
    )(qkv, qkv, qkv, c_gate)


DIL_GROUP = 128
DIL_SUPER = DIL_GROUP * max(d for _, d in DIL_PATTERNS)
N_PAT = len(DIL_PATTERNS)
assert all(w == DIL_GROUP * d and DIL_SUPER % w == 0 for w, d in DIL_PATTERNS)


def _dil_band_bias():
    i = np.arange(DIL_GROUP)[:, None]
    j = np.arange(2 * DIL_GROUP)[None, :]
    return jnp.asarray(np.where((j >= i) & (j <= i + DIL_GROUP), 0.0, NEG_BIG), F32)


def _dil_kernel(q_ref, k_ref, v_ref, band_ref, o_ref, num_ref, den_ref, mx_ref):
    sb = pl.program_id(1)
    band = band_ref[...]
    col = lax.broadcasted_iota(jnp.int32, band.shape, 1)
    band_first = jnp.where((sb == 0) & (col < DIL_GROUP), NEG_BIG, band)
    ones = jnp.ones((2 * DIL_GROUP, HEAD_DIM), BF16)

    def rows(start, d):
        return pl.ds(start, DIL_GROUP, stride=d) if d > 1 else pl.ds(start, DIL_GROUP)

    for g, (_, d) in enumerate(DIL_PATTERNS):
        span = DIL_GROUP * d
        for u in range(DIL_SUPER // span):
            t0 = sb * DIL_SUPER + u * span
            prev0 = jnp.maximum(t0 - span, 0) if u == 0 else t0 - span
            bias = band_first if u == 0 else band
            for c in range(d):
                own = rows(u * span + c, d)
                q = q_ref[own, :].astype(BF16)
                k = jnp.concatenate([k_ref[rows(prev0 + c, d), :], k_ref[rows(t0 + c, d), :]],
                                    axis=0).astype(BF16)
                v = jnp.concatenate([v_ref[rows(prev0 + c, d), :], v_ref[rows(t0 + c, d), :]],
                                    axis=0).astype(BF16)
                s = lax.dot_general(q, k, (((1,), (1,)), ((), ())),
                                    preferred_element_type=F32) + bias
                m = jnp.max(s, axis=-1, keepdims=True)
                p = jnp.exp2(s - m).astype(BF16)
                r = jnp.dot(p, jnp.concatenate([v, ones], axis=-1), preferred_element_type=F32)
                num_ref[g, own, :] = r[:, :HEAD_DIM]
                den_ref[g, own, :] = r[:, HEAD_DIM:]
                mx_ref[g, own, :] = jnp.broadcast_to(m, (DIL_GROUP, HEAD_DIM))

    def combine(i, carry):
        r = pl.ds(pl.multiple_of(i * DIL_GROUP, DIL_GROUP), DIL_GROUP)
        mx = [mx_ref[g, r, :] for g in range(N_PAT)]
        top = functools.reduce(jnp.maximum, mx)
        wgt = [jnp.exp2(t - top) for t in mx]
        num = functools.reduce(jnp.add, [wgt[g] * num_ref[g, r, :] for g in range(N_PAT)])
        den = functools.reduce(jnp.add, [wgt[g] * den_ref[g, r, :] for g in range(N_PAT)])
        o_ref[r, :] = (num / den).astype(BF16)
        return carry

    lax.fori_loop(0, DIL_SUPER // DIL_GROUP, combine, 0)


def _dil_attn(qkv, band):
    steps = SEQ // DIL_SUPER
    per_pattern = pltpu.VMEM((N_PAT, DIL_SUPER, HEAD_DIM), F32)
    return pl.pallas_call(
        _dil_kernel,
        grid=(BATCH * N_HEADS, steps),
        in_specs=[
            pl.BlockSpec((None, None, None, DIL_SUPER, HEAD_DIM),
                         lambda bh, sb: (QI, bh // N_HEADS, bh % N_HEADS, sb, 0)),
            pl.BlockSpec((None, None, None, SEQ, HEAD_DIM),
                         lambda bh, sb: (KI, bh // N_HEADS, bh % N_HEADS, 0, 0)),
            pl.BlockSpec((None, None, None, SEQ, HEAD_DIM),
                         lambda bh, sb: (VI, bh // N_HEADS, bh % N_HEADS, 0, 0)),
            pl.BlockSpec((DIL_GROUP, 2 * DIL_GROUP), lambda bh, sb: (0, 0)),
        ],
        out_specs=pl.BlockSpec((DIL_SUPER, HEAD_DIM),
                               lambda bh, sb: ((bh // N_HEADS) * steps + sb, bh % N_HEADS)),
        out_shape=jax.ShapeDtypeStruct((N_TOK, D_ATT), BF16),
        scratch_shapes=[per_pattern, per_pattern, per_pattern],
        compiler_params=pltpu.CompilerParams(
            dimension_semantics=("parallel", "arbitrary"), vmem_limit_bytes=48 * MIB),
        name="dil_attn",
    )(qkv, qkv, qkv, band)


MRG_TM = 512
MRG_TJ = 512


def _merge_kernel(x_ref, gain_ref, yd_ref, yf_ref, wgd_ref, wgf_ref, bgd_ref, bgf_ref,
                  wpd_ref, wpf_ref, wo_ref, o_ref, h_ref, acc_ref):
    j = pl.program_id(1)

    @pl.when(j == 0)
    def _():
        h_ref[...] = _rmsnorm(x_ref[...], gain_ref[...]).astype(BF16)
        acc_ref[...] = jnp.zeros_like(acc_ref)

    h = h_ref[...]
    gd = jnp.dot(h, wgd_ref[...], preferred_element_type=F32) + bgd_ref[...]
    gf = jnp.dot(h, wgf_ref[...], preferred_element_type=F32) + bgf_ref[...]
    pd = jnp.dot(yd_ref[...], wpd_ref[...], preferred_element_type=F32)
    pf = jnp.dot(yf_ref[...], wpf_ref[...], preferred_element_type=F32)
    merged = jax.nn.sigmoid(gd) * pd + jax.nn.sigmoid(gf) * pf
    acc_ref[...] += jnp.dot(merged.astype(BF16), wo_ref[...], preferred_element_type=F32)

    @pl.when(j == pl.num_programs(1) - 1)
    def _():
        o_ref[...] = x_ref[...] + acc_ref[...]


def _merge(x, gain, yd, yf, wgd, wgf, bgd, bgf, wpd, wpf, wo):
    grid = (N_TOK // MRG_TM, D_MODEL // MRG_TJ)
    return pl.pallas_call(
        _merge_kernel,
        grid=grid,
        in_specs=[
            pl.BlockSpec((MRG_TM, D_MODEL), lambda i, j: (i, 0)),
            pl.BlockSpec((1, D_MODEL), lambda i, j: (0, 0)),
            pl.BlockSpec((MRG_TM, D_ATT), lambda i, j: (i, 0)),
            pl.BlockSpec((MRG_TM, D_ATT), lambda i, j: (i, 0)),
            pl.BlockSpec((D_MODEL, MRG_TJ), lambda i, j: (0, j)),
            pl.BlockSpec((D_MODEL, MRG_TJ), lambda i, j: (0, j)),
            pl.BlockSpec((1, MRG_TJ), lambda i, j: (0, j)),
            pl.BlockSpec((1, MRG_TJ), lambda i, j: (0, j)),
            pl.BlockSpec((D_ATT, MRG_TJ), lambda i, j: (0, j)),
            pl.BlockSpec((D_ATT, MRG_TJ), lambda i, j: (0, j)),
            pl.BlockSpec((MRG_TJ, D_MODEL), lambda i, j: (j, 0)),
        ],
        out_specs=pl.BlockSpec((MRG_TM, D_MODEL), lambda i, j: (i, 0)),
        out_shape=jax.ShapeDtypeStruct((N_TOK, D_MODEL), F32),
        scratch_shapes=[pltpu.VMEM((MRG_TM, D_MODEL), BF16), pltpu.VMEM((MRG_TM, D_MODEL), F32)],
        compiler_params=pltpu.CompilerParams(
            dimension_semantics=("parallel", "arbitrary"), vmem_limit_bytes=56 * MIB),
        name="merge_proj",
    )(x, gain, yd, yf, wgd, wgf, bgd, bgf, wpd, wpf, wo)


def _rope_tables():
    half = ROPE_DIM // 2
    pos = jnp.arange(SEQ, dtype=F32)
    inv_freq = ROPE_THETA ** (-jnp.arange(0, ROPE_DIM, 2, dtype=F32) / ROPE_DIM)
    ang = pos[:, None] * inv_freq[None, :]
    cos, sin = jnp.cos(ang), jnp.sin(ang)
    rest = HEAD_DIM - ROPE_DIM
    cos_full = jnp.concatenate([cos, cos, jnp.ones((SEQ, rest), F32)], axis=-1)
    sin_signed = jnp.concatenate([-sin, sin, jnp.zeros((SEQ, rest), F32)], axis=-1)
    assert cos_full.shape == (SEQ, HEAD_DIM) and half * 2 == ROPE_DIM
    return cos_full, sin_signed


def kernel(x, ffn1_norm, ffn1_w_gate, ffn1_w_up, ffn1_w_down, mix_norm, w_in, b_forget, b_gate_dil, b_gate_fox, w_proj_dil, w_proj_fox, w_out, ffn2_norm, ffn2_w_gate, ffn2_w_up, ffn2_w_down, final_norm):
    assert x.shape == (BATCH, SEQ, D_MODEL) and x.dtype == F32
    cos_full, sin_signed = _rope_tables()
    dil_band = _dil_band_bias()
    fgain = final_norm.reshape(1, D_MODEL)
    c_qkv = 2 * N_MIX_QKV * D_ATT
    c_gd = c_qkv + N_HEADS

    xt = x.reshape(N_TOK, D_MODEL)
    for l in range(DEPTH):
        last = l == DEPTH - 1
        w_in_l = w_in[l]
        w_qkv = w_in_l[:, :c_qkv].astype(BF16)
        w_f = jnp.pad(w_in_l[:, c_qkv:c_gd], ((0, 0), (0, LANES - N_HEADS))).astype(BF16)
        b_f = jnp.pad(b_forget[l], (0, LANES - N_HEADS)).reshape(1, LANES)
        w_gd = w_in_l[:, c_gd:c_gd + D_MODEL].astype(BF16)
        w_gf = w_in_l[:, c_gd + D_MODEL:].astype(BF16)

        xt = _ffn(xt, ffn1_norm[l].reshape(1, D_MODEL), ffn1_w_gate[l].astype(BF16),
                  ffn1_w_up[l].astype(BF16), ffn1_w_down[l].astype(BF16), fgain, False)

        qkv_dil, qkv_fox, logf = _qkv_proj(xt, mix_norm[l].reshape(1, D_MODEL), w_qkv, w_f, b_f,
                                           cos_full, sin_signed)
        c_gate = _cumsum(logf.reshape(BATCH, SEQ, LANES))
        y_fox = _fox_attn(qkv_fox, c_gate)
        y_dil = _dil_attn(qkv_dil, dil_band)

        xt = _merge(xt, mix_norm[l].reshape(1, D_MODEL), y_dil, y_fox, w_gd, w_gf,
                    b_gate_dil[l].reshape(1, D_MODEL), b_gate_fox[l].reshape(1, D_MODEL),
                    w_proj_dil[l].astype(BF16), w_proj_fox[l].astype(BF16),
                    w_out[l].astype(BF16))

        xt = _ffn(xt, ffn2_norm[l].reshape(1, D_MODEL), ffn2_w_gate[l].astype(BF16),
                  ffn2_w_up[l].astype(BF16), ffn2_w_down[l].astype(BF16), fgain, last)
    return xt.reshape(BATCH, SEQ, D_MODEL)
```

```python
import functools

import numpy as np
import jax
import jax.numpy as jnp
from jax import lax
from jax.experimental import pallas as pl
from jax.experimental.pallas import tpu as pltpu

D_MODEL = 2048
BATCH = 2
SEQ = 8192
DEPTH = 1
HEAD_DIM = 128
N_HEADS = 8
D_ATT = N_HEADS * HEAD_DIM
DIL_PATTERNS = ((128, 1), (512, 4), (2048, 16))
MAX_WINDOW = 2048
ROPE_THETA = 500000.0
ROPE_DIM = HEAD_DIM // 4
D_FF = 5632
NORM_EPS = 1e-6
N_TOK = BATCH * SEQ
LANES = 128

NEG_BIG = -1e30
LOG2E = 1.4426950408889634
MIB = 1024 * 1024

F32 = jnp.float32
BF16 = jnp.bfloat16


def _rmsnorm(x, gain):
    ms = jnp.mean(x * x, axis=-1, keepdims=True)
    return x * lax.rsqrt(ms + NORM_EPS) * gain


FFN_TM = 1024
FFN_TF = 512
FFN_STEPS = D_FF // FFN_TF


def _ffn_kernel(x_ref, gain_ref, wg_ref, wu_ref, wd_ref, fgain_ref, o_ref, h_ref, a_ref,
                *, final_norm):
    j = pl.program_id(1)

    def gate_up():
        h = h_ref[...]
        g = jnp.dot(h, wg_ref[...], preferred_element_type=F32)
        u = jnp.dot(h, wu_ref[...], preferred_element_type=F32)
        a_ref[j % 2] = ((g * jax.nn.sigmoid(g)) * u).astype(BF16)

    def down():
        o_ref[...] += jnp.dot(a_ref[(j + 1) % 2], wd_ref[...], preferred_element_type=F32)

    @pl.when(j == 0)
    def _():
        h_ref[...] = _rmsnorm(x_ref[...], gain_ref[...]).astype(BF16)
        o_ref[...] = jnp.zeros_like(o_ref)
        gate_up()

    @pl.when((j > 0) & (j < FFN_STEPS))
    def _():
        down()
        gate_up()

    @pl.when(j == FFN_STEPS)
    def _():
        down()
        y = x_ref[...] + 0.5 * o_ref[...]
        if final_norm:
            y = _rmsnorm(y, fgain_ref[...])
        o_ref[...] = y


def _ffn(x, gain, wg, wu, wd, fgain, final_norm):
    grid = (N_TOK // FFN_TM, FFN_STEPS + 1)
    last = FFN_STEPS - 1
    return pl.pallas_call(
        functools.partial(_ffn_kernel, final_norm=final_norm),
        grid=grid,
        in_specs=[
            pl.BlockSpec((FFN_TM, D_MODEL), lambda i, j: (i, 0)),
            pl.BlockSpec((1, D_MODEL), lambda i, j: (0, 0)),
            pl.BlockSpec((D_MODEL, FFN_TF), lambda i, j: (0, jnp.minimum(j, last))),
            pl.BlockSpec((D_MODEL, FFN_TF), lambda i, j: (0, jnp.minimum(j, last))),
            pl.BlockSpec((FFN_TF, D_MODEL), lambda i, j: (jnp.maximum(j - 1, 0), 0)),
            pl.BlockSpec((1, D_MODEL), lambda i, j: (0, 0)),
        ],
        out_specs=pl.BlockSpec((FFN_TM, D_MODEL), lambda i, j: (i, 0)),
        out_shape=jax.ShapeDtypeStruct((N_TOK, D_MODEL), F32),
        scratch_shapes=[pltpu.VMEM((FFN_TM, D_MODEL), BF16),
                        pltpu.VMEM((2, FFN_TM, FFN_TF), BF16)],
        compiler_params=pltpu.CompilerParams(
            dimension_semantics=("parallel", "arbitrary"), vmem_limit_bytes=58 * MIB),
        name="ffn",
    )(x, gain, wg, wu, wd, fgain)


QKV_TM = 1024
QKV_TN = 512
HEADS_PER_TILE = QKV_TN // HEAD_DIM
N_MIX_QKV = 3
QI, KI, VI = range(N_MIX_QKV)
Q_DIL, K_DIL, V_DIL, Q_FOX, K_FOX, V_FOX = range(2 * N_MIX_QKV)


ROPE_HALF = ROPE_DIM // 2
ROPE_PARTNER = LANES // 2


def _rope_lane_order():
    order = np.arange(HEAD_DIM)
    second = np.arange(ROPE_HALF, ROPE_DIM)
    moved = np.arange(ROPE_PARTNER, ROPE_PARTNER + ROPE_HALF)
    order[moved], order[second] = second, moved
    return order


def _rope(x, cos, sin_signed):
    return x * cos + pltpu.roll(x, ROPE_PARTNER, 1) * sin_signed


def _qkv_kernel(x_ref, gain_ref, w_ref, wf_ref, bf_ref, cos_ref, sin_ref,
                dil_ref, fox_ref, logf_ref, h_ref, r_ref):
    n = pl.program_id(1)
    scale = (HEAD_DIM ** -0.5) * LOG2E
    tiles_per_which = D_ATT // QKV_TN
    n_dil = N_MIX_QKV * tiles_per_which
    which_prev = (n - 1) // tiles_per_which

    def matmul():
        r_ref[...] = jnp.dot(h_ref[...], w_ref[...], preferred_element_type=F32)

    def prev_heads():
        r = r_ref[...]
        return [r[:, hh * HEAD_DIM:(hh + 1) * HEAD_DIM] for hh in range(HEADS_PER_TILE)]

    def finish_dil():
        rotary = which_prev < V_DIL
        s = jnp.where(which_prev == Q_DIL, scale, 1.0)
        cos = jnp.where(rotary, cos_ref[...], 1.0) * s
        sin = jnp.where(rotary, sin_ref[...], 0.0) * s
        for hh, t in enumerate(prev_heads()):
            dil_ref[hh] = _rope(t, cos, sin)

    def finish_fox():
        s = jnp.where(which_prev == Q_FOX, scale, 1.0)
        for hh, t in enumerate(prev_heads()):
            fox_ref[hh] = (t * s).astype(BF16)

    @pl.when(n == 0)
    def _():
        h = _rmsnorm(x_ref[...], gain_ref[...]).astype(BF16)
        h_ref[...] = h
        f = jnp.dot(h, wf_ref[...], preferred_element_type=F32) + bf_ref[...]
        logf_ref[...] = jnp.minimum(f, 0.0) - jnp.log1p(jnp.exp(-jnp.abs(f)))
        matmul()

    @pl.when((n >= 1) & (n <= n_dil))
    def _():
        finish_dil()
        matmul()

    @pl.when((n > n_dil) & (n < 2 * n_dil))
    def _():
        finish_fox()
        matmul()

    @pl.when(n == 2 * n_dil)
    def _():
        finish_fox()


def _qkv_proj(x, gain, w_qkv, w_f, b_f, cos, sin_signed):
    tiles_per_which = D_ATT // QKV_TN
    n_dil = N_MIX_QKV * tiles_per_which
    n_all = 2 * n_dil
    s_tiles = SEQ // QKV_TM
    grid = (N_TOK // QKV_TM, n_all + 1)

    def head_block(m, tile):
        return (tile // tiles_per_which, m // s_tiles, tile % tiles_per_which, m % s_tiles, 0)

    return pl.pallas_call(
        _qkv_kernel,
        grid=grid,
        in_specs=[
            pl.BlockSpec((QKV_TM, D_MODEL), lambda m, n: (m, 0)),
            pl.BlockSpec((1, D_MODEL), lambda m, n: (0, 0)),
            pl.BlockSpec((D_MODEL, QKV_TN), lambda m, n: (0, jnp.minimum(n, n_all - 1))),
            pl.BlockSpec((D_MODEL, LANES), lambda m, n: (0, 0)),
            pl.BlockSpec((1, LANES), lambda m, n: (0, 0)),
            pl.BlockSpec((QKV_TM, LANES), lambda m, n: (m % s_tiles, 0)),
            pl.BlockSpec((QKV_TM, LANES), lambda m, n: (m % s_tiles, 0)),
        ],
        out_specs=[
            pl.BlockSpec((None, None, HEADS_PER_TILE, QKV_TM, HEAD_DIM),
                         lambda m, n: head_block(m, jnp.clip(n - 1, 0, n_dil - 1))),
            pl.BlockSpec((None, None, HEADS_PER_TILE, QKV_TM, HEAD_DIM),
                         lambda m, n: head_block(m, jnp.clip(n - 1, n_dil, n_all - 1) - n_dil)),
            pl.BlockSpec((QKV_TM, LANES), lambda m, n: (m, 0)),
        ],
        out_shape=[
            jax.ShapeDtypeStruct((N_MIX_QKV, BATCH, N_HEADS, SEQ, HEAD_DIM), F32),
            jax.ShapeDtypeStruct((N_MIX_QKV, BATCH, N_HEADS, SEQ, HEAD_DIM), BF16),
            jax.ShapeDtypeStruct((N_TOK, LANES), F32),
        ],
        scratch_shapes=[pltpu.VMEM((QKV_TM, D_MODEL), BF16),
                        pltpu.VMEM((QKV_TM, QKV_TN), F32)],
        compiler_params=pltpu.CompilerParams(
            dimension_semantics=("parallel", "arbitrary"), vmem_limit_bytes=48 * MIB),
        name="qkv_proj",
    )(x, gain, w_qkv, w_f, b_f, cos, sin_signed)


CUM_TS = 1024
CUM_CHUNK = 128


def _cumsum_kernel(lf_ref, c_ref, carry_ref):
    @pl.when(pl.program_id(1) == 0)
    def _():
        carry_ref[...] = jnp.zeros_like(carry_ref)

    r = lax.broadcasted_iota(jnp.int32, (CUM_CHUNK, CUM_CHUNK), 0)
    c = lax.broadcasted_iota(jnp.int32, (CUM_CHUNK, CUM_CHUNK), 1)
    tri = (c <= r).astype(F32)
    carry = carry_ref[...]
    for i in range(CUM_TS // CUM_CHUNK):
        rows = slice(i * CUM_CHUNK, (i + 1) * CUM_CHUNK)
        cs = jnp.dot(tri, lf_ref[rows, :], preferred_element_type=F32,
                     precision=lax.Precision.HIGHEST) + carry
        c_ref[rows, :] = cs
        carry = cs[CUM_CHUNK - 1:CUM_CHUNK, :]
    carry_ref[...] = carry


def _cumsum(logf):
    return pl.pallas_call(
        _cumsum_kernel,
        grid=(BATCH, SEQ // CUM_TS),
        in_specs=[pl.BlockSpec((None, CUM_TS, LANES), lambda b, s: (b, s, 0))],
        out_specs=pl.BlockSpec((None, CUM_TS, LANES), lambda b, s: (b, s, 0)),
        out_shape=jax.ShapeDtypeStruct((BATCH, SEQ, LANES), F32),
        scratch_shapes=[pltpu.VMEM((1, LANES), F32)],
        compiler_params=pltpu.CompilerParams(dimension_semantics=("parallel", "arbitrary")),
        name="gate_cumsum",
    )(logf)


FOX_TQ = 1024
FOX_TK = 512
BF16_ROWS = 16
FOX_VT_ROWS = HEAD_DIM + BF16_ROWS
N_SPLIT = 3


def _gate_columns(c_tile, head, key_side):
    lane = lax.broadcasted_iota(jnp.int32, c_tile.shape, 1)
    c = jnp.sum(jnp.where(lane == head, c_tile, 0.0), axis=-1, keepdims=True) * LOG2E
    if key_side:
        c = -c
    base = N_SPLIT if key_side else 0
    cols = jnp.where((lane >= N_SPLIT - base) & (lane < 2 * N_SPLIT - base), 1.0, 0.0)
    rest = c
    for t in range(N_SPLIT):
        term = rest.astype(BF16).astype(F32)
        cols = jnp.where(lane == base + t, term, cols)
        rest = rest - term
    return cols.astype(BF16)


def _fox_kernel(q_ref, k_ref, v_ref, c_ref, o_ref, kx_ref, vt_ref, st_ref, mx_ref, m_ref, acc_ref):
    head = pl.program_id(0) % N_HEADS
    qi = pl.program_id(1)

    @pl.when(qi == 0)
    def _():
        ones_tile = (lax.broadcasted_iota(jnp.int32, (BF16_ROWS, FOX_TK), 0) == 0).astype(BF16)

        def build(i, carry):
            rows = pl.ds(pl.multiple_of(i * FOX_TK, FOX_TK), FOX_TK)
            kx_ref[rows, :] = _gate_columns(c_ref[rows, :], head, True)
            vt_ref[0:HEAD_DIM, rows] = v_ref[rows, :].astype(F32).T.astype(BF16)
            vt_ref[HEAD_DIM:FOX_VT_ROWS, rows] = ones_tile
            return carry

        lax.fori_loop(0, SEQ // FOX_TK, build, 0)

    q_rows = pl.ds(pl.multiple_of(qi * FOX_TQ, FOX_TQ), FOX_TQ)
    q_aug = jnp.concatenate([q_ref[...], _gate_columns(c_ref[q_rows, :], head, False)], axis=-1)

    def key_rows(kb):
        return pl.ds(pl.multiple_of(kb * FOX_TK, FOX_TK), FOX_TK)

    def scores(kb, slot, q_lo=0):
        rows = key_rows(kb)
        k_aug = jnp.concatenate([k_ref[rows, :], kx_ref[rows, :]], axis=-1)
        st = lax.dot_general(k_aug, q_aug[q_lo:, :], (((1,), (1,)), ((), ())),
                             preferred_element_type=F32)
        st_ref[slot, :, q_lo:] = st
        mx_ref[slot, :, q_lo:] = jnp.max(st, axis=0, keepdims=True)

    def accumulate(kb, slot, q_lo=0, key_lo=None):
        st = st_ref[slot, :, q_lo:]
        if key_lo is None:
            st_max = mx_ref[slot, :, q_lo:]
        else:
            key = lax.broadcasted_iota(jnp.int32, st.shape, 0) + key_lo
            qry = lax.broadcasted_iota(jnp.int32, st.shape, 1) + q_lo
            st = jnp.where(key <= qry, st, NEG_BIG)
            st_max = jnp.max(st, axis=0, keepdims=True)
        m = m_ref[:, q_lo:]
        m_new = jnp.maximum(m, st_max)
        p = jnp.exp2(st - m_new).astype(BF16)
        pv = jnp.dot(vt_ref[:, key_rows(kb)], p, preferred_element_type=F32)
        acc_ref[:, q_lo:] = jnp.exp2(m - m_new) * acc_ref[:, q_lo:] + pv
        m_ref[:, q_lo:] = m_new

    m_ref[...] = jnp.full(m_ref.shape, NEG_BIG, F32)
    acc_ref[...] = jnp.zeros_like(acc_ref)
    scores(0, 0)

    def pair(j, carry):
        scores(2 * j + 1, 1)
        accumulate(2 * j, 0)
        scores(2 * j + 2, 0)
        accumulate(2 * j + 1, 1)
        return carry

    lax.fori_loop(0, qi, pair, 0)
    scores(2 * qi + 1, 1, q_lo=FOX_TK)
    accumulate(2 * qi, 0, key_lo=0)
    accumulate(2 * qi + 1, 1, q_lo=FOX_TK, key_lo=FOX_TK)

    out_t = acc_ref[0:HEAD_DIM, :] / acc_ref[HEAD_DIM:HEAD_DIM + 1, :]
    o_ref[...] = out_t.T.astype(BF16)


def _fox_attn(qkv, c_gate):
    assert FOX_TQ == 2 * FOX_TK
    q_tiles = SEQ // FOX_TQ
    return pl.pallas_call(
        _fox_kernel,
        grid=(BATCH * N_HEADS, q_tiles),
        in_specs=[
            pl.BlockSpec((None, None, None, FOX_TQ, HEAD_DIM),
                         lambda bh, qi: (QI, bh // N_HEADS, bh % N_HEADS, qi, 0)),
            pl.BlockSpec((None, None, None, SEQ, HEAD_DIM),
                         lambda bh, qi: (KI, bh // N_HEADS, bh % N_HEADS, 0, 0)),
            pl.BlockSpec((None, None, None, SEQ, HEAD_DIM),
                         lambda bh, qi: (VI, bh // N_HEADS, bh % N_HEADS, 0, 0)),
            pl.BlockSpec((None, SEQ, LANES), lambda bh, qi: (bh // N_HEADS, 0, 0)),
        ],
        out_specs=pl.BlockSpec((FOX_TQ, HEAD_DIM),
                               lambda bh, qi: ((bh // N_HEADS) * q_tiles + qi, bh % N_HEADS)),
        out_shape=jax.ShapeDtypeStruct((N_TOK, D_ATT), BF16),
        scratch_shapes=[
            pltpu.VMEM((SEQ, LANES), BF16),
            pltpu.VMEM((FOX_VT_ROWS, SEQ), BF16),
            pltpu.VMEM((2, FOX_TK, FOX_TQ), F32),
            pltpu.VMEM((2, 1, FOX_TQ), F32),
            pltpu.VMEM((1, FOX_TQ), F32),
            pltpu.VMEM((FOX_VT_ROWS, FOX_TQ), F32),
        ],
        compiler_params=pltpu.CompilerParams(
            dimension_semantics=("parallel", "arbitrary"), vmem_limit_bytes=48 * MIB),
        name="fox_attn",
    )(qkv, qkv, qkv, c_gate)


DIL_GROUP = 128
DIL_SUPER = DIL_GROUP * max(d for _, d in DIL_PATTERNS)
N_PAT = len(DIL_PATTERNS)
assert all(w == DIL_GROUP * d and DIL_SUPER % w == 0 for w, d in DIL_PATTERNS)


def _dil_band_bias():
    i = np.arange(DIL_GROUP)[:, None]
    j = np.arange(2 * DIL_GROUP)[None, :]
    return jnp.asarray(np.where((j >= i) & (j <= i + DIL_GROUP), 0.0, NEG_BIG), F32)


def _dil_kernel(q_ref, k_ref, v_ref, band_ref, o_ref, num_ref, den_ref, mx_ref):
    sb = pl.program_id(1)
    band = band_ref[...]
    col = lax.broadcasted_iota(jnp.int32, band.shape, 1)
    band_first = jnp.where((sb == 0) & (col < DIL_GROUP), NEG_BIG, band)
    ones = jnp.ones((2 * DIL_GROUP, HEAD_DIM), BF16)

    def rows(start, d):
        return pl.ds(start, DIL_GROUP, stride=d) if d > 1 else pl.ds(start, DIL_GROUP)

    for g, (_, d) in enumerate(DIL_PATTERNS):
        span = DIL_GROUP * d
        for u in range(DIL_SUPER // span):
            t0 = sb * DIL_SUPER + u * span
            prev0 = jnp.maximum(t0 - span, 0) if u == 0 else t0 - span
            bias = band_first if u == 0 else band
            for c in range(d):
                own = rows(u * span + c, d)
                q = q_ref[own, :].astype(BF16)
                k = jnp.concatenate([k_ref[rows(prev0 + c, d), :], k_ref[rows(t0 + c, d), :]],
                                    axis=0).astype(BF16)
                v = jnp.concatenate([v_ref[rows(prev0 + c, d), :], v_ref[rows(t0 + c, d), :]],
                                    axis=0).astype(BF16)
                s = lax.dot_general(q, k, (((1,), (1,)), ((), ())),
                                    preferred_element_type=F32) + bias
                m = jnp.max(s, axis=-1, keepdims=True)
                p = jnp.exp2(s - m).astype(BF16)
                r = jnp.dot(p, jnp.concatenate([v, ones], axis=-1), preferred_element_type=F32)
                num_ref[g, own, :] = r[:, :HEAD_DIM]
                den_ref[g, own, :] = r[:, HEAD_DIM:]
                mx_ref[g, own, :] = jnp.broadcast_to(m, (DIL_GROUP, HEAD_DIM))

    def combine(i, carry):
        r = pl.ds(pl.multiple_of(i * DIL_GROUP, DIL_GROUP), DIL_GROUP)
        mx = [mx_ref[g, r, :] for g in range(N_PAT)]
        top = functools.reduce(jnp.maximum, mx)
        wgt = [jnp.exp2(t - top) for t in mx]
        num = functools.reduce(jnp.add, [wgt[g] * num_ref[g, r, :] for g in range(N_PAT)])
        den = functools.reduce(jnp.add, [wgt[g] * den_ref[g, r, :] for g in range(N_PAT)])
        o_ref[r, :] = (num / den).astype(BF16)
        return carry

    lax.fori_loop(0, DIL_SUPER // DIL_GROUP, combine, 0)


def _dil_attn(qkv, band):
    steps = SEQ // DIL_SUPER
    per_pattern = pltpu.VMEM((N_PAT, DIL_SUPER, HEAD_DIM), F32)
    return pl.pallas_call(
        _dil_kernel,
        grid=(BATCH * N_HEADS, steps),
        in_specs=[
            pl.BlockSpec((None, None, None, DIL_SUPER, HEAD_DIM),
                         lambda bh, sb: (QI, bh // N_HEADS, bh % N_HEADS, sb, 0)),
            pl.BlockSpec((None, None, None, SEQ, HEAD_DIM),
                         lambda bh, sb: (KI, bh // N_HEADS, bh % N_HEADS, 0, 0)),
            pl.BlockSpec((None, None, None, SEQ, HEAD_DIM),
                         lambda bh, sb: (VI, bh // N_HEADS, bh % N_HEADS, 0, 0)),
            pl.BlockSpec((DIL_GROUP, 2 * DIL_GROUP), lambda bh, sb: (0, 0)),
        ],
        out_specs=pl.BlockSpec((DIL_SUPER, HEAD_DIM),
                               lambda bh, sb: ((bh // N_HEADS) * steps + sb, bh % N_HEADS)),
        out_shape=jax.ShapeDtypeStruct((N_TOK, D_ATT), BF16),
        scratch_shapes=[per_pattern, per_pattern, per_pattern],
        compiler_params=pltpu.CompilerParams(
            dimension_semantics=("parallel", "arbitrary"), vmem_limit_bytes=48 * MIB),
        name="dil_attn",
    )(qkv, qkv, qkv, band)


MRG_TM = 512
MRG_TJ = 512


def _merge_kernel(x_ref, gain_ref, yd_ref, yf_ref, wgd_ref, wgf_ref, bgd_ref, bgf_ref,
                  wpd_ref, wpf_ref, wo_ref, o_ref, h_ref, acc_ref):
    j = pl.program_id(1)

    @pl.when(j == 0)
    def _():
        h_ref[...] = _rmsnorm(x_ref[...], gain_ref[...]).astype(BF16)
        acc_ref[...] = jnp.zeros_like(acc_ref)

    h = h_ref[...]
    gd = jnp.dot(h, wgd_ref[...], preferred_element_type=F32) + bgd_ref[...]
    gf = jnp.dot(h, wgf_ref[...], preferred_element_type=F32) + bgf_ref[...]
    pd = jnp.dot(yd_ref[...], wpd_ref[...], preferred_element_type=F32)
    pf = jnp.dot(yf_ref[...], wpf_ref[...], preferred_element_type=F32)
    merged = jax.nn.sigmoid(gd) * pd + jax.nn.sigmoid(gf) * pf
    acc_ref[...] += jnp.dot(merged.astype(BF16), wo_ref[...], preferred_element_type=F32)

    @pl.when(j == pl.num_programs(1) - 1)
    def _():
        o_ref[...] = x_ref[...] + acc_ref[...]


def _merge(x, gain, yd, yf, wgd, wgf, bgd, bgf, wpd, wpf, wo):
    grid = (N_TOK // MRG_TM, D_MODEL // MRG_TJ)
    return pl.pallas_call(
        _merge_kernel,
        grid=grid,
        in_specs=[
            pl.BlockSpec((MRG_TM, D_MODEL), lambda i, j: (i, 0)),
            pl.BlockSpec((1, D_MODEL), lambda i, j: (0, 0)),
            pl.BlockSpec((MRG_TM, D_ATT), lambda i, j: (i, 0)),
            pl.BlockSpec((MRG_TM, D_ATT), lambda i, j: (i, 0)),
            pl.BlockSpec((D_MODEL, MRG_TJ), lambda i, j: (0, j)),
            pl.BlockSpec((D_MODEL, MRG_TJ), lambda i, j: (0, j)),
            pl.BlockSpec((1, MRG_TJ), lambda i, j: (0, j)),
            pl.BlockSpec((1, MRG_TJ), lambda i, j: (0, j)),
            pl.BlockSpec((D_ATT, MRG_TJ), lambda i, j: (0, j)),
            pl.BlockSpec((D_ATT, MRG_TJ), lambda i, j: (0, j)),
            pl.BlockSpec((MRG_TJ, D_MODEL), lambda i, j: (j, 0)),
        ],
        out_specs=pl.BlockSpec((MRG_TM, D_MODEL), lambda i, j: (i, 0)),
        out_shape=jax.ShapeDtypeStruct((N_TOK, D_MODEL), F32),
        scratch_shapes=[pltpu.VMEM((MRG_TM, D_MODEL), BF16), pltpu.VMEM((MRG_TM, D_MODEL), F32)],
        compiler_params=pltpu.CompilerParams(
            dimension_semantics=("parallel", "arbitrary"), vmem_limit_bytes=56 * MIB),
        name="merge_proj",
    )(x, gain, yd, yf, wgd, wgf, bgd, bgf, wpd, wpf, wo)


def _rope_tables():
    pos = jnp.arange(SEQ, dtype=F32)
    inv_freq = ROPE_THETA ** (-jnp.arange(0, ROPE_DIM, 2, dtype=F32) / ROPE_DIM)
    ang = pos[:, None] * inv_freq[None, :]
    cos, sin = jnp.cos(ang), jnp.sin(ang)
    gap = ROPE_PARTNER - ROPE_HALF
    tail = HEAD_DIM - ROPE_PARTNER - ROPE_HALF
    one, zero = (lambda n: jnp.ones((SEQ, n), F32)), (lambda n: jnp.zeros((SEQ, n), F32))
    cos_full = jnp.concatenate([cos, one(gap), cos, one(tail)], axis=-1)
    sin_signed = jnp.concatenate([-sin, zero(gap), sin, zero(tail)], axis=-1)
    assert cos_full.shape == (SEQ, HEAD_DIM)
    return cos_full, sin_signed


def _reorder_head_dims(w, order):
    runs, start = [], 0
    for i in range(1, HEAD_DIM + 1):
        if i == HEAD_DIM or order[i] != order[i - 1] + 1:
            runs.append((int(order[start]), i - start))
            start = i
    w3 = w.reshape(w.shape[0], -1, HEAD_DIM)
    w3 = jnp.concatenate([w3[:, :, lo:lo + n] for lo, n in runs], axis=-1)
    return w3.reshape(w.shape)


def kernel(x, ffn1_norm, ffn1_w_gate, ffn1_w_up, ffn1_w_down, mix_norm, w_in, b_forget, b_gate_dil, b_gate_fox, w_proj_dil, w_proj_fox, w_out, ffn2_norm, ffn2_w_gate, ffn2_w_up, ffn2_w_down, final_norm):
    assert x.shape == (BATCH, SEQ, D_MODEL) and x.dtype == F32
    cos_full, sin_signed = _rope_tables()
    rope_order = _rope_lane_order()
    dil_band = _dil_band_bias()
    fgain = final_norm.reshape(1, D_MODEL)
    c_qkv = 2 * N_MIX_QKV * D_ATT
    c_gd = c_qkv + N_HEADS

    xt = x.reshape(N_TOK, D_MODEL)
    for l in range(DEPTH):
        last = l == DEPTH - 1
        w_in_l = w_in[l]
        c_rot = 2 * D_ATT
        w_qkv = jnp.concatenate([_reorder_head_dims(w_in_l[:, :c_rot], rope_order),
                                 w_in_l[:, c_rot:c_qkv]], axis=1).astype(BF16)
        w_f = jnp.pad(w_in_l[:, c_qkv:c_gd], ((0, 0), (0, LANES - N_HEADS))).astype(BF16)
        b_f = jnp.pad(b_forget[l], (0, LANES - N_HEADS)).reshape(1, LANES)
        w_gd = w_in_l[:, c_gd:c_gd + D_MODEL].astype(BF16)
        w_gf = w_in_l[:, c_gd + D_MODEL:].astype(BF16)

        xt = _ffn(xt, ffn1_norm[l].reshape(1, D_MODEL), ffn1_w_gate[l].astype(BF16),
                  ffn1_w_up[l].astype(BF16), ffn1_w_down[l].astype(BF16), fgain, False)

        qkv_dil, qkv_fox, logf = _qkv_proj(xt, mix_norm[l].reshape(1, D_MODEL), w_qkv, w_f, b_f,
                                           cos_full, sin_signed)
        c_gate = _cumsum(logf.reshape(BATCH, SEQ, LANES))
        y_fox = _fox_attn(qkv_fox, c_gate)
        y_dil = _dil_attn(qkv_dil, dil_band)

        xt = _merge(xt, mix_norm[l].reshape(1, D_MODEL), y_dil, y_fox, w_gd, w_gf,
                    b_gate_dil[l].reshape(1, D_MODEL), b_gate_fox[l].reshape(1, D_MODEL),
                    w_proj_dil[l].astype(BF16), w_proj_fox[l].astype(BF16),
                    w_out[l].astype(BF16))

        xt = _ffn(xt, ffn2_norm[l].reshape(1, D_MODEL), ffn2_w_gate[l].astype(BF16),
                  ffn2_w_up[l].astype(BF16), ffn2_w_down[l].astype(BF16), fgain, last)
    return xt.reshape(BATCH, SEQ, D_MODEL)
```

```python
import functools

import numpy as np
import jax
import jax.numpy as jnp
from jax import lax
from jax.experimental import pallas as pl
from jax.experimental.pallas import tpu as pltpu

D_MODEL = 2048
BATCH = 2
SEQ = 8192
DEPTH = 1
HEAD_DIM = 128
N_HEADS = 8
D_ATT = N_HEADS * HEAD_DIM
DIL_PATTERNS = ((128, 1), (512, 4), (2048, 16))
MAX_WINDOW = 2048
ROPE_THETA = 500000.0
ROPE_DIM = HEAD_DIM // 4
D_FF = 5632
NORM_EPS = 1e-6
N_TOK = BATCH * SEQ
LANES = 128

NEG_BIG = -1e30
LOG2E = 1.4426950408889634
MIB = 1024 * 1024

F32 = jnp.float32
BF16 = jnp.bfloat16


def _rmsnorm(x, gain):
    ms = jnp.mean(x * x, axis=-1, keepdims=True)
    return x * lax.rsqrt(ms + NORM_EPS) * gain


FFN_TM = 1024
FFN_TF = 512
FFN_STEPS = D_FF // FFN_TF


def _ffn_kernel(x_ref, gain_ref, wg_ref, wu_ref, wd_ref, wd_last_ref, fgain_ref, o_ref, h_ref, a_ref,
                *, final_norm):
    j = pl.program_id(1)
    last = FFN_STEPS - 1

    def gate_up():
        h = h_ref[...]
        g = jnp.dot(h, wg_ref[...], preferred_element_type=F32)
        u = jnp.dot(h, wu_ref[...], preferred_element_type=F32)
        a_ref[j % 2] = ((g * jax.nn.sigmoid(g)) * u).astype(BF16)

    def down():
        o_ref[...] += jnp.dot(a_ref[(j + 1) % 2], wd_ref[...], preferred_element_type=F32)

    @pl.when(j == 0)
    def _():
        h_ref[...] = _rmsnorm(x_ref[...], gain_ref[...]).astype(BF16)
        o_ref[...] = jnp.zeros_like(o_ref)
        gate_up()

    @pl.when((j > 0) & (j < last))
    def _():
        down()
        gate_up()

    @pl.when(j == last)
    def _():
        down()
        gate_up()
        o_ref[...] += jnp.dot(a_ref[last % 2], wd_last_ref[...], preferred_element_type=F32)
        y = x_ref[...] + 0.5 * o_ref[...]
        if final_norm:
            y = _rmsnorm(y, fgain_ref[...])
        o_ref[...] = y


def _ffn(x, gain, wg, wu, wd, fgain, final_norm):
    grid = (N_TOK // FFN_TM, FFN_STEPS)
    last = FFN_STEPS - 1
    return pl.pallas_call(
        functools.partial(_ffn_kernel, final_norm=final_norm),
        grid=grid,
        in_specs=[
            pl.BlockSpec((FFN_TM, D_MODEL), lambda i, j: (i, 0)),
            pl.BlockSpec((1, D_MODEL), lambda i, j: (0, 0)),
            pl.BlockSpec((None, D_MODEL, FFN_TF), lambda i, j: (j, 0, 0)),
            pl.BlockSpec((None, D_MODEL, FFN_TF), lambda i, j: (j, 0, 0)),
            pl.BlockSpec((FFN_TF, D_MODEL), lambda i, j: (jnp.maximum(j - 1, 0), 0)),
            pl.BlockSpec((FFN_TF, D_MODEL), lambda i, j: (last, 0), pipeline_mode=pl.Buffered(1)),
            pl.BlockSpec((1, D_MODEL), lambda i, j: (0, 0)),
        ],
        out_specs=pl.BlockSpec((FFN_TM, D_MODEL), lambda i, j: (i, 0)),
        out_shape=jax.ShapeDtypeStruct((N_TOK, D_MODEL), F32),
        scratch_shapes=[pltpu.VMEM((FFN_TM, D_MODEL), BF16),
                        pltpu.VMEM((2, FFN_TM, FFN_TF), BF16)],
        compiler_params=pltpu.CompilerParams(
            dimension_semantics=("parallel", "arbitrary"), vmem_limit_bytes=61 * MIB),
        name="ffn",
    )(x, gain, wg, wu, wd, wd, fgain)


QKV_TM = 1024
N_MIX_QKV = 3
QI, KI, VI = range(N_MIX_QKV)
N_QKV_TILES = 2 * N_MIX_QKV


ROPE_HALF = ROPE_DIM // 2
ROPE_PARTNER = LANES // 2


def _rope_lane_order():
    order = np.arange(HEAD_DIM)
    second = np.arange(ROPE_HALF, ROPE_DIM)
    moved = np.arange(ROPE_PARTNER, ROPE_PARTNER + ROPE_HALF)
    order[moved], order[second] = second, moved
    return order


def _rope(x, cos, sin_signed):
    return x * cos + pltpu.roll(x, ROPE_PARTNER, 1) * sin_signed


def _qkv_kernel(x_ref, gain_ref, w_ref, wf_ref, bf_ref, cos_ref, sin_ref,
                dil_ref, fox_ref, logf_ref, h_ref, r_ref):
    n = pl.program_id(1)
    scale = (HEAD_DIM ** -0.5) * LOG2E
    is_q = n <= 2

    def matmul():
        r_ref[...] = jnp.dot(h_ref[...], w_ref[...], preferred_element_type=F32)

    def heads():
        r = r_ref[...]
        return [r[:, hh * HEAD_DIM:(hh + 1) * HEAD_DIM] for hh in range(N_HEADS)]

    def finish_dil(rotary, scaled):
        s = jnp.where(scaled, scale, 1.0)
        cos = jnp.where(rotary, cos_ref[...], 1.0) * s
        sin = jnp.where(rotary, sin_ref[...], 0.0) * s
        for hh, t in enumerate(heads()):
            dil_ref[hh] = _rope(t, cos, sin)

    def finish_fox(scaled):
        s = jnp.where(scaled, scale, 1.0)
        for hh, t in enumerate(heads()):
            fox_ref[hh] = (t * s).astype(BF16)

    @pl.when(n == 0)
    def _():
        h = _rmsnorm(x_ref[...], gain_ref[...]).astype(BF16)
        h_ref[...] = h
        f = jnp.dot(h, wf_ref[...], preferred_element_type=F32) + bf_ref[...]
        logf_ref[...] = jnp.minimum(f, 0.0) - jnp.log1p(jnp.exp(-jnp.abs(f)))
        matmul()

    @pl.when((n == 1) | (n == 3))
    def _():
        finish_dil(True, is_q)
        matmul()

    @pl.when((n == 2) | (n == 4))
    def _():
        finish_fox(is_q)
        matmul()

    @pl.when(n == N_QKV_TILES - 1)
    def _():
        finish_dil(False, False)
        matmul()
        finish_fox(False)


def _qkv_proj(x, gain, w_qkv, w_f, b_f, cos, sin_signed):
    s_tiles = SEQ // QKV_TM
    grid = (N_TOK // QKV_TM, N_QKV_TILES)

    def head_block(m, which):
        return (which, m // s_tiles, 0, m % s_tiles, 0)

    return pl.pallas_call(
        _qkv_kernel,
        grid=grid,
        in_specs=[
            pl.BlockSpec((QKV_TM, D_MODEL), lambda m, n: (m, 0)),
            pl.BlockSpec((1, D_MODEL), lambda m, n: (0, 0)),
            pl.BlockSpec((None, D_MODEL, D_ATT), lambda m, n: (n, 0, 0)),
            pl.BlockSpec((D_MODEL, LANES), lambda m, n: (0, 0)),
            pl.BlockSpec((1, LANES), lambda m, n: (0, 0)),
            pl.BlockSpec((QKV_TM, LANES), lambda m, n: (m % s_tiles, 0)),
            pl.BlockSpec((QKV_TM, LANES), lambda m, n: (m % s_tiles, 0)),
        ],
        out_specs=[
            pl.BlockSpec((None, None, N_HEADS, QKV_TM, HEAD_DIM),
                         lambda m, n: head_block(m, jnp.maximum(n - 1, 0) // 2)),
            pl.BlockSpec((None, None, N_HEADS, QKV_TM, HEAD_DIM),
                         lambda m, n: head_block(m, jnp.maximum(n - 3, 0))),
            pl.BlockSpec((QKV_TM, LANES), lambda m, n: (m, 0)),
        ],
        out_shape=[
            jax.ShapeDtypeStruct((N_MIX_QKV, BATCH, N_HEADS, SEQ, HEAD_DIM), F32),
            jax.ShapeDtypeStruct((N_MIX_QKV, BATCH, N_HEADS, SEQ, HEAD_DIM), BF16),
            jax.ShapeDtypeStruct((N_TOK, LANES), F32),
        ],
        scratch_shapes=[pltpu.VMEM((QKV_TM, D_MODEL), BF16),
                        pltpu.VMEM((QKV_TM, D_ATT), F32)],
        compiler_params=pltpu.CompilerParams(
            dimension_semantics=("parallel", "arbitrary"), vmem_limit_bytes=56 * MIB),
        name="qkv_proj",
    )(x, gain, w_qkv, w_f, b_f, cos, sin_signed)


CUM_TS = 1024
CUM_CHUNK = 128


def _cumsum_kernel(lf_ref, c_ref, carry_ref):
    @pl.when(pl.program_id(1) == 0)
    def _():
        carry_ref[...] = jnp.zeros_like(carry_ref)

    r = lax.broadcasted_iota(jnp.int32, (CUM_CHUNK, CUM_CHUNK), 0)
    c = lax.broadcasted_iota(jnp.int32, (CUM_CHUNK, CUM_CHUNK), 1)
    tri = (c <= r).astype(F32)
    carry = carry_ref[...]
    for i in range(CUM_TS // CUM_CHUNK):
        rows = slice(i * CUM_CHUNK, (i + 1) * CUM_CHUNK)
        cs = jnp.dot(tri, lf_ref[rows, :], preferred_element_type=F32,
                     precision=lax.Precision.HIGHEST) + carry
        c_ref[rows, :] = cs
        carry = cs[CUM_CHUNK - 1:CUM_CHUNK, :]
    carry_ref[...] = carry


def _cumsum(logf):
    return pl.pallas_call(
        _cumsum_kernel,
        grid=(BATCH, SEQ // CUM_TS),
        in_specs=[pl.BlockSpec((None, CUM_TS, LANES), lambda b, s: (b, s, 0))],
        out_specs=pl.BlockSpec((None, CUM_TS, LANES), lambda b, s: (b, s, 0)),
        out_shape=jax.ShapeDtypeStruct((BATCH, SEQ, LANES), F32),
        scratch_shapes=[pltpu.VMEM((1, LANES), F32)],
        compiler_params=pltpu.CompilerParams(dimension_semantics=("parallel", "arbitrary")),
        name="gate_cumsum",
    )(logf)


FOX_TQ = 1024
FOX_TK = 512
BF16_ROWS = 16
FOX_VT_ROWS = HEAD_DIM + BF16_ROWS
N_SPLIT = 3


def _gate_columns(c_tile, head, key_side):
    lane = lax.broadcasted_iota(jnp.int32, c_tile.shape, 1)
    c = jnp.sum(jnp.where(lane == head, c_tile, 0.0), axis=-1, keepdims=True) * LOG2E
    if key_side:
        c = -c
    base = N_SPLIT if key_side else 0
    cols = jnp.where((lane >= N_SPLIT - base) & (lane < 2 * N_SPLIT - base), 1.0, 0.0)
    rest = c
    for t in range(N_SPLIT):
        term = rest.astype(BF16).astype(F32)
        cols = jnp.where(lane == base + t, term, cols)
        rest = rest - term
    return cols.astype(BF16)


def _fox_kernel(q_ref, k_ref, v_ref, c_ref, o_ref, kx_ref, vt_ref, st_ref, mx_ref, m_ref, acc_ref):
    head = pl.program_id(0) % N_HEADS
    qi = pl.program_id(1)

    @pl.when(qi == 0)
    def _():
        ones_tile = (lax.broadcasted_iota(jnp.int32, (BF16_ROWS, FOX_TK), 0) == 0).astype(BF16)

        def build(i, carry):
            rows = pl.ds(pl.multiple_of(i * FOX_TK, FOX_TK), FOX_TK)
            kx_ref[rows, :] = _gate_columns(c_ref[rows, :], head, True)
            vt_ref[0:HEAD_DIM, rows] = v_ref[rows, :].astype(F32).T.astype(BF16)
            vt_ref[HEAD_DIM:FOX_VT_ROWS, rows] = ones_tile
            return carry

        lax.fori_loop(0, SEQ // FOX_TK, build, 0)

    q_rows = pl.ds(pl.multiple_of(qi * FOX_TQ, FOX_TQ), FOX_TQ)
    q_aug = jnp.concatenate([q_ref[...], _gate_columns(c_ref[q_rows, :], head, False)], axis=-1)

    def key_rows(kb):
        return pl.ds(pl.multiple_of(kb * FOX_TK, FOX_TK), FOX_TK)

    def scores(kb, slot, q_lo=0):
        rows = key_rows(kb)
        k_aug = jnp.concatenate([k_ref[rows, :], kx_ref[rows, :]], axis=-1)
        st = lax.dot_general(k_aug, q_aug[q_lo:, :], (((1,), (1,)), ((), ())),
                             preferred_element_type=F32)
        st_ref[slot, :, q_lo:] = st
        mx_ref[slot, :, q_lo:] = jnp.max(st, axis=0, keepdims=True)

    def accumulate(kb, slot, q_lo=0, key_lo=None):
        st = st_ref[slot, :, q_lo:]
        if key_lo is None:
            st_max = mx_ref[slot, :, q_lo:]
        else:
            key = lax.broadcasted_iota(jnp.int32, st.shape, 0) + key_lo
            qry = lax.broadcasted_iota(jnp.int32, st.shape, 1) + q_lo
            st = jnp.where(key <= qry, st, NEG_BIG)
            st_max = jnp.max(st, axis=0, keepdims=True)
        m = m_ref[:, q_lo:]
        m_new = jnp.maximum(m, st_max)
        p = jnp.exp2(st - m_new).astype(BF16)
        pv = jnp.dot(vt_ref[:, key_rows(kb)], p, preferred_element_type=F32)
        acc_ref[:, q_lo:] = jnp.exp2(m - m_new) * acc_ref[:, q_lo:] + pv
        m_ref[:, q_lo:] = m_new

    m_ref[...] = jnp.full(m_ref.shape, NEG_BIG, F32)
    acc_ref[...] = jnp.zeros_like(acc_ref)
    scores(0, 0)

    def pair(j, carry):
        scores(2 * j + 1, 1)
        accumulate(2 * j, 0)
        scores(2 * j + 2, 0)
        accumulate(2 * j + 1, 1)
        return carry

    lax.fori_loop(0, qi, pair, 0)
    scores(2 * qi + 1, 1, q_lo=FOX_TK)
    accumulate(2 * qi, 0, key_lo=0)
    accumulate(2 * qi + 1, 1, q_lo=FOX_TK, key_lo=FOX_TK)

    out_t = acc_ref[0:HEAD_DIM, :] / acc_ref[HEAD_DIM:HEAD_DIM + 1, :]
    o_ref[...] = out_t.T.astype(BF16)


def _fox_attn(qkv, c_gate):
    assert FOX_TQ == 2 * FOX_TK
    q_tiles = SEQ // FOX_TQ
    return pl.pallas_call(
        _fox_kernel,
        grid=(BATCH * N_HEADS, q_tiles),
        in_specs=[
            pl.BlockSpec((None, None, None, FOX_TQ, HEAD_DIM),
                         lambda bh, qi: (QI, bh // N_HEADS, bh % N_HEADS, qi, 0)),
            pl.BlockSpec((None, None, None, SEQ, HEAD_DIM),
                         lambda bh, qi: (KI, bh // N_HEADS, bh % N_HEADS, 0, 0)),
            pl.BlockSpec((None, None, None, SEQ, HEAD_DIM),
                         lambda bh, qi: (VI, bh // N_HEADS, bh % N_HEADS, 0, 0)),
            pl.BlockSpec((None, SEQ, LANES), lambda bh, qi: (bh // N_HEADS, 0, 0)),
        ],
        out_specs=pl.BlockSpec((FOX_TQ, HEAD_DIM),
                               lambda bh, qi: ((bh // N_HEADS) * q_tiles + qi, bh % N_HEADS)),
        out_shape=jax.ShapeDtypeStruct((N_TOK, D_ATT), BF16),
        scratch_shapes=[
            pltpu.VMEM((SEQ, LANES), BF16),
            pltpu.VMEM((FOX_VT_ROWS, SEQ), BF16),
            pltpu.VMEM((2, FOX_TK, FOX_TQ), F32),
            pltpu.VMEM((2, 1, FOX_TQ), F32),
            pltpu.VMEM((1, FOX_TQ), F32),
            pltpu.VMEM((FOX_VT_ROWS, FOX_TQ), F32),
        ],
        compiler_params=pltpu.CompilerParams(
            dimension_semantics=("parallel", "arbitrary"), vmem_limit_bytes=48 * MIB),
        name="fox_attn",
    )(qkv, qkv, qkv, c_gate)


DIL_GROUP = 128
DIL_SUPER = DIL_GROUP * max(d for _, d in DIL_PATTERNS)
N_PAT = len(DIL_PATTERNS)
assert all(w == DIL_GROUP * d and DIL_SUPER % w == 0 for w, d in DIL_PATTERNS)


def _dil_band_bias():
    i = np.arange(DIL_GROUP)[:, None]
    j = np.arange(2 * DIL_GROUP)[None, :]
    return jnp.asarray(np.where((j >= i) & (j <= i + DIL_GROUP), 0.0, NEG_BIG), F32)


def _dil_kernel(q_ref, k_ref, v_ref, band_ref, o_ref, num_ref, den_ref, mx_ref):
    sb = pl.program_id(1)
    band = band_ref[...]
    col = lax.broadcasted_iota(jnp.int32, band.shape, 1)
    band_first = jnp.where((sb == 0) & (col < DIL_GROUP), NEG_BIG, band)
    ones = jnp.ones((2 * DIL_GROUP, HEAD_DIM), BF16)

    def rows(start, d):
        return pl.ds(start, DIL_GROUP, stride=d) if d > 1 else pl.ds(start, DIL_GROUP)

    for g, (_, d) in enumerate(DIL_PATTERNS):
        span = DIL_GROUP * d
        for u in range(DIL_SUPER // span):
            t0 = sb * DIL_SUPER + u * span
            prev0 = jnp.maximum(t0 - span, 0) if u == 0 else t0 - span
            bias = band_first if u == 0 else band
            for c in range(d):
                own = rows(u * span + c, d)
                q = q_ref[own, :].astype(BF16)
                k = jnp.concatenate([k_ref[rows(prev0 + c, d), :], k_ref[rows(t0 + c, d), :]],
                                    axis=0).astype(BF16)
                v = jnp.concatenate([v_ref[rows(prev0 + c, d), :], v_ref[rows(t0 + c, d), :]],
                                    axis=0).astype(BF16)
                s = lax.dot_general(q, k, (((1,), (1,)), ((), ())),
                                    preferred_element_type=F32) + bias
                m = jnp.max(s, axis=-1, keepdims=True)
                p = jnp.exp2(s - m).astype(BF16)
                r = jnp.dot(p, jnp.concatenate([v, ones], axis=-1), preferred_element_type=F32)
                num_ref[g, own, :] = r[:, :HEAD_DIM]
                den_ref[g, own, :] = r[:, HEAD_DIM:]
                mx_ref[g, own, :] = jnp.broadcast_to(m, (DIL_GROUP, HEAD_DIM))

    def combine(i, carry):
        r = pl.ds(pl.multiple_of(i * DIL_GROUP, DIL_GROUP), DIL_GROUP)
        mx = [mx_ref[g, r, :] for g in range(N_PAT)]
        top = functools.reduce(jnp.maximum, mx)
        wgt = [jnp.exp2(t - top) for t in mx]
        num = functools.reduce(jnp.add, [wgt[g] * num_ref[g, r, :] for g in range(N_PAT)])
        den = functools.reduce(jnp.add, [wgt[g] * den_ref[g, r, :] for g in range(N_PAT)])
        o_ref[r, :] = (num / den).astype(BF16)
        return carry

    lax.fori_loop(0, DIL_SUPER // DIL_GROUP, combine, 0)


def _dil_attn(qkv, band):
    steps = SEQ // DIL_SUPER
    per_pattern = pltpu.VMEM((N_PAT, DIL_SUPER, HEAD_DIM), F32)
    return pl.pallas_call(
        _dil_kernel,
        grid=(BATCH * N_HEADS, steps),
        in_specs=[
            pl.BlockSpec((None, None, None, DIL_SUPER, HEAD_DIM),
                         lambda bh, sb: (QI, bh // N_HEADS, bh % N_HEADS, sb, 0)),
            pl.BlockSpec((None, None, None, SEQ, HEAD_DIM),
                         lambda bh, sb: (KI, bh // N_HEADS, bh % N_HEADS, 0, 0)),
            pl.BlockSpec((None, None, None, SEQ, HEAD_DIM),
                         lambda bh, sb: (VI, bh // N_HEADS, bh % N_HEADS, 0, 0)),
            pl.BlockSpec((DIL_GROUP, 2 * DIL_GROUP), lambda bh, sb: (0, 0)),
        ],
        out_specs=pl.BlockSpec((DIL_SUPER, HEAD_DIM),
                               lambda bh, sb: ((bh // N_HEADS) * steps + sb, bh % N_HEADS)),
        out_shape=jax.ShapeDtypeStruct((N_TOK, D_ATT), BF16),
        scratch_shapes=[per_pattern, per_pattern, per_pattern],
        compiler_params=pltpu.CompilerParams(
            dimension_semantics=("parallel", "arbitrary"), vmem_limit_bytes=48 * MIB),
        name="dil_attn",
    )(qkv, qkv, qkv, band)


MRG_TM = 512
MRG_TJ = 512


def _merge_kernel(x_ref, gain_ref, yd_ref, yf_ref, wgd_ref, wgf_ref, bgd_ref, bgf_ref,
                  wpd_ref, wpf_ref, wo_ref, o_ref, h_ref, acc_ref):
    j = pl.program_id(1)

    @pl.when(j == 0)
    def _():
        h_ref[...] = _rmsnorm(x_ref[...], gain_ref[...]).astype(BF16)
        acc_ref[...] = jnp.zeros_like(acc_ref)

    h = h_ref[...]
    gd = jnp.dot(h, wgd_ref[...], preferred_element_type=F32) + bgd_ref[...]
    gf = jnp.dot(h, wgf_ref[...], preferred_element_type=F32) + bgf_ref[...]
    pd = jnp.dot(yd_ref[...], wpd_ref[...], preferred_element_type=F32)
    pf = jnp.dot(yf_ref[...], wpf_ref[...], preferred_element_type=F32)
    merged = jax.nn.sigmoid(gd) * pd + jax.nn.sigmoid(gf) * pf
    acc_ref[...] += jnp.dot(merged.astype(BF16), wo_ref[...], preferred_element_type=F32)

    @pl.when(j == pl.num_programs(1) - 1)
    def _():
        o_ref[...] = x_ref[...] + acc_ref[...]


def _merge(x, gain, yd, yf, wgd, wgf, bgd, bgf, wpd, wpf, wo):
    grid = (N_TOK // MRG_TM, D_MODEL // MRG_TJ)
    return pl.pallas_call(
        _merge_kernel,
        grid=grid,
        in_specs=[
            pl.BlockSpec((MRG_TM, D_MODEL), lambda i, j: (i, 0)),
            pl.BlockSpec((1, D_MODEL), lambda i, j: (0, 0)),
            pl.BlockSpec((MRG_TM, D_ATT), lambda i, j: (i, 0)),
            pl.BlockSpec((MRG_TM, D_ATT), lambda i, j: (i, 0)),
            pl.BlockSpec((None, D_MODEL, MRG_TJ), lambda i, j: (j, 0, 0)),
            pl.BlockSpec((None, D_MODEL, MRG_TJ), lambda i, j: (j, 0, 0)),
            pl.BlockSpec((1, MRG_TJ), lambda i, j: (0, j)),
            pl.BlockSpec((1, MRG_TJ), lambda i, j: (0, j)),
            pl.BlockSpec((None, D_ATT, MRG_TJ), lambda i, j: (j, 0, 0)),
            pl.BlockSpec((None, D_ATT, MRG_TJ), lambda i, j: (j, 0, 0)),
            pl.BlockSpec((MRG_TJ, D_MODEL), lambda i, j: (j, 0)),
        ],
        out_specs=pl.BlockSpec((MRG_TM, D_MODEL), lambda i, j: (i, 0)),
        out_shape=jax.ShapeDtypeStruct((N_TOK, D_MODEL), F32),
        scratch_shapes=[pltpu.VMEM((MRG_TM, D_MODEL), BF16), pltpu.VMEM((MRG_TM, D_MODEL), F32)],
        compiler_params=pltpu.CompilerParams(
            dimension_semantics=("parallel", "arbitrary"), vmem_limit_bytes=56 * MIB),
        name="merge_proj",
    )(x, gain, yd, yf, wgd, wgf, bgd, bgf, wpd, wpf, wo)


def _rope_tables():
    pos = jnp.arange(SEQ, dtype=F32)
    inv_freq = ROPE_THETA ** (-jnp.arange(0, ROPE_DIM, 2, dtype=F32) / ROPE_DIM)
    ang = pos[:, None] * inv_freq[None, :]
    cos, sin = jnp.cos(ang), jnp.sin(ang)
    gap = ROPE_PARTNER - ROPE_HALF
    tail = HEAD_DIM - ROPE_PARTNER - ROPE_HALF
    one, zero = (lambda n: jnp.ones((SEQ, n), F32)), (lambda n: jnp.zeros((SEQ, n), F32))
    cos_full = jnp.concatenate([cos, one(gap), cos, one(tail)], axis=-1)
    sin_signed = jnp.concatenate([-sin, zero(gap), sin, zero(tail)], axis=-1)
    assert cos_full.shape == (SEQ, HEAD_DIM)
    return cos_full, sin_signed


def _column_tiles(w, tile):
    rows, n_cols = w.shape
    return w.astype(BF16).reshape(rows, n_cols // tile, tile).transpose(1, 0, 2)


def _reorder_head_dims(w, order):
    runs, start = [], 0
    for i in range(1, HEAD_DIM + 1):
        if i == HEAD_DIM or order[i] != order[i - 1] + 1:
            runs.append((int(order[start]), i - start))
            start = i
    w3 = w.reshape(w.shape[0], -1, HEAD_DIM)
    w3 = jnp.concatenate([w3[:, :, lo:lo + n] for lo, n in runs], axis=-1)
    return w3.reshape(w.shape)


def kernel(x, ffn1_norm, ffn1_w_gate, ffn1_w_up, ffn1_w_down, mix_norm, w_in, b_forget, b_gate_dil, b_gate_fox, w_proj_dil, w_proj_fox, w_out, ffn2_norm, ffn2_w_gate, ffn2_w_up, ffn2_w_down, final_norm):
    assert x.shape == (BATCH, SEQ, D_MODEL) and x.dtype == F32
    cos_full, sin_signed = _rope_tables()
    rope_order = _rope_lane_order()
    dil_band = _dil_band_bias()
    fgain = final_norm.reshape(1, D_MODEL)
    c_qkv = 2 * N_MIX_QKV * D_ATT
    c_gd = c_qkv + N_HEADS

    xt = x.reshape(N_TOK, D_MODEL)
    for l in range(DEPTH):
        last = l == DEPTH - 1
        w_in_l = w_in[l]
        cols = [w_in_l[:, i * D_ATT:(i + 1) * D_ATT] for i in range(N_QKV_TILES)]
        qd, kd, vd, qf, kf, vf = cols
        qd, kd = _reorder_head_dims(qd, rope_order), _reorder_head_dims(kd, rope_order)
        w_qkv = jnp.stack([qd, qf, kd, kf, vd, vf]).astype(BF16)
        w_f = jnp.pad(w_in_l[:, c_qkv:c_gd], ((0, 0), (0, LANES - N_HEADS))).astype(BF16)
        b_f = jnp.pad(b_forget[l], (0, LANES - N_HEADS)).reshape(1, LANES)
        w_gd = _column_tiles(w_in_l[:, c_gd:c_gd + D_MODEL], MRG_TJ)
        w_gf = _column_tiles(w_in_l[:, c_gd + D_MODEL:], MRG_TJ)

        xt = _ffn(xt, ffn1_norm[l].reshape(1, D_MODEL), _column_tiles(ffn1_w_gate[l], FFN_TF),
                  _column_tiles(ffn1_w_up[l], FFN_TF), ffn1_w_down[l].astype(BF16), fgain, False)

        qkv_dil, qkv_fox, logf = _qkv_proj(xt, mix_norm[l].reshape(1, D_MODEL), w_qkv, w_f, b_f,
                                           cos_full, sin_signed)
        c_gate = _cumsum(logf.reshape(BATCH, SEQ, LANES))
        y_fox = _fox_attn(qkv_fox, c_gate)
        y_dil = _dil_attn(qkv_dil, dil_band)

        xt = _merge(xt, mix_norm[l].reshape(1, D_MODEL), y_dil, y_fox, w_gd, w_gf,
                    b_gate_dil[l].reshape(1, D_MODEL), b_gate_fox[l].reshape(1, D_MODEL),
                    _column_tiles(w_proj_dil[l], MRG_TJ), _column_tiles(w_proj_fox[l], MRG_TJ),
                    w_out[l].astype(BF16))

        xt = _ffn(xt, ffn2_norm[l].reshape(1, D_MODEL), _column_tiles(ffn2_w_gate[l], FFN_TF),
                  _column_tiles(ffn2_w_up[l], FFN_TF), ffn2_w_down[l].astype(BF16), fgain, last)
    return xt.reshape(BATCH, SEQ, D_MODEL)
```

```python
import functools

import numpy as np
import jax
import jax.numpy as jnp
from jax import lax
from jax.experimental import pallas as pl
from jax.experimental.pallas import tpu as pltpu

D_MODEL = 2048
BATCH = 2
SEQ = 8192
DEPTH = 1
HEAD_DIM = 128
N_HEADS = 8
D_ATT = N_HEADS * HEAD_DIM
DIL_PATTERNS = ((128, 1), (512, 4), (2048, 16))
MAX_WINDOW = 2048
ROPE_THETA = 500000.0
ROPE_DIM = HEAD_DIM // 4
D_FF = 5632
NORM_EPS = 1e-6
N_TOK = BATCH * SEQ
LANES = 128

NEG_BIG = -1e30
LOG2E = 1.4426950408889634
MIB = 1024 * 1024

F32 = jnp.float32
BF16 = jnp.bfloat16


def _rmsnorm(x, gain):
    ms = jnp.mean(x * x, axis=-1, keepdims=True)
    return x * lax.rsqrt(ms + NORM_EPS) * gain


FFN_TM = 1024
FFN_TF = 512
FFN_STEPS = D_FF // FFN_TF


def _ffn_kernel(x_ref, gain_ref, wg_ref, wu_ref, wd_ref, wd_last_ref, fgain_ref, o_ref, h_ref, a_ref,
                *, final_norm):
    j = pl.program_id(1)
    last = FFN_STEPS - 1

    def gate_up():
        h = h_ref[...]
        g = jnp.dot(h, wg_ref[...], preferred_element_type=F32)
        u = jnp.dot(h, wu_ref[...], preferred_element_type=F32)
        a_ref[j % 2] = ((g * jax.nn.sigmoid(g)) * u).astype(BF16)

    def down():
        o_ref[...] += jnp.dot(a_ref[(j + 1) % 2], wd_ref[...], preferred_element_type=F32)

    @pl.when(j == 0)
    def _():
        h_ref[...] = _rmsnorm(x_ref[...], gain_ref[...]).astype(BF16)
        o_ref[...] = jnp.zeros_like(o_ref)
        gate_up()

    @pl.when((j > 0) & (j < last))
    def _():
        down()
        gate_up()

    @pl.when(j == last)
    def _():
        down()
        gate_up()
        o_ref[...] += jnp.dot(a_ref[last % 2], wd_last_ref[...], preferred_element_type=F32)
        y = x_ref[...] + 0.5 * o_ref[...]
        if final_norm:
            y = _rmsnorm(y, fgain_ref[...])
        o_ref[...] = y


def _ffn(x, gain, wg, wu, wd, fgain, final_norm):
    grid = (N_TOK // FFN_TM, FFN_STEPS)
    last = FFN_STEPS - 1
    return pl.pallas_call(
        functools.partial(_ffn_kernel, final_norm=final_norm),
        grid=grid,
        in_specs=[
            pl.BlockSpec((FFN_TM, D_MODEL), lambda i, j: (i, 0)),
            pl.BlockSpec((1, D_MODEL), lambda i, j: (0, 0)),
            pl.BlockSpec((None, D_MODEL, FFN_TF), lambda i, j: (j, 0, 0)),
            pl.BlockSpec((None, D_MODEL, FFN_TF), lambda i, j: (j, 0, 0)),
            pl.BlockSpec((FFN_TF, D_MODEL), lambda i, j: (jnp.maximum(j - 1, 0), 0)),
            pl.BlockSpec((FFN_TF, D_MODEL), lambda i, j: (last, 0), pipeline_mode=pl.Buffered(1)),
            pl.BlockSpec((1, D_MODEL), lambda i, j: (0, 0)),
        ],
        out_specs=pl.BlockSpec((FFN_TM, D_MODEL), lambda i, j: (i, 0)),
        out_shape=jax.ShapeDtypeStruct((N_TOK, D_MODEL), F32),
        scratch_shapes=[pltpu.VMEM((FFN_TM, D_MODEL), BF16),
                        pltpu.VMEM((2, FFN_TM, FFN_TF), BF16)],
        compiler_params=pltpu.CompilerParams(
            dimension_semantics=("parallel", "arbitrary"), vmem_limit_bytes=61 * MIB),
        name="ffn",
    )(x, gain, wg, wu, wd, wd, fgain)


QKV_TM = 1024
N_MIX_QKV = 3
QI, KI, VI = range(N_MIX_QKV)
N_QKV_TILES = 2 * N_MIX_QKV


ROPE_HALF = ROPE_DIM // 2
ROPE_PARTNER = LANES // 2


def _rope(x, cos, sin_signed):
    return x * cos + pltpu.roll(x, ROPE_PARTNER, 1) * sin_signed


def _qkv_kernel(x_ref, gain_ref, w_ref, wf_ref, bf_ref, cos_ref, sin_ref,
                dil_ref, fox_ref, logf_ref, h_ref, r_ref):
    n = pl.program_id(1)
    scale = (HEAD_DIM ** -0.5) * LOG2E
    is_q = n <= 2

    def matmul():
        r_ref[...] = jnp.dot(h_ref[...], w_ref[...], preferred_element_type=F32)

    def heads():
        r = r_ref[...]
        return [r[:, hh * HEAD_DIM:(hh + 1) * HEAD_DIM] for hh in range(N_HEADS)]

    def finish_dil(rotary, scaled):
        s = jnp.where(scaled, scale, 1.0)
        cos = jnp.where(rotary, cos_ref[...], 1.0) * s
        sin = jnp.where(rotary, sin_ref[...], 0.0) * s
        for hh, t in enumerate(heads()):
            dil_ref[hh] = _rope(t, cos, sin)

    def finish_fox(scaled):
        s = jnp.where(scaled, scale, 1.0)
        for hh, t in enumerate(heads()):
            fox_ref[hh] = (t * s).astype(BF16)

    @pl.when(n == 0)
    def _():
        h = _rmsnorm(x_ref[...], gain_ref[...]).astype(BF16)
        h_ref[...] = h
        f = jnp.dot(h, wf_ref[...], preferred_element_type=F32) + bf_ref[...]
        logf_ref[...] = jnp.minimum(f, 0.0) - jnp.log1p(jnp.exp(-jnp.abs(f)))
        matmul()

    @pl.when((n == 1) | (n == 3))
    def _():
        finish_dil(True, is_q)
        matmul()

    @pl.when((n == 2) | (n == 4))
    def _():
        finish_fox(is_q)
        matmul()

    @pl.when(n == N_QKV_TILES - 1)
    def _():
        finish_dil(False, False)
        matmul()
        finish_fox(False)


def _qkv_proj(x, gain, w_qkv, w_f, b_f, cos, sin_signed):
    s_tiles = SEQ // QKV_TM
    grid = (N_TOK // QKV_TM, N_QKV_TILES)

    def head_block(m, which):
        return (which, m // s_tiles, 0, m % s_tiles, 0)

    return pl.pallas_call(
        _qkv_kernel,
        grid=grid,
        in_specs=[
            pl.BlockSpec((QKV_TM, D_MODEL), lambda m, n: (m, 0)),
            pl.BlockSpec((1, D_MODEL), lambda m, n: (0, 0)),
            pl.BlockSpec((None, D_MODEL, D_ATT), lambda m, n: (n, 0, 0)),
            pl.BlockSpec((D_MODEL, LANES), lambda m, n: (0, 0)),
            pl.BlockSpec((1, LANES), lambda m, n: (0, 0)),
            pl.BlockSpec((QKV_TM, LANES), lambda m, n: (m % s_tiles, 0)),
            pl.BlockSpec((QKV_TM, LANES), lambda m, n: (m % s_tiles, 0)),
        ],
        out_specs=[
            pl.BlockSpec((None, None, N_HEADS, QKV_TM, HEAD_DIM),
                         lambda m, n: head_block(m, jnp.maximum(n - 1, 0) // 2)),
            pl.BlockSpec((None, None, N_HEADS, QKV_TM, HEAD_DIM),
                         lambda m, n: head_block(m, jnp.maximum(n - 3, 0))),
            pl.BlockSpec((QKV_TM, LANES), lambda m, n: (m, 0)),
        ],
        out_shape=[
            jax.ShapeDtypeStruct((N_MIX_QKV, BATCH, N_HEADS, SEQ, HEAD_DIM), F32),
            jax.ShapeDtypeStruct((N_MIX_QKV, BATCH, N_HEADS, SEQ, HEAD_DIM), BF16),
            jax.ShapeDtypeStruct((N_TOK, LANES), F32),
        ],
        scratch_shapes=[pltpu.VMEM((QKV_TM, D_MODEL), BF16),
                        pltpu.VMEM((QKV_TM, D_ATT), F32)],
        compiler_params=pltpu.CompilerParams(
            dimension_semantics=("parallel", "arbitrary"), vmem_limit_bytes=56 * MIB),
        name="qkv_proj",
    )(x, gain, w_qkv, w_f, b_f, cos, sin_signed)


CUM_TS = 1024
CUM_CHUNK = 128


def _cumsum_kernel(lf_ref, c_ref, carry_ref):
    @pl.when(pl.program_id(1) == 0)
    def _():
        carry_ref[...] = jnp.zeros_like(carry_ref)

    r = lax.broadcasted_iota(jnp.int32, (CUM_CHUNK, CUM_CHUNK), 0)
    c = lax.broadcasted_iota(jnp.int32, (CUM_CHUNK, CUM_CHUNK), 1)
    tri = (c <= r).astype(F32)
    carry = carry_ref[...]
    for i in range(CUM_TS // CUM_CHUNK):
        rows = slice(i * CUM_CHUNK, (i + 1) * CUM_CHUNK)
        cs = jnp.dot(tri, lf_ref[rows, :], preferred_element_type=F32,
                     precision=lax.Precision.HIGHEST) + carry
        c_ref[rows, :] = cs
        carry = cs[CUM_CHUNK - 1:CUM_CHUNK, :]
    carry_ref[...] = carry


def _cumsum(logf):
    return pl.pallas_call(
        _cumsum_kernel,
        grid=(BATCH, SEQ // CUM_TS),
        in_specs=[pl.BlockSpec((None, CUM_TS, LANES), lambda b, s: (b, s, 0))],
        out_specs=pl.BlockSpec((None, CUM_TS, LANES), lambda b, s: (b, s, 0)),
        out_shape=jax.ShapeDtypeStruct((BATCH, SEQ, LANES), F32),
        scratch_shapes=[pltpu.VMEM((1, LANES), F32)],
        compiler_params=pltpu.CompilerParams(dimension_semantics=("parallel", "arbitrary")),
        name="gate_cumsum",
    )(logf)


FOX_TQ = 2048
FOX_TK = 512
FOX_DIAG = FOX_TQ // FOX_TK
BF16_ROWS = 16
FOX_VT_ROWS = HEAD_DIM + BF16_ROWS
N_SPLIT = 3


def _gate_columns(c_tile, head, key_side):
    lane = lax.broadcasted_iota(jnp.int32, c_tile.shape, 1)
    c = jnp.sum(jnp.where(lane == head, c_tile, 0.0), axis=-1, keepdims=True) * LOG2E
    if key_side:
        c = -c
    base = N_SPLIT if key_side else 0
    cols = jnp.where((lane >= N_SPLIT - base) & (lane < 2 * N_SPLIT - base), 1.0, 0.0)
    rest = c
    for t in range(N_SPLIT):
        term = rest.astype(BF16).astype(F32)
        cols = jnp.where(lane == base + t, term, cols)
        rest = rest - term
    return cols.astype(BF16)


def _fox_kernel(q_ref, k_ref, v_ref, c_ref, o_ref, kx_ref, vt_ref, st_ref, mx_ref, m_ref, acc_ref):
    head = pl.program_id(0) % N_HEADS
    qi = pl.program_id(1)

    @pl.when(qi == 0)
    def _():
        ones_tile = (lax.broadcasted_iota(jnp.int32, (BF16_ROWS, FOX_TK), 0) == 0).astype(BF16)

        def build(i, carry):
            rows = pl.ds(pl.multiple_of(i * FOX_TK, FOX_TK), FOX_TK)
            kx_ref[rows, :] = _gate_columns(c_ref[rows, :], head, True)
            vt_ref[0:HEAD_DIM, rows] = v_ref[rows, :].astype(F32).T.astype(BF16)
            vt_ref[HEAD_DIM:FOX_VT_ROWS, rows] = ones_tile
            return carry

        lax.fori_loop(0, SEQ // FOX_TK, build, 0)

    q_rows = pl.ds(pl.multiple_of(qi * FOX_TQ, FOX_TQ), FOX_TQ)
    q_aug = jnp.concatenate([q_ref[...], _gate_columns(c_ref[q_rows, :], head, False)], axis=-1)

    def key_rows(kb):
        return pl.ds(pl.multiple_of(kb * FOX_TK, FOX_TK), FOX_TK)

    def scores(kb, slot, q_lo=0):
        rows = key_rows(kb)
        k_aug = jnp.concatenate([k_ref[rows, :], kx_ref[rows, :]], axis=-1)
        st = lax.dot_general(k_aug, q_aug[q_lo:, :], (((1,), (1,)), ((), ())),
                             preferred_element_type=F32)
        st_ref[slot, :, q_lo:] = st
        mx_ref[slot, :, q_lo:] = jnp.max(st, axis=0, keepdims=True)

    def accumulate(kb, slot, q_lo=0, key_lo=None):
        st = st_ref[slot, :, q_lo:]
        if key_lo is None:
            st_max = mx_ref[slot, :, q_lo:]
        else:
            key = lax.broadcasted_iota(jnp.int32, st.shape, 0) + key_lo
            qry = lax.broadcasted_iota(jnp.int32, st.shape, 1) + q_lo
            st = jnp.where(key <= qry, st, NEG_BIG)
            st_max = jnp.max(st, axis=0, keepdims=True)
        m = m_ref[:, q_lo:]
        m_new = jnp.maximum(m, st_max)
        p = jnp.exp2(st - m_new).astype(BF16)
        pv = jnp.dot(vt_ref[:, key_rows(kb)], p, preferred_element_type=F32)
        acc_ref[:, q_lo:] = jnp.exp2(m - m_new) * acc_ref[:, q_lo:] + pv
        m_ref[:, q_lo:] = m_new

    m_ref[...] = jnp.full(m_ref.shape, NEG_BIG, F32)
    acc_ref[...] = jnp.zeros_like(acc_ref)
    scores(0, 0)

    def pair(j, carry):
        scores(2 * j + 1, 1)
        accumulate(2 * j, 0)
        scores(2 * j + 2, 0)
        accumulate(2 * j + 1, 1)
        return carry

    first = qi * FOX_DIAG
    lax.fori_loop(0, qi * (FOX_DIAG // 2), pair, 0)
    for d in range(FOX_DIAG):
        if d + 1 < FOX_DIAG:
            scores(first + d + 1, (d + 1) % 2, q_lo=(d + 1) * FOX_TK)
        accumulate(first + d, d % 2, q_lo=d * FOX_TK, key_lo=d * FOX_TK)

    out_t = acc_ref[0:HEAD_DIM, :] / acc_ref[HEAD_DIM:HEAD_DIM + 1, :]
    o_ref[...] = out_t.T.astype(BF16)


def _fox_attn(qkv, c_gate):
    assert FOX_TQ % (2 * FOX_TK) == 0
    q_tiles = SEQ // FOX_TQ
    return pl.pallas_call(
        _fox_kernel,
        grid=(BATCH * N_HEADS, q_tiles),
        in_specs=[
            pl.BlockSpec((None, None, None, FOX_TQ, HEAD_DIM),
                         lambda bh, qi: (QI, bh // N_HEADS, bh % N_HEADS, qi, 0)),
            pl.BlockSpec((None, None, None, SEQ, HEAD_DIM),
                         lambda bh, qi: (KI, bh // N_HEADS, bh % N_HEADS, 0, 0)),
            pl.BlockSpec((None, None, None, SEQ, HEAD_DIM),
                         lambda bh, qi: (VI, bh // N_HEADS, bh % N_HEADS, 0, 0)),
            pl.BlockSpec((None, SEQ, LANES), lambda bh, qi: (bh // N_HEADS, 0, 0)),
        ],
        out_specs=pl.BlockSpec((FOX_TQ, HEAD_DIM),
                               lambda bh, qi: ((bh // N_HEADS) * q_tiles + qi, bh % N_HEADS)),
        out_shape=jax.ShapeDtypeStruct((N_TOK, D_ATT), BF16),
        scratch_shapes=[
            pltpu.VMEM((SEQ, LANES), BF16),
            pltpu.VMEM((FOX_VT_ROWS, SEQ), BF16),
            pltpu.VMEM((2, FOX_TK, FOX_TQ), F32),
            pltpu.VMEM((2, 1, FOX_TQ), F32),
            pltpu.VMEM((1, FOX_TQ), F32),
            pltpu.VMEM((FOX_VT_ROWS, FOX_TQ), F32),
        ],
        compiler_params=pltpu.CompilerParams(
            dimension_semantics=("parallel", "arbitrary"), vmem_limit_bytes=48 * MIB),
        name="fox_attn",
    )(qkv, qkv, qkv, c_gate)


DIL_GROUP = 128
DIL_SUPER = DIL_GROUP * max(d for _, d in DIL_PATTERNS)
N_PAT = len(DIL_PATTERNS)
assert all(w == DIL_GROUP * d and DIL_SUPER % w == 0 for w, d in DIL_PATTERNS)


def _dil_band_bias():
    i = np.arange(DIL_GROUP)[:, None]
    j = np.arange(2 * DIL_GROUP)[None, :]
    return jnp.asarray(np.where((j >= i) & (j <= i + DIL_GROUP), 0.0, NEG_BIG), F32)


def _dil_kernel(q_ref, k_ref, v_ref, band_ref, o_ref, num_ref, den_ref, mx_ref):
    sb = pl.program_id(1)
    band = band_ref[...]
    col = lax.broadcasted_iota(jnp.int32, band.shape, 1)
    band_first = jnp.where((sb == 0) & (col < DIL_GROUP), NEG_BIG, band)
    ones = jnp.ones((2 * DIL_GROUP, HEAD_DIM), BF16)

    def rows(start, d):
        return pl.ds(start, DIL_GROUP, stride=d) if d > 1 else pl.ds(start, DIL_GROUP)

    for g, (_, d) in enumerate(DIL_PATTERNS):
        span = DIL_GROUP * d
        for u in range(DIL_SUPER // span):
            t0 = sb * DIL_SUPER + u * span
            prev0 = jnp.maximum(t0 - span, 0) if u == 0 else t0 - span
            bias = band_first if u == 0 else band
            for c in range(d):
                own = rows(u * span + c, d)
                q = q_ref[own, :].astype(BF16)
                k = jnp.concatenate([k_ref[rows(prev0 + c, d), :], k_ref[rows(t0 + c, d), :]],
                                    axis=0).astype(BF16)
                v = jnp.concatenate([v_ref[rows(prev0 + c, d), :], v_ref[rows(t0 + c, d), :]],
                                    axis=0).astype(BF16)
                s = lax.dot_general(q, k, (((1,), (1,)), ((), ())),
                                    preferred_element_type=F32) + bias
                m = jnp.max(s, axis=-1, keepdims=True)
                p = jnp.exp2(s - m).astype(BF16)
                r = jnp.dot(p, jnp.concatenate([v, ones], axis=-1), preferred_element_type=F32)
                num_ref[g, own, :] = r[:, :HEAD_DIM]
                den_ref[g, own, :] = r[:, HEAD_DIM:]
                mx_ref[g, own, :] = jnp.broadcast_to(m, (DIL_GROUP, HEAD_DIM))

    def combine(i, carry):
        r = pl.ds(pl.multiple_of(i * DIL_GROUP, DIL_GROUP), DIL_GROUP)
        mx = [mx_ref[g, r, :] for g in range(N_PAT)]
        top = functools.reduce(jnp.maximum, mx)
        wgt = [jnp.exp2(t - top) for t in mx]
        num = functools.reduce(jnp.add, [wgt[g] * num_ref[g, r, :] for g in range(N_PAT)])
        den = functools.reduce(jnp.add, [wgt[g] * den_ref[g, r, :] for g in range(N_PAT)])
        o_ref[r, :] = (num / den).astype(BF16)
        return carry

    lax.fori_loop(0, DIL_SUPER // DIL_GROUP, combine, 0)


def _dil_attn(qkv, band):
    steps = SEQ // DIL_SUPER
    per_pattern = pltpu.VMEM((N_PAT, DIL_SUPER, HEAD_DIM), F32)
    return pl.pallas_call(
        _dil_kernel,
        grid=(BATCH * N_HEADS, steps),
        in_specs=[
            pl.BlockSpec((None, None, None, DIL_SUPER, HEAD_DIM),
                         lambda bh, sb: (QI, bh // N_HEADS, bh % N_HEADS, sb, 0)),
            pl.BlockSpec((None, None, None, SEQ, HEAD_DIM),
                         lambda bh, sb: (KI, bh // N_HEADS, bh % N_HEADS, 0, 0)),
            pl.BlockSpec((None, None, None, SEQ, HEAD_DIM),
                         lambda bh, sb: (VI, bh // N_HEADS, bh % N_HEADS, 0, 0)),
            pl.BlockSpec((DIL_GROUP, 2 * DIL_GROUP), lambda bh, sb: (0, 0)),
        ],
        out_specs=pl.BlockSpec((DIL_SUPER, HEAD_DIM),
                               lambda bh, sb: ((bh // N_HEADS) * steps + sb, bh % N_HEADS)),
        out_shape=jax.ShapeDtypeStruct((N_TOK, D_ATT), BF16),
        scratch_shapes=[per_pattern, per_pattern, per_pattern],
        compiler_params=pltpu.CompilerParams(
            dimension_semantics=("parallel", "arbitrary"), vmem_limit_bytes=48 * MIB),
        name="dil_attn",
    )(qkv, qkv, qkv, band)


MRG_TM = 512
MRG_TJ = 512


MRG_STEPS = D_MODEL // MRG_TJ


def _merge_kernel(x_ref, gain_ref, yd_ref, yf_ref, wgd_ref, wgf_ref, bgd_ref, bgf_ref,
                  wpd_ref, wpf_ref, wo_ref, wo_last_ref, o_ref, h_ref, m_ref):
    j = pl.program_id(1)
    last = MRG_STEPS - 1

    def merge_tile():
        h = h_ref[...]
        gd = jnp.dot(h, wgd_ref[...], preferred_element_type=F32) + bgd_ref[...]
        gf = jnp.dot(h, wgf_ref[...], preferred_element_type=F32) + bgf_ref[...]
        pd = jnp.dot(yd_ref[...], wpd_ref[...], preferred_element_type=F32)
        pf = jnp.dot(yf_ref[...], wpf_ref[...], preferred_element_type=F32)
        m_ref[j % 2] = (jax.nn.sigmoid(gd) * pd + jax.nn.sigmoid(gf) * pf).astype(BF16)

    def project():
        o_ref[...] += jnp.dot(m_ref[(j + 1) % 2], wo_ref[...], preferred_element_type=F32)

    @pl.when(j == 0)
    def _():
        h_ref[...] = _rmsnorm(x_ref[...], gain_ref[...]).astype(BF16)
        o_ref[...] = jnp.zeros_like(o_ref)
        merge_tile()

    @pl.when((j > 0) & (j < last))
    def _():
        project()
        merge_tile()

    @pl.when(j == last)
    def _():
        project()
        merge_tile()
        o_ref[...] += jnp.dot(m_ref[last % 2], wo_last_ref[...], preferred_element_type=F32)
        o_ref[...] = x_ref[...] + o_ref[...]


def _merge(x, gain, yd, yf, w_gates, bgd, bgf, wpd, wpf, wo):
    grid = (N_TOK // MRG_TM, MRG_STEPS)
    last = MRG_STEPS - 1
    return pl.pallas_call(
        _merge_kernel,
        grid=grid,
        in_specs=[
            pl.BlockSpec((MRG_TM, D_MODEL), lambda i, j: (i, 0)),
            pl.BlockSpec((1, D_MODEL), lambda i, j: (0, 0)),
            pl.BlockSpec((MRG_TM, D_ATT), lambda i, j: (i, 0)),
            pl.BlockSpec((MRG_TM, D_ATT), lambda i, j: (i, 0)),
            pl.BlockSpec((None, D_MODEL, MRG_TJ), lambda i, j: (j, 0, 0)),
            pl.BlockSpec((None, D_MODEL, MRG_TJ), lambda i, j: (MRG_STEPS + j, 0, 0)),
            pl.BlockSpec((1, MRG_TJ), lambda i, j: (0, j)),
            pl.BlockSpec((1, MRG_TJ), lambda i, j: (0, j)),
            pl.BlockSpec((None, D_ATT, MRG_TJ), lambda i, j: (j, 0, 0)),
            pl.BlockSpec((None, D_ATT, MRG_TJ), lambda i, j: (j, 0, 0)),
            pl.BlockSpec((MRG_TJ, D_MODEL), lambda i, j: (jnp.maximum(j - 1, 0), 0)),
            pl.BlockSpec((MRG_TJ, D_MODEL), lambda i, j: (last, 0), pipeline_mode=pl.Buffered(1)),
        ],
        out_specs=pl.BlockSpec((MRG_TM, D_MODEL), lambda i, j: (i, 0)),
        out_shape=jax.ShapeDtypeStruct((N_TOK, D_MODEL), F32),
        scratch_shapes=[pltpu.VMEM((MRG_TM, D_MODEL), BF16),
                        pltpu.VMEM((2, MRG_TM, MRG_TJ), BF16)],
        compiler_params=pltpu.CompilerParams(
            dimension_semantics=("parallel", "arbitrary"), vmem_limit_bytes=56 * MIB),
        name="merge_proj",
    )(x, gain, yd, yf, w_gates, w_gates, bgd, bgf, wpd, wpf, wo, wo)


def _rope_tables():
    pos = jnp.arange(SEQ, dtype=F32)
    inv_freq = ROPE_THETA ** (-jnp.arange(0, ROPE_DIM, 2, dtype=F32) / ROPE_DIM)
    ang = pos[:, None] * inv_freq[None, :]
    cos, sin = jnp.cos(ang), jnp.sin(ang)
    gap = ROPE_PARTNER - ROPE_HALF
    tail = HEAD_DIM - ROPE_PARTNER - ROPE_HALF
    one, zero = (lambda n: jnp.ones((SEQ, n), F32)), (lambda n: jnp.zeros((SEQ, n), F32))
    cos_full = jnp.concatenate([cos, one(gap), cos, one(tail)], axis=-1)
    sin_signed = jnp.concatenate([-sin, zero(gap), sin, zero(tail)], axis=-1)
    assert cos_full.shape == (SEQ, HEAD_DIM)
    return cos_full, sin_signed


PREP_ROWS = 512
C_QKV = 2 * N_MIX_QKV * D_ATT
C_GATES = C_QKV + N_HEADS
GATE_LANE_OFF = C_GATES % LANES
_PREP_PARAMS = pltpu.CompilerParams(dimension_semantics=("parallel",), vmem_limit_bytes=40 * MIB)


def _cast_kernel(w_ref, o_ref):
    o_ref[...] = w_ref[...].astype(BF16)


def _cast_rows(w):
    rows, n_cols = w.shape
    return pl.pallas_call(
        _cast_kernel,
        grid=(rows // PREP_ROWS,),
        in_specs=[pl.BlockSpec((PREP_ROWS, n_cols), lambda i: (i, 0))],
        out_specs=pl.BlockSpec((PREP_ROWS, n_cols), lambda i: (i, 0)),
        out_shape=jax.ShapeDtypeStruct(w.shape, BF16),
        compiler_params=_PREP_PARAMS,
        name="cast_rows",
    )(w)


def _cast_column_tiles(w, tile):
    rows, n_cols = w.shape
    return pl.pallas_call(
        _cast_kernel,
        grid=(n_cols // tile,),
        in_specs=[pl.BlockSpec((rows, tile), lambda j: (0, j))],
        out_specs=pl.BlockSpec((None, rows, tile), lambda j: (j, 0, 0)),
        out_shape=jax.ShapeDtypeStruct((n_cols // tile, rows, tile), BF16),
        compiler_params=_PREP_PARAMS,
        name="cast_column_tiles",
    )(w)


def _qkv_weight_kernel(w_ref, o_ref):
    j = pl.program_id(0)
    rotary = (j % 2 == 0) & (j < 2 * (N_MIX_QKV - 1))

    @pl.when(rotary)
    def _():
        lane = lax.broadcasted_iota(jnp.int32, (D_MODEL, HEAD_DIM), 1)
        up = (lane >= ROPE_PARTNER) & (lane < ROPE_PARTNER + ROPE_HALF)
        down = (lane >= ROPE_HALF) & (lane < ROPE_DIM)
        shift = ROPE_PARTNER - ROPE_HALF
        for hh in range(N_HEADS):
            cols = slice(hh * HEAD_DIM, (hh + 1) * HEAD_DIM)
            t = w_ref[:, cols]
            t = jnp.where(up, pltpu.roll(t, shift, 1),
                          jnp.where(down, pltpu.roll(t, LANES - shift, 1), t))
            o_ref[:, cols] = t.astype(BF16)

    @pl.when(jnp.logical_not(rotary))
    def _():
        o_ref[...] = w_ref[...].astype(BF16)


def _qkv_weights(w_in_l):
    return pl.pallas_call(
        _qkv_weight_kernel,
        grid=(N_QKV_TILES,),
        in_specs=[pl.BlockSpec((D_MODEL, D_ATT), lambda j: (0, (j % 2) * N_MIX_QKV + j // 2))],
        out_specs=pl.BlockSpec((None, D_MODEL, D_ATT), lambda j: (j, 0, 0)),
        out_shape=jax.ShapeDtypeStruct((N_QKV_TILES, D_MODEL, D_ATT), BF16),
        compiler_params=_PREP_PARAMS,
        name="qkv_weights",
    )(w_in_l)


def _gate_weight_kernel(a_ref, b_ref, f_ref, o_ref, wf_ref):
    lane = lax.broadcasted_iota(jnp.int32, (D_MODEL, LANES), 1)
    n_chunks = MRG_TJ // LANES
    chunks = [a_ref[:, k * LANES:(k + 1) * LANES] for k in range(n_chunks)] + [b_ref[...]]
    rolled = [pltpu.roll(c, LANES - GATE_LANE_OFF, 1) for c in chunks]
    for k in range(n_chunks):
        o_ref[:, k * LANES:(k + 1) * LANES] = jnp.where(
            lane < LANES - GATE_LANE_OFF, rolled[k], rolled[k + 1]).astype(BF16)

    @pl.when(pl.program_id(0) == 0)
    def _():
        wf_ref[...] = jnp.where(lane < N_HEADS, f_ref[...], 0.0).astype(BF16)


def _gate_weights(w_in_l):
    assert (C_GATES - GATE_LANE_OFF) % MRG_TJ == 0 and C_QKV % LANES == 0
    first_tile = (C_GATES - GATE_LANE_OFF) // MRG_TJ
    first_group = (C_GATES - GATE_LANE_OFF) // LANES
    per_tile = MRG_TJ // LANES
    n_tiles = 2 * D_MODEL // MRG_TJ
    return pl.pallas_call(
        _gate_weight_kernel,
        grid=(n_tiles,),
        in_specs=[
            pl.BlockSpec((D_MODEL, MRG_TJ), lambda t: (0, first_tile + t)),
            pl.BlockSpec((D_MODEL, LANES), lambda t: (0, first_group + per_tile * (t + 1))),
            pl.BlockSpec((D_MODEL, LANES), lambda t: (0, C_QKV // LANES)),
        ],
        out_specs=[
            pl.BlockSpec((None, D_MODEL, MRG_TJ), lambda t: (t, 0, 0)),
            pl.BlockSpec((D_MODEL, LANES), lambda t: (0, 0)),
        ],
        out_shape=[
            jax.ShapeDtypeStruct((n_tiles, D_MODEL, MRG_TJ), BF16),
            jax.ShapeDtypeStruct((D_MODEL, LANES), BF16),
        ],
        compiler_params=pltpu.CompilerParams(dimension_semantics=("arbitrary",),
                                             vmem_limit_bytes=40 * MIB),
        name="gate_weights",
    )(w_in_l, w_in_l, w_in_l)


def kernel(x, ffn1_norm, ffn1_w_gate, ffn1_w_up, ffn1_w_down, mix_norm, w_in, b_forget, b_gate_dil, b_gate_fox, w_proj_dil, w_proj_fox, w_out, ffn2_norm, ffn2_w_gate, ffn2_w_up, ffn2_w_down, final_norm):
    assert x.shape == (BATCH, SEQ, D_MODEL) and x.dtype == F32
    assert w_in.shape == (DEPTH, D_MODEL, C_GATES + 2 * D_MODEL)
    cos_full, sin_signed = _rope_tables()
    dil_band = _dil_band_bias()
    fgain = final_norm.reshape(1, D_MODEL)

    xt = x.reshape(N_TOK, D_MODEL)
    for l in range(DEPTH):
        last = l == DEPTH - 1
        w_qkv = _qkv_weights(w_in[l])
        w_gates, w_f = _gate_weights(w_in[l])
        b_f = jnp.pad(b_forget[l], (0, LANES - N_HEADS)).reshape(1, LANES)

        xt = _ffn(xt, ffn1_norm[l].reshape(1, D_MODEL), _cast_column_tiles(ffn1_w_gate[l], FFN_TF),
                  _cast_column_tiles(ffn1_w_up[l], FFN_TF), _cast_rows(ffn1_w_down[l]), fgain, False)

        qkv_dil, qkv_fox, logf = _qkv_proj(xt, mix_norm[l].reshape(1, D_MODEL), w_qkv, w_f, b_f,
                                           cos_full, sin_signed)
        c_gate = _cumsum(logf.reshape(BATCH, SEQ, LANES))
        y_fox = _fox_attn(qkv_fox, c_gate)
        y_dil = _dil_attn(qkv_dil, dil_band)

        xt = _merge(xt, mix_norm[l].reshape(1, D_MODEL), y_dil, y_fox, w_gates,
                    b_gate_dil[l].reshape(1, D_MODEL), b_gate_fox[l].reshape(1, D_MODEL),
                    _cast_column_tiles(w_proj_dil[l], MRG_TJ),
                    _cast_column_tiles(w_proj_fox[l], MRG_TJ), _cast_rows(w_out[l]))

        xt = _ffn(xt, ffn2_norm[l].reshape(1, D_MODEL), _cast_column_tiles(ffn2_w_gate[l], FFN_TF),
                  _cast_column_tiles(ffn2_w_up[l], FFN_TF), _cast_rows(ffn2_w_down[l]), fgain, last)
    return xt.reshape(BATCH, SEQ, D_MODEL)
```

```python
import functools

import numpy as np
import jax
import jax.numpy as jnp
from jax import lax
from jax.experimental import pallas as pl
from jax.experimental.pallas import tpu as pltpu

D_MODEL = 2048
BATCH = 2
SEQ = 8192
DEPTH = 1
HEAD_DIM = 128
N_HEADS = 8
D_ATT = N_HEADS * HEAD_DIM
DIL_PATTERNS = ((128, 1), (512, 4), (2048, 16))
MAX_WINDOW = 2048
ROPE_THETA = 500000.0
ROPE_DIM = HEAD_DIM // 4
D_FF = 5632
NORM_EPS = 1e-6
N_TOK = BATCH * SEQ
LANES = 128

NEG_BIG = -1e30
LOG2E = 1.4426950408889634
MIB = 1024 * 1024

F32 = jnp.float32
BF16 = jnp.bfloat16


def _rmsnorm(x, gain):
    ms = jnp.mean(x * x, axis=-1, keepdims=True)
    return x * lax.rsqrt(ms + NORM_EPS) * gain


FFN_TM = 1024
FFN_TF = 512
FFN_STEPS = D_FF // FFN_TF


def _ffn_kernel(x_ref, gain_ref, wg_ref, wu_ref, wd_ref, wd_last_ref, fgain_ref, o_ref, h_ref, a_ref,
                *, final_norm):
    j = pl.program_id(1)
    last = FFN_STEPS - 1

    def gate_up():
        h = h_ref[...]
        g = jnp.dot(h, wg_ref[...], preferred_element_type=F32)
        u = jnp.dot(h, wu_ref[...], preferred_element_type=F32)
        a_ref[j % 2] = ((g * jax.nn.sigmoid(g)) * u).astype(BF16)

    def down():
        o_ref[...] += jnp.dot(a_ref[(j + 1) % 2], wd_ref[...], preferred_element_type=F32)

    @pl.when(j == 0)
    def _():
        h_ref[...] = _rmsnorm(x_ref[...], gain_ref[...]).astype(BF16)
        o_ref[...] = jnp.zeros_like(o_ref)
        gate_up()

    @pl.when((j > 0) & (j < last))
    def _():
        down()
        gate_up()

    @pl.when(j == last)
    def _():
        down()
        gate_up()
        o_ref[...] += jnp.dot(a_ref[last % 2], wd_last_ref[...], preferred_element_type=F32)
        y = x_ref[...] + 0.5 * o_ref[...]
        if final_norm:
            y = _rmsnorm(y, fgain_ref[...])
        o_ref[...] = y


def _ffn(x, gain, wg, wu, wd, fgain, final_norm):
    grid = (N_TOK // FFN_TM, FFN_STEPS)
    last = FFN_STEPS - 1
    return pl.pallas_call(
        functools.partial(_ffn_kernel, final_norm=final_norm),
        grid=grid,
        in_specs=[
            pl.BlockSpec((FFN_TM, D_MODEL), lambda i, j: (i, 0)),
            pl.BlockSpec((1, D_MODEL), lambda i, j: (0, 0)),
            pl.BlockSpec((None, D_MODEL, FFN_TF), lambda i, j: (j, 0, 0)),
            pl.BlockSpec((None, D_MODEL, FFN_TF), lambda i, j: (j, 0, 0)),
            pl.BlockSpec((FFN_TF, D_MODEL), lambda i, j: (jnp.maximum(j - 1, 0), 0)),
            pl.BlockSpec((FFN_TF, D_MODEL), lambda i, j: (last, 0), pipeline_mode=pl.Buffered(1)),
            pl.BlockSpec((1, D_MODEL), lambda i, j: (0, 0)),
        ],
        out_specs=pl.BlockSpec((FFN_TM, D_MODEL), lambda i, j: (i, 0)),
        out_shape=jax.ShapeDtypeStruct((N_TOK, D_MODEL), F32),
        scratch_shapes=[pltpu.VMEM((FFN_TM, D_MODEL), BF16),
                        pltpu.VMEM((2, FFN_TM, FFN_TF), BF16)],
        compiler_params=pltpu.CompilerParams(
            dimension_semantics=("parallel", "arbitrary"), vmem_limit_bytes=61 * MIB),
        name="ffn",
    )(x, gain, wg, wu, wd, wd, fgain)


QKV_TM = 1024
N_MIX_QKV = 3
QI, KI, VI = range(N_MIX_QKV)
N_QKV_TILES = 2 * N_MIX_QKV


ROPE_HALF = ROPE_DIM // 2
ROPE_PARTNER = LANES // 2


def _rope(x, cos, sin_signed):
    return x * cos + pltpu.roll(x, ROPE_PARTNER, 1) * sin_signed


NT_DIMS = (((1,), (1,)), ((), ()))


def _qkv_kernel(x_ref, gain_ref, w_ref, wf_ref, bf_ref, cos_ref, sin_ref,
                dil_ref, fox_ref, vt_ref, logf_ref, h_ref, r_ref):
    n = pl.program_id(1)
    scale = (HEAD_DIM ** -0.5) * LOG2E
    is_q = n <= 2

    def matmul():
        r_ref[...] = lax.dot_general(h_ref[...], w_ref[...], NT_DIMS, preferred_element_type=F32)

    def heads():
        r = r_ref[...]
        return [r[:, hh * HEAD_DIM:(hh + 1) * HEAD_DIM] for hh in range(N_HEADS)]

    def finish_dil(rotary, scaled):
        s = jnp.where(scaled, scale, 1.0)
        cos = jnp.where(rotary, cos_ref[...], 1.0) * s
        sin = jnp.where(rotary, sin_ref[...], 0.0) * s
        for hh, t in enumerate(heads()):
            dil_ref[hh] = _rope(t, cos, sin)

    def finish_fox(scaled):
        s = jnp.where(scaled, scale, 1.0)
        for hh, t in enumerate(heads()):
            fox_ref[hh] = (t * s).astype(BF16)

    @pl.when(n == 0)
    def _():
        h = _rmsnorm(x_ref[...], gain_ref[...]).astype(BF16)
        h_ref[...] = h
        f = lax.dot_general(h, wf_ref[...], NT_DIMS, preferred_element_type=F32) + bf_ref[...]
        logf_ref[...] = jnp.minimum(f, 0.0) - jnp.log1p(jnp.exp(-jnp.abs(f)))
        matmul()

    @pl.when((n == 1) | (n == 3))
    def _():
        finish_dil(True, is_q)
        matmul()

    @pl.when((n == 2) | (n == 4))
    def _():
        finish_fox(is_q)
        matmul()

    @pl.when(n == N_QKV_TILES - 1)
    def _():
        finish_dil(False, False)
        matmul()
        for hh, t in enumerate(heads()):
            vt_ref[hh] = t.T.astype(BF16)


def _qkv_proj(x, gain, w_qkv, w_f, b_f, cos, sin_signed):
    s_tiles = SEQ // QKV_TM
    grid = (N_TOK // QKV_TM, N_QKV_TILES)

    def head_block(m, which):
        return (which, m // s_tiles, 0, m % s_tiles, 0)

    return pl.pallas_call(
        _qkv_kernel,
        grid=grid,
        in_specs=[
            pl.BlockSpec((QKV_TM, D_MODEL), lambda m, n: (m, 0)),
            pl.BlockSpec((1, D_MODEL), lambda m, n: (0, 0)),
            pl.BlockSpec((None, D_ATT, D_MODEL), lambda m, n: (n, 0, 0)),
            pl.BlockSpec((LANES, D_MODEL), lambda m, n: (0, 0)),
            pl.BlockSpec((1, LANES), lambda m, n: (0, 0)),
            pl.BlockSpec((QKV_TM, LANES), lambda m, n: (m % s_tiles, 0)),
            pl.BlockSpec((QKV_TM, LANES), lambda m, n: (m % s_tiles, 0)),
        ],
        out_specs=[
            pl.BlockSpec((None, None, N_HEADS, QKV_TM, HEAD_DIM),
                         lambda m, n: head_block(m, jnp.maximum(n - 1, 0) // 2)),
            pl.BlockSpec((None, None, N_HEADS, QKV_TM, HEAD_DIM),
                         lambda m, n: head_block(m, jnp.clip(n - 3, 0, 1))),
            pl.BlockSpec((None, N_HEADS, HEAD_DIM, QKV_TM),
                         lambda m, n: (m // s_tiles, 0, 0, m % s_tiles)),
            pl.BlockSpec((QKV_TM, LANES), lambda m, n: (m, 0)),
        ],
        out_shape=[
            jax.ShapeDtypeStruct((N_MIX_QKV, BATCH, N_HEADS, SEQ, HEAD_DIM), F32),
            jax.ShapeDtypeStruct((2, BATCH, N_HEADS, SEQ, HEAD_DIM), BF16),
            jax.ShapeDtypeStruct((BATCH, N_HEADS, HEAD_DIM, SEQ), BF16),
            jax.ShapeDtypeStruct((N_TOK, LANES), F32),
        ],
        scratch_shapes=[pltpu.VMEM((QKV_TM, D_MODEL), BF16),
                        pltpu.VMEM((QKV_TM, D_ATT), F32)],
        compiler_params=pltpu.CompilerParams(
            dimension_semantics=("parallel", "arbitrary"), vmem_limit_bytes=61 * MIB),
        name="qkv_proj",
    )(x, gain, w_qkv, w_f, b_f, cos, sin_signed)


CUM_TS = 1024
CUM_CHUNK = 128


def _cumsum_kernel(lf_ref, c_ref, carry_ref):
    @pl.when(pl.program_id(1) == 0)
    def _():
        carry_ref[...] = jnp.zeros_like(carry_ref)

    r = lax.broadcasted_iota(jnp.int32, (CUM_CHUNK, CUM_CHUNK), 0)
    c = lax.broadcasted_iota(jnp.int32, (CUM_CHUNK, CUM_CHUNK), 1)
    tri = (c <= r).astype(F32)
    carry = carry_ref[...]
    for i in range(CUM_TS // CUM_CHUNK):
        rows = slice(i * CUM_CHUNK, (i + 1) * CUM_CHUNK)
        cs = jnp.dot(tri, lf_ref[rows, :], preferred_element_type=F32,
                     precision=lax.Precision.HIGHEST) + carry
        c_ref[rows, :] = cs
        carry = cs[CUM_CHUNK - 1:CUM_CHUNK, :]
    carry_ref[...] = carry


def _cumsum(logf):
    return pl.pallas_call(
        _cumsum_kernel,
        grid=(BATCH, SEQ // CUM_TS),
        in_specs=[pl.BlockSpec((None, CUM_TS, LANES), lambda b, s: (b, s, 0))],
        out_specs=pl.BlockSpec((None, CUM_TS, LANES), lambda b, s: (b, s, 0)),
        out_shape=jax.ShapeDtypeStruct((BATCH, SEQ, LANES), F32),
        scratch_shapes=[pltpu.VMEM((1, LANES), F32)],
        compiler_params=pltpu.CompilerParams(dimension_semantics=("parallel", "arbitrary")),
        name="gate_cumsum",
    )(logf)


FOX_TQ = 2048
FOX_TK = 512
FOX_DIAG = FOX_TQ // FOX_TK
BF16_ROWS = 16
FOX_VT_ROWS = HEAD_DIM + BF16_ROWS
N_SPLIT = 3


def _gate_columns(c_tile, head, key_side):
    lane = lax.broadcasted_iota(jnp.int32, c_tile.shape, 1)
    c = jnp.sum(jnp.where(lane == head, c_tile, 0.0), axis=-1, keepdims=True) * LOG2E
    if key_side:
        c = -c
    base = N_SPLIT if key_side else 0
    cols = jnp.where((lane >= N_SPLIT - base) & (lane < 2 * N_SPLIT - base), 1.0, 0.0)
    rest = c
    for t in range(N_SPLIT):
        term = rest.astype(BF16).astype(F32)
        cols = jnp.where(lane == base + t, term, cols)
        rest = rest - term
    return cols.astype(BF16)


def _fox_kernel(q_ref, k_ref, vt_ref, c_ref, o_ref, kx_ref, st_ref, mx_ref, m_ref, acc_ref):
    head = pl.program_id(0) % N_HEADS
    qi = pl.program_id(1)
    ones_tile = (lax.broadcasted_iota(jnp.int32, (BF16_ROWS, FOX_TK), 0) == 0).astype(BF16)

    @pl.when(qi == 0)
    def _():
        def build(i, carry):
            rows = pl.ds(pl.multiple_of(i * FOX_TK, FOX_TK), FOX_TK)
            kx_ref[rows, :] = _gate_columns(c_ref[rows, :], head, True)
            return carry

        lax.fori_loop(0, SEQ // FOX_TK, build, 0)

    q_rows = pl.ds(pl.multiple_of(qi * FOX_TQ, FOX_TQ), FOX_TQ)
    q_aug = jnp.concatenate([q_ref[...], _gate_columns(c_ref[q_rows, :], head, False)], axis=-1)

    def key_rows(kb):
        return pl.ds(pl.multiple_of(kb * FOX_TK, FOX_TK), FOX_TK)

    def scores(kb, slot, q_lo=0):
        rows = key_rows(kb)
        k_aug = jnp.concatenate([k_ref[rows, :], kx_ref[rows, :]], axis=-1)
        st = lax.dot_general(k_aug, q_aug[q_lo:, :], (((1,), (1,)), ((), ())),
                             preferred_element_type=F32)
        st_ref[slot, :, q_lo:] = st
        mx_ref[slot, :, q_lo:] = jnp.max(st, axis=0, keepdims=True)

    def accumulate(kb, slot, q_lo=0, key_lo=None):
        st = st_ref[slot, :, q_lo:]
        if key_lo is None:
            st_max = mx_ref[slot, :, q_lo:]
        else:
            key = lax.broadcasted_iota(jnp.int32, st.shape, 0) + key_lo
            qry = lax.broadcasted_iota(jnp.int32, st.shape, 1) + q_lo
            st = jnp.where(key <= qry, st, NEG_BIG)
            st_max = jnp.max(st, axis=0, keepdims=True)
        m = m_ref[:, q_lo:]
        m_new = jnp.maximum(m, st_max)
        p = jnp.exp2(st - m_new).astype(BF16)
        vt = jnp.concatenate([vt_ref[:, key_rows(kb)], ones_tile], axis=0)
        pv = jnp.dot(vt, p, preferred_element_type=F32)
        acc_ref[:, q_lo:] = jnp.exp2(m - m_new) * acc_ref[:, q_lo:] + pv
        m_ref[:, q_lo:] = m_new

    m_ref[...] = jnp.full(m_ref.shape, NEG_BIG, F32)
    acc_ref[...] = jnp.zeros_like(acc_ref)
    scores(0, 0)

    def pair(j, carry):
        scores(2 * j + 1, 1)
        accumulate(2 * j, 0)
        scores(2 * j + 2, 0)
        accumulate(2 * j + 1, 1)
        return carry

    first = qi * FOX_DIAG
    lax.fori_loop(0, qi * (FOX_DIAG // 2), pair, 0)
    for d in range(FOX_DIAG):
        if d + 1 < FOX_DIAG:
            scores(first + d + 1, (d + 1) % 2, q_lo=(d + 1) * FOX_TK)
        accumulate(first + d, d % 2, q_lo=d * FOX_TK, key_lo=d * FOX_TK)

    out_t = acc_ref[0:HEAD_DIM, :] / acc_ref[HEAD_DIM:HEAD_DIM + 1, :]
    o_ref[...] = out_t.T.astype(BF16)


def _fox_attn(qk, vt, c_gate):
    assert FOX_TQ % (2 * FOX_TK) == 0
    q_tiles = SEQ // FOX_TQ
    return pl.pallas_call(
        _fox_kernel,
        grid=(BATCH * N_HEADS, q_tiles),
        in_specs=[
            pl.BlockSpec((None, None, None, FOX_TQ, HEAD_DIM),
                         lambda bh, qi: (QI, bh // N_HEADS, bh % N_HEADS, qi, 0)),
            pl.BlockSpec((None, None, None, SEQ, HEAD_DIM),
                         lambda bh, qi: (KI, bh // N_HEADS, bh % N_HEADS, 0, 0)),
            pl.BlockSpec((None, None, HEAD_DIM, SEQ),
                         lambda bh, qi: (bh // N_HEADS, bh % N_HEADS, 0, 0)),
            pl.BlockSpec((None, SEQ, LANES), lambda bh, qi: (bh // N_HEADS, 0, 0)),
        ],
        out_specs=pl.BlockSpec((FOX_TQ, HEAD_DIM),
                               lambda bh, qi: ((bh // N_HEADS) * q_tiles + qi, bh % N_HEADS)),
        out_shape=jax.ShapeDtypeStruct((N_TOK, D_ATT), BF16),
        scratch_shapes=[
            pltpu.VMEM((SEQ, LANES), BF16),
            pltpu.VMEM((2, FOX_TK, FOX_TQ), F32),
            pltpu.VMEM((2, 1, FOX_TQ), F32),
            pltpu.VMEM((1, FOX_TQ), F32),
            pltpu.VMEM((FOX_VT_ROWS, FOX_TQ), F32),
        ],
        compiler_params=pltpu.CompilerParams(
            dimension_semantics=("parallel", "arbitrary"), vmem_limit_bytes=48 * MIB),
        name="fox_attn",
    )(qk, qk, vt, c_gate)


DIL_GROUP = 128
DIL_SUPER = DIL_GROUP * max(d for _, d in DIL_PATTERNS)
N_PAT = len(DIL_PATTERNS)
assert all(w == DIL_GROUP * d and DIL_SUPER % w == 0 for w, d in DIL_PATTERNS)


def _dil_band_bias():
    i = np.arange(DIL_GROUP)[:, None]
    j = np.arange(2 * DIL_GROUP)[None, :]
    return jnp.asarray(np.where((j >= i) & (j <= i + DIL_GROUP), 0.0, NEG_BIG), F32)


def _dil_kernel(q_ref, k_ref, v_ref, band_ref, o_ref, num_ref, den_ref, mx_ref):
    sb = pl.program_id(1)
    band = band_ref[...]
    col = lax.broadcasted_iota(jnp.int32, band.shape, 1)
    band_first = jnp.where((sb == 0) & (col < DIL_GROUP), NEG_BIG, band)
    ones = jnp.ones((2 * DIL_GROUP, HEAD_DIM), BF16)

    def rows(start, d):
        return pl.ds(start, DIL_GROUP, stride=d) if d > 1 else pl.ds(start, DIL_GROUP)

    for g, (_, d) in enumerate(DIL_PATTERNS):
        span = DIL_GROUP * d
        for u in range(DIL_SUPER // span):
            t0 = sb * DIL_SUPER + u * span
            prev0 = jnp.maximum(t0 - span, 0) if u == 0 else t0 - span
            bias = band_first if u == 0 else band
            for c in range(d):
                own = rows(u * span + c, d)
                q = q_ref[own, :].astype(BF16)
                k = jnp.concatenate([k_ref[rows(prev0 + c, d), :], k_ref[rows(t0 + c, d), :]],
                                    axis=0).astype(BF16)
                v = jnp.concatenate([v_ref[rows(prev0 + c, d), :], v_ref[rows(t0 + c, d), :]],
                                    axis=0).astype(BF16)
                s = lax.dot_general(q, k, (((1,), (1,)), ((), ())),
                                    preferred_element_type=F32) + bias
                m = jnp.max(s, axis=-1, keepdims=True)
                p = jnp.exp2(s - m).astype(BF16)
                r = jnp.dot(p, jnp.concatenate([v, ones], axis=-1), preferred_element_type=F32)
                num_ref[g, own, :] = r[:, :HEAD_DIM]
                den_ref[g, own, :] = r[:, HEAD_DIM:]
                mx_ref[g, own, :] = jnp.broadcast_to(m, (DIL_GROUP, HEAD_DIM))

    def combine(i, carry):
        r = pl.ds(pl.multiple_of(i * DIL_GROUP, DIL_GROUP), DIL_GROUP)
        mx = [mx_ref[g, r, :] for g in range(N_PAT)]
        top = functools.reduce(jnp.maximum, mx)
        wgt = [jnp.exp2(t - top) for t in mx]
        num = functools.reduce(jnp.add, [wgt[g] * num_ref[g, r, :] for g in range(N_PAT)])
        den = functools.reduce(jnp.add, [wgt[g] * den_ref[g, r, :] for g in range(N_PAT)])
        o_ref[r, :] = (num / den).astype(BF16)
        return carry

    lax.fori_loop(0, DIL_SUPER // DIL_GROUP, combine, 0)


def _dil_attn(qkv, band):
    steps = SEQ // DIL_SUPER
    per_pattern = pltpu.VMEM((N_PAT, DIL_SUPER, HEAD_DIM), F32)
    return pl.pallas_call(
        _dil_kernel,
        grid=(BATCH * N_HEADS, steps),
        in_specs=[
            pl.BlockSpec((None, None, None, DIL_SUPER, HEAD_DIM),
                         lambda bh, sb: (QI, bh // N_HEADS, bh % N_HEADS, sb, 0)),
            pl.BlockSpec((None, None, None, SEQ, HEAD_DIM),
                         lambda bh, sb: (KI, bh // N_HEADS, bh % N_HEADS, 0, 0)),
            pl.BlockSpec((None, None, None, SEQ, HEAD_DIM),
                         lambda bh, sb: (VI, bh // N_HEADS, bh % N_HEADS, 0, 0)),
            pl.BlockSpec((DIL_GROUP, 2 * DIL_GROUP), lambda bh, sb: (0, 0)),
        ],
        out_specs=pl.BlockSpec((DIL_SUPER, HEAD_DIM),
                               lambda bh, sb: ((bh // N_HEADS) * steps + sb, bh % N_HEADS)),
        out_shape=jax.ShapeDtypeStruct((N_TOK, D_ATT), BF16),
        scratch_shapes=[per_pattern, per_pattern, per_pattern],
        compiler_params=pltpu.CompilerParams(
            dimension_semantics=("parallel", "arbitrary"), vmem_limit_bytes=48 * MIB),
        name="dil_attn",
    )(qkv, qkv, qkv, band)


MRG_TM = 512
MRG_TJ = 512


MRG_STEPS = D_MODEL // MRG_TJ


def _merge_kernel(x_ref, gain_ref, yd_ref, yf_ref, wgd_ref, wgf_ref, bgd_ref, bgf_ref,
                  wpd_ref, wpf_ref, wo_ref, wo_last_ref, o_ref, h_ref, m_ref):
    j = pl.program_id(1)
    last = MRG_STEPS - 1

    def merge_tile():
        h = h_ref[...]
        gd = lax.dot_general(h, wgd_ref[...], NT_DIMS, preferred_element_type=F32) + bgd_ref[...]
        gf = lax.dot_general(h, wgf_ref[...], NT_DIMS, preferred_element_type=F32) + bgf_ref[...]
        pd = jnp.dot(yd_ref[...], wpd_ref[...], preferred_element_type=F32)
        pf = jnp.dot(yf_ref[...], wpf_ref[...], preferred_element_type=F32)
        m_ref[j % 2] = (jax.nn.sigmoid(gd) * pd + jax.nn.sigmoid(gf) * pf).astype(BF16)

    def project():
        o_ref[...] += jnp.dot(m_ref[(j + 1) % 2], wo_ref[...], preferred_element_type=F32)

    @pl.when(j == 0)
    def _():
        h_ref[...] = _rmsnorm(x_ref[...], gain_ref[...]).astype(BF16)
        o_ref[...] = jnp.zeros_like(o_ref)
        merge_tile()

    @pl.when((j > 0) & (j < last))
    def _():
        project()
        merge_tile()

    @pl.when(j == last)
    def _():
        project()
        merge_tile()
        o_ref[...] += jnp.dot(m_ref[last % 2], wo_last_ref[...], preferred_element_type=F32)
        o_ref[...] = x_ref[...] + o_ref[...]


def _merge(x, gain, yd, yf, w_gates, bgd, bgf, wpd, wpf, wo):
    grid = (N_TOK // MRG_TM, MRG_STEPS)
    last = MRG_STEPS - 1
    return pl.pallas_call(
        _merge_kernel,
        grid=grid,
        in_specs=[
            pl.BlockSpec((MRG_TM, D_MODEL), lambda i, j: (i, 0)),
            pl.BlockSpec((1, D_MODEL), lambda i, j: (0, 0)),
            pl.BlockSpec((MRG_TM, D_ATT), lambda i, j: (i, 0)),
            pl.BlockSpec((MRG_TM, D_ATT), lambda i, j: (i, 0)),
            pl.BlockSpec((None, MRG_TJ, D_MODEL), lambda i, j: (j, 0, 0)),
            pl.BlockSpec((None, MRG_TJ, D_MODEL), lambda i, j: (MRG_STEPS + j, 0, 0)),
            pl.BlockSpec((1, MRG_TJ), lambda i, j: (0, j)),
            pl.BlockSpec((1, MRG_TJ), lambda i, j: (0, j)),
            pl.BlockSpec((None, D_ATT, MRG_TJ), lambda i, j: (j, 0, 0)),
            pl.BlockSpec((None, D_ATT, MRG_TJ), lambda i, j: (j, 0, 0)),
            pl.BlockSpec((MRG_TJ, D_MODEL), lambda i, j: (jnp.maximum(j - 1, 0), 0)),
            pl.BlockSpec((MRG_TJ, D_MODEL), lambda i, j: (last, 0), pipeline_mode=pl.Buffered(1)),
        ],
        out_specs=pl.BlockSpec((MRG_TM, D_MODEL), lambda i, j: (i, 0)),
        out_shape=jax.ShapeDtypeStruct((N_TOK, D_MODEL), F32),
        scratch_shapes=[pltpu.VMEM((MRG_TM, D_MODEL), BF16),
                        pltpu.VMEM((2, MRG_TM, MRG_TJ), BF16)],
        compiler_params=pltpu.CompilerParams(
            dimension_semantics=("parallel", "arbitrary"), vmem_limit_bytes=56 * MIB),
        name="merge_proj",
    )(x, gain, yd, yf, w_gates, w_gates, bgd, bgf, wpd, wpf, wo, wo)


def _rope_tables():
    pos = np.arange(SEQ, dtype=np.float64)
    inv_freq = ROPE_THETA ** (-np.arange(0, ROPE_DIM, 2, dtype=np.float64) / ROPE_DIM)
    ang = pos[:, None] * inv_freq[None, :]
    cos, sin = np.cos(ang), np.sin(ang)
    gap = ROPE_PARTNER - ROPE_HALF
    tail = HEAD_DIM - ROPE_PARTNER - ROPE_HALF
    one, zero = (lambda n: np.ones((SEQ, n))), (lambda n: np.zeros((SEQ, n)))
    cos_full = np.concatenate([cos, one(gap), cos, one(tail)], axis=-1)
    sin_signed = np.concatenate([-sin, zero(gap), sin, zero(tail)], axis=-1)
    assert cos_full.shape == (SEQ, HEAD_DIM)
    return jnp.asarray(cos_full, F32), jnp.asarray(sin_signed, F32)


PREP_ROWS = 512
C_QKV = 2 * N_MIX_QKV * D_ATT
C_GATES = C_QKV + N_HEADS
_PREP_PARAMS = pltpu.CompilerParams(dimension_semantics=("parallel",), vmem_limit_bytes=40 * MIB)


def _cast_kernel(w_ref, o_ref):
    o_ref[...] = w_ref[...].astype(BF16)


def _cast_rows(w):
    rows, n_cols = w.shape
    return pl.pallas_call(
        _cast_kernel,
        grid=(rows // PREP_ROWS,),
        in_specs=[pl.BlockSpec((PREP_ROWS, n_cols), lambda i: (i, 0))],
        out_specs=pl.BlockSpec((PREP_ROWS, n_cols), lambda i: (i, 0)),
        out_shape=jax.ShapeDtypeStruct(w.shape, BF16),
        compiler_params=_PREP_PARAMS,
        name="cast_rows",
    )(w)


def _cast_column_tiles(w, tile):
    rows, n_cols = w.shape
    return pl.pallas_call(
        _cast_kernel,
        grid=(n_cols // tile,),
        in_specs=[pl.BlockSpec((rows, tile), lambda j: (0, j))],
        out_specs=pl.BlockSpec((None, rows, tile), lambda j: (j, 0, 0)),
        out_shape=jax.ShapeDtypeStruct((n_cols // tile, rows, tile), BF16),
        compiler_params=_PREP_PARAMS,
        name="cast_column_tiles",
    )(w)


def _qkv_weight_kernel(w_ref, o_ref):
    j = pl.program_id(0)
    rotary = (j % 2 == 0) & (j < 2 * (N_MIX_QKV - 1))

    @pl.when(rotary)
    def _():
        for hh in range(N_HEADS):
            base = hh * HEAD_DIM
            for dst, src, n in ((0, 0, ROPE_HALF),
                                (ROPE_HALF, ROPE_PARTNER, ROPE_HALF),
                                (ROPE_DIM, ROPE_DIM, ROPE_PARTNER - ROPE_DIM),
                                (ROPE_PARTNER, ROPE_HALF, ROPE_HALF),
                                (ROPE_PARTNER + ROPE_HALF, ROPE_PARTNER + ROPE_HALF,
                                 HEAD_DIM - ROPE_PARTNER - ROPE_HALF)):
                o_ref[base + dst:base + dst + n, :] = w_ref[base + src:base + src + n, :].astype(BF16)

    @pl.when(jnp.logical_not(rotary))
    def _():
        o_ref[...] = w_ref[...].astype(BF16)


def _qkv_weights(w_t):
    return pl.pallas_call(
        _qkv_weight_kernel,
        grid=(N_QKV_TILES,),
        in_specs=[pl.BlockSpec((D_ATT, D_MODEL), lambda j: ((j % 2) * N_MIX_QKV + j // 2, 0))],
        out_specs=pl.BlockSpec((None, D_ATT, D_MODEL), lambda j: (j, 0, 0)),
        out_shape=jax.ShapeDtypeStruct((N_QKV_TILES, D_ATT, D_MODEL), BF16),
        compiler_params=_PREP_PARAMS,
        name="qkv_weights",
    )(w_t)


F32_ROWS = 8
GATE_ROW_OFF = C_GATES % MRG_TJ


def _gate_weight_kernel(a_ref, b_ref, f_ref, o_ref, wf_ref):
    o_ref[...] = jnp.concatenate([a_ref[GATE_ROW_OFF:, :], b_ref[...]], axis=0).astype(BF16)

    @pl.when(pl.program_id(0) == 0)
    def _():
        pad = jnp.zeros((LANES - N_HEADS, D_MODEL), F32)
        wf_ref[...] = jnp.concatenate([f_ref[...], pad], axis=0).astype(BF16)


def _gate_weights(w_t):
    assert GATE_ROW_OFF == N_HEADS == F32_ROWS and C_QKV % MRG_TJ == 0
    first_tile = C_QKV // MRG_TJ
    return pl.pallas_call(
        _gate_weight_kernel,
        grid=(2 * MRG_STEPS,),
        in_specs=[
            pl.BlockSpec((MRG_TJ, D_MODEL), lambda t: (first_tile + t, 0)),
            pl.BlockSpec((GATE_ROW_OFF, D_MODEL),
                         lambda t: ((C_QKV + MRG_TJ * (t + 1)) // GATE_ROW_OFF, 0)),
            pl.BlockSpec((N_HEADS, D_MODEL), lambda t: (C_QKV // N_HEADS, 0)),
        ],
        out_specs=[
            pl.BlockSpec((None, MRG_TJ, D_MODEL), lambda t: (t, 0, 0)),
            pl.BlockSpec((LANES, D_MODEL), lambda t: (0, 0)),
        ],
        out_shape=[
            jax.ShapeDtypeStruct((2 * MRG_STEPS, MRG_TJ, D_MODEL), BF16),
            jax.ShapeDtypeStruct((LANES, D_MODEL), BF16),
        ],
        compiler_params=pltpu.CompilerParams(dimension_semantics=("arbitrary",),
                                             vmem_limit_bytes=40 * MIB),
        name="gate_weights",
    )(w_t, w_t, w_t)


def kernel(x, ffn1_norm, ffn1_w_gate, ffn1_w_up, ffn1_w_down, mix_norm, w_in, b_forget, b_gate_dil, b_gate_fox, w_proj_dil, w_proj_fox, w_out, ffn2_norm, ffn2_w_gate, ffn2_w_up, ffn2_w_down, final_norm):
    assert x.shape == (BATCH, SEQ, D_MODEL) and x.dtype == F32
    assert w_in.shape == (DEPTH, D_MODEL, C_GATES + 2 * D_MODEL)
    cos_full, sin_signed = _rope_tables()
    dil_band = _dil_band_bias()
    fgain = final_norm.reshape(1, D_MODEL)

    xt = x.reshape(N_TOK, D_MODEL)
    for l in range(DEPTH):
        last = l == DEPTH - 1
        w_t = jnp.swapaxes(w_in[l], 0, 1)
        w_qkv = _qkv_weights(w_t)
        w_gates, w_f = _gate_weights(w_t)
        b_f = jnp.pad(b_forget[l], (0, LANES - N_HEADS)).reshape(1, LANES)

        xt = _ffn(xt, ffn1_norm[l].reshape(1, D_MODEL), _cast_column_tiles(ffn1_w_gate[l], FFN_TF),
                  _cast_column_tiles(ffn1_w_up[l], FFN_TF), _cast_rows(ffn1_w_down[l]), fgain, False)

        qkv_dil, qk_fox, vt_fox, logf = _qkv_proj(xt, mix_norm[l].reshape(1, D_MODEL), w_qkv, w_f,
                                                  b_f, cos_full, sin_signed)
        c_gate = _cumsum(logf.reshape(BATCH, SEQ, LANES))
        y_fox = _fox_attn(qk_fox, vt_fox, c_gate)
        y_dil = _dil_attn(qkv_dil, dil_band)

        xt = _merge(xt, mix_norm[l].reshape(1, D_MODEL), y_dil, y_fox, w_gates,
                    b_gate_dil[l].reshape(1, D_MODEL), b_gate_fox[l].reshape(1, D_MODEL),
                    _cast_column_tiles(w_proj_dil[l], MRG_TJ),
                    _cast_column_tiles(w_proj_fox[l], MRG_TJ), _cast_rows(w_out[l]))

        xt = _ffn(xt, ffn2_norm[l].reshape(1, D_MODEL), _cast_column_tiles(ffn2_w_gate[l], FFN_TF),
                  _cast_column_tiles(ffn2_w_up[l], FFN_TF), _cast_rows(ffn2_w_down[l]), fgain, last)
    return xt.reshape(BATCH, SEQ, D_MODEL)
```

```python
import functools

import numpy as np
import jax
import jax.numpy as jnp
from jax import lax
from jax.experimental import pallas as pl
from jax.experimental.pallas import tpu as pltpu

D_MODEL = 2048
BATCH = 2
SEQ = 8192
DEPTH = 1
HEAD_DIM = 128
N_HEADS = 8
D_ATT = N_HEADS * HEAD_DIM
DIL_PATTERNS = ((128, 1), (512, 4), (2048, 16))
MAX_WINDOW = 2048
ROPE_THETA = 500000.0
ROPE_DIM = HEAD_DIM // 4
D_FF = 5632
NORM_EPS = 1e-6
N_TOK = BATCH * SEQ
LANES = 128

NEG_BIG = -1e30
LOG2E = 1.4426950408889634
MIB = 1024 * 1024

F32 = jnp.float32
BF16 = jnp.bfloat16


def _rmsnorm(x, gain):
    ms = jnp.mean(x * x, axis=-1, keepdims=True)
    return x * lax.rsqrt(ms + NORM_EPS) * gain


FFN_TM = 1024
FFN_TF = 512
FFN_STEPS = D_FF // FFN_TF


def _ffn_kernel(x_ref, gain_ref, wg_ref, wu_ref, wd_ref, wd_last_ref, fgain_ref, o_ref, h_ref, a_ref,
                *, final_norm):
    j = pl.program_id(1)
    last = FFN_STEPS - 1

    def gate_up():
        h = h_ref[...]
        g = jnp.dot(h, wg_ref[...], preferred_element_type=F32)
        u = jnp.dot(h, wu_ref[...], preferred_element_type=F32)
        a_ref[j % 2] = ((g * jax.nn.sigmoid(g)) * u).astype(BF16)

    def down():
        o_ref[...] += jnp.dot(a_ref[(j + 1) % 2], wd_ref[...], preferred_element_type=F32)

    @pl.when(j == 0)
    def _():
        h_ref[...] = _rmsnorm(x_ref[...], gain_ref[...]).astype(BF16)
        o_ref[...] = jnp.zeros_like(o_ref)
        gate_up()

    @pl.when((j > 0) & (j < last))
    def _():
        down()
        gate_up()

    @pl.when(j == last)
    def _():
        down()
        gate_up()
        o_ref[...] += jnp.dot(a_ref[last % 2], wd_last_ref[...], preferred_element_type=F32)
        y = x_ref[...] + 0.5 * o_ref[...]
        if final_norm:
            y = _rmsnorm(y, fgain_ref[...])
        o_ref[...] = y


def _ffn(x, gain, wg, wu, wd, fgain, final_norm):
    grid = (N_TOK // FFN_TM, FFN_STEPS)
    last = FFN_STEPS - 1
    return pl.pallas_call(
        functools.partial(_ffn_kernel, final_norm=final_norm),
        grid=grid,
        in_specs=[
            pl.BlockSpec((FFN_TM, D_MODEL), lambda i, j: (i, 0)),
            pl.BlockSpec((1, D_MODEL), lambda i, j: (0, 0)),
            pl.BlockSpec((None, D_MODEL, FFN_TF), lambda i, j: (j, 0, 0)),
            pl.BlockSpec((None, D_MODEL, FFN_TF), lambda i, j: (j, 0, 0)),
            pl.BlockSpec((FFN_TF, D_MODEL), lambda i, j: (jnp.maximum(j - 1, 0), 0)),
            pl.BlockSpec((FFN_TF, D_MODEL), lambda i, j: (last, 0), pipeline_mode=pl.Buffered(1)),
            pl.BlockSpec((1, D_MODEL), lambda i, j: (0, 0)),
        ],
        out_specs=pl.BlockSpec((FFN_TM, D_MODEL), lambda i, j: (i, 0)),
        out_shape=jax.ShapeDtypeStruct((N_TOK, D_MODEL), F32),
        scratch_shapes=[pltpu.VMEM((FFN_TM, D_MODEL), BF16),
                        pltpu.VMEM((2, FFN_TM, FFN_TF), BF16)],
        compiler_params=pltpu.CompilerParams(
            dimension_semantics=("parallel", "arbitrary"), vmem_limit_bytes=61 * MIB),
        name="ffn",
    )(x, gain, wg, wu, wd, wd, fgain)


QKV_TM = 1024
N_MIX_QKV = 3
QI, KI, VI = range(N_MIX_QKV)
N_QKV_TILES = 2 * N_MIX_QKV


ROPE_HALF = ROPE_DIM // 2
ROPE_PARTNER = LANES // 2

DIL_RES = max(d for _, d in DIL_PATTERNS)
DIL_RES_LO = 4
assert DIL_RES == DIL_RES_LO * DIL_RES_LO
DIL_UNIT = QKV_TM
ROWS_PER_RES = DIL_UNIT // DIL_RES


def _res_slot(res):
    return DIL_RES_LO * (res % DIL_RES_LO) + res // DIL_RES_LO


def _rope(x, cos, sin_signed):
    return x * cos + pltpu.roll(x, ROPE_PARTNER, 1) * sin_signed


NT_DIMS = (((1,), (1,)), ((), ()))


def _qkv_kernel(x_ref, gain_ref, w_ref, wf_ref, bf_ref, cos_ref, sin_ref,
                dil_ref, fox_ref, vt_ref, logf_ref, h_ref, r_ref, tmp_ref):
    n = pl.program_id(1)
    scale = (HEAD_DIM ** -0.5) * LOG2E
    is_q = n <= 2

    def matmul():
        r = lax.dot_general(h_ref[...], w_ref[...], NT_DIMS, preferred_element_type=F32)
        for hh in range(N_HEADS):
            r_ref[hh] = r[:, hh * HEAD_DIM:(hh + 1) * HEAD_DIM]

    def heads():
        return [r_ref[hh] for hh in range(N_HEADS)]

    def finish_dil(rotary, scaled):
        s = jnp.where(scaled, scale, 1.0)
        quarter = DIL_UNIT // DIL_RES_LO
        for hh in range(N_HEADS):
            for a in range(DIL_RES_LO):
                tmp_ref[a * quarter:(a + 1) * quarter, :] = r_ref[
                    hh, pl.ds(a, quarter, stride=DIL_RES_LO), :]
            for a in range(DIL_RES_LO):
                for b in range(DIL_RES_LO):
                    slot = DIL_RES_LO * a + b
                    rows = slice(slot * ROWS_PER_RES, (slot + 1) * ROWS_PER_RES)
                    t = tmp_ref[pl.ds(a * quarter + b, ROWS_PER_RES, stride=DIL_RES_LO), :]
                    cos = jnp.where(rotary, cos_ref[rows, :], 1.0) * s
                    sin = jnp.where(rotary, sin_ref[rows, :], 0.0) * s
                    dil_ref[hh, rows, :] = _rope(t, cos, sin)

    def finish_fox(scaled):
        s = jnp.where(scaled, scale, 1.0)
        for hh, t in enumerate(heads()):
            fox_ref[hh] = (t * s).astype(BF16)

    @pl.when(n == 0)
    def _():
        h = _rmsnorm(x_ref[...], gain_ref[...]).astype(BF16)
        h_ref[...] = h
        f = lax.dot_general(h, wf_ref[...], NT_DIMS, preferred_element_type=F32) + bf_ref[...]
        logf_ref[...] = jnp.minimum(f, 0.0) - jnp.log1p(jnp.exp(-jnp.abs(f)))
        matmul()

    @pl.when((n == 1) | (n == 3))
    def _():
        finish_dil(True, is_q)
        matmul()

    @pl.when((n == 2) | (n == 4))
    def _():
        finish_fox(is_q)
        matmul()

    @pl.when(n == N_QKV_TILES - 1)
    def _():
        finish_dil(False, False)
        matmul()
        for hh, t in enumerate(heads()):
            vt_ref[hh] = t.T.astype(BF16)


def _qkv_proj(x, gain, w_qkv, w_f, b_f, cos, sin_signed):
    s_tiles = SEQ // QKV_TM
    grid = (N_TOK // QKV_TM, N_QKV_TILES)

    def head_block(m, which):
        return (which, m // s_tiles, 0, m % s_tiles, 0)

    return pl.pallas_call(
        _qkv_kernel,
        grid=grid,
        in_specs=[
            pl.BlockSpec((QKV_TM, D_MODEL), lambda m, n: (m, 0)),
            pl.BlockSpec((1, D_MODEL), lambda m, n: (0, 0)),
            pl.BlockSpec((None, D_ATT, D_MODEL), lambda m, n: (n, 0, 0)),
            pl.BlockSpec((LANES, D_MODEL), lambda m, n: (0, 0)),
            pl.BlockSpec((1, LANES), lambda m, n: (0, 0)),
            pl.BlockSpec((QKV_TM, LANES), lambda m, n: (m % s_tiles, 0)),
            pl.BlockSpec((QKV_TM, LANES), lambda m, n: (m % s_tiles, 0)),
        ],
        out_specs=[
            pl.BlockSpec((None, None, N_HEADS, QKV_TM, HEAD_DIM),
                         lambda m, n: head_block(m, jnp.maximum(n - 1, 0) // 2)),
            pl.BlockSpec((None, None, N_HEADS, QKV_TM, HEAD_DIM),
                         lambda m, n: head_block(m, jnp.clip(n - 3, 0, 1))),
            pl.BlockSpec((None, N_HEADS, HEAD_DIM, QKV_TM),
                         lambda m, n: (m // s_tiles, 0, 0, m % s_tiles)),
            pl.BlockSpec((QKV_TM, LANES), lambda m, n: (m, 0)),
        ],
        out_shape=[
            jax.ShapeDtypeStruct((N_MIX_QKV, BATCH, N_HEADS, SEQ, HEAD_DIM), F32),
            jax.ShapeDtypeStruct((2, BATCH, N_HEADS, SEQ, HEAD_DIM), BF16),
            jax.ShapeDtypeStruct((BATCH, N_HEADS, HEAD_DIM, SEQ), BF16),
            jax.ShapeDtypeStruct((N_TOK, LANES), F32),
        ],
        scratch_shapes=[pltpu.VMEM((QKV_TM, D_MODEL), BF16),
                        pltpu.VMEM((N_HEADS, QKV_TM, HEAD_DIM), F32),
                        pltpu.VMEM((QKV_TM, HEAD_DIM), F32)],
        compiler_params=pltpu.CompilerParams(
            dimension_semantics=("parallel", "arbitrary"), vmem_limit_bytes=61 * MIB),
        name="qkv_proj",
    )(x, gain, w_qkv, w_f, b_f, cos, sin_signed)


CUM_TS = 1024
CUM_CHUNK = 128


def _cumsum_kernel(lf_ref, c_ref, carry_ref):
    @pl.when(pl.program_id(1) == 0)
    def _():
        carry_ref[...] = jnp.zeros_like(carry_ref)

    r = lax.broadcasted_iota(jnp.int32, (CUM_CHUNK, CUM_CHUNK), 0)
    c = lax.broadcasted_iota(jnp.int32, (CUM_CHUNK, CUM_CHUNK), 1)
    tri = (c <= r).astype(F32)
    carry = carry_ref[...]
    for i in range(CUM_TS // CUM_CHUNK):
        rows = slice(i * CUM_CHUNK, (i + 1) * CUM_CHUNK)
        cs = jnp.dot(tri, lf_ref[rows, :], preferred_element_type=F32,
                     precision=lax.Precision.HIGHEST) + carry
        c_ref[rows, :] = cs
        carry = cs[CUM_CHUNK - 1:CUM_CHUNK, :]
    carry_ref[...] = carry


def _cumsum(logf):
    return pl.pallas_call(
        _cumsum_kernel,
        grid=(BATCH, SEQ // CUM_TS),
        in_specs=[pl.BlockSpec((None, CUM_TS, LANES), lambda b, s: (b, s, 0))],
        out_specs=pl.BlockSpec((None, CUM_TS, LANES), lambda b, s: (b, s, 0)),
        out_shape=jax.ShapeDtypeStruct((BATCH, SEQ, LANES), F32),
        scratch_shapes=[pltpu.VMEM((1, LANES), F32)],
        compiler_params=pltpu.CompilerParams(dimension_semantics=("parallel", "arbitrary")),
        name="gate_cumsum",
    )(logf)


FOX_TQ = 2048
FOX_TK = 512
FOX_DIAG = FOX_TQ // FOX_TK
BF16_ROWS = 16
FOX_VT_ROWS = HEAD_DIM + BF16_ROWS
N_SPLIT = 3


def _gate_columns(c_tile, head, key_side):
    lane = lax.broadcasted_iota(jnp.int32, c_tile.shape, 1)
    c = jnp.sum(jnp.where(lane == head, c_tile, 0.0), axis=-1, keepdims=True) * LOG2E
    if key_side:
        c = -c
    base = N_SPLIT if key_side else 0
    cols = jnp.where((lane >= N_SPLIT - base) & (lane < 2 * N_SPLIT - base), 1.0, 0.0)
    rest = c
    for t in range(N_SPLIT):
        term = rest.astype(BF16).astype(F32)
        cols = jnp.where(lane == base + t, term, cols)
        rest = rest - term
    return cols.astype(BF16)


def _fox_kernel(q_ref, k_ref, vt_ref, c_ref, o_ref, kx_ref, st_ref, mx_ref, m_ref, acc_ref):
    head = pl.program_id(0) % N_HEADS
    qi = pl.program_id(1)
    ones_tile = (lax.broadcasted_iota(jnp.int32, (BF16_ROWS, FOX_TK), 0) == 0).astype(BF16)

    @pl.when(qi == 0)
    def _():
        def build(i, carry):
            rows = pl.ds(pl.multiple_of(i * FOX_TK, FOX_TK), FOX_TK)
            kx_ref[rows, :] = _gate_columns(c_ref[rows, :], head, True)
            return carry

        lax.fori_loop(0, SEQ // FOX_TK, build, 0)

    q_rows = pl.ds(pl.multiple_of(qi * FOX_TQ, FOX_TQ), FOX_TQ)
    q_aug = jnp.concatenate([q_ref[...], _gate_columns(c_ref[q_rows, :], head, False)], axis=-1)

    def key_rows(kb):
        return pl.ds(pl.multiple_of(kb * FOX_TK, FOX_TK), FOX_TK)

    def scores(kb, slot, q_lo=0):
        rows = key_rows(kb)
        k_aug = jnp.concatenate([k_ref[rows, :], kx_ref[rows, :]], axis=-1)
        st = lax.dot_general(k_aug, q_aug[q_lo:, :], (((1,), (1,)), ((), ())),
                             preferred_element_type=F32)
        st_ref[slot, :, q_lo:] = st
        mx_ref[slot, :, q_lo:] = jnp.max(st, axis=0, keepdims=True)

    def accumulate(kb, slot, q_lo=0, key_lo=None):
        st = st_ref[slot, :, q_lo:]
        if key_lo is None:
            st_max = mx_ref[slot, :, q_lo:]
        else:
            key = lax.broadcasted_iota(jnp.int32, st.shape, 0) + key_lo
            qry = lax.broadcasted_iota(jnp.int32, st.shape, 1) + q_lo
            st = jnp.where(key <= qry, st, NEG_BIG)
            st_max = jnp.max(st, axis=0, keepdims=True)
        m = m_ref[:, q_lo:]
        m_new = jnp.maximum(m, st_max)
        p = jnp.exp2(st - m_new).astype(BF16)
        vt = jnp.concatenate([vt_ref[:, key_rows(kb)], ones_tile], axis=0)
        pv = jnp.dot(vt, p, preferred_element_type=F32)
        acc_ref[:, q_lo:] = jnp.exp2(m - m_new) * acc_ref[:, q_lo:] + pv
        m_ref[:, q_lo:] = m_new

    m_ref[...] = jnp.full(m_ref.shape, NEG_BIG, F32)
    acc_ref[...] = jnp.zeros_like(acc_ref)
    scores(0, 0)

    def pair(j, carry):
        scores(2 * j + 1, 1)
        accumulate(2 * j, 0)
        scores(2 * j + 2, 0)
        accumulate(2 * j + 1, 1)
        return carry

    first = qi * FOX_DIAG
    lax.fori_loop(0, qi * (FOX_DIAG // 2), pair, 0)
    for d in range(FOX_DIAG):
        if d + 1 < FOX_DIAG:
            scores(first + d + 1, (d + 1) % 2, q_lo=(d + 1) * FOX_TK)
        accumulate(first + d, d % 2, q_lo=d * FOX_TK, key_lo=d * FOX_TK)

    out_t = acc_ref[0:HEAD_DIM, :] / acc_ref[HEAD_DIM:HEAD_DIM + 1, :]
    o_ref[...] = out_t.T.astype(BF16)


def _fox_attn(qk, vt, c_gate):
    assert FOX_TQ % (2 * FOX_TK) == 0
    q_tiles = SEQ // FOX_TQ
    return pl.pallas_call(
        _fox_kernel,
        grid=(BATCH * N_HEADS, q_tiles),
        in_specs=[
            pl.BlockSpec((None, None, None, FOX_TQ, HEAD_DIM),
                         lambda bh, qi: (QI, bh // N_HEADS, bh % N_HEADS, qi, 0)),
            pl.BlockSpec((None, None, None, SEQ, HEAD_DIM),
                         lambda bh, qi: (KI, bh // N_HEADS, bh % N_HEADS, 0, 0)),
            pl.BlockSpec((None, None, HEAD_DIM, SEQ),
                         lambda bh, qi: (bh // N_HEADS, bh % N_HEADS, 0, 0)),
            pl.BlockSpec((None, SEQ, LANES), lambda bh, qi: (bh // N_HEADS, 0, 0)),
        ],
        out_specs=pl.BlockSpec((FOX_TQ, HEAD_DIM),
                               lambda bh, qi: ((bh // N_HEADS) * q_tiles + qi, bh % N_HEADS)),
        out_shape=jax.ShapeDtypeStruct((N_TOK, D_ATT), BF16),
        scratch_shapes=[
            pltpu.VMEM((SEQ, LANES), BF16),
            pltpu.VMEM((2, FOX_TK, FOX_TQ), F32),
            pltpu.VMEM((2, 1, FOX_TQ), F32),
            pltpu.VMEM((1, FOX_TQ), F32),
            pltpu.VMEM((FOX_VT_ROWS, FOX_TQ), F32),
        ],
        compiler_params=pltpu.CompilerParams(
            dimension_semantics=("parallel", "arbitrary"), vmem_limit_bytes=48 * MIB),
        name="fox_attn",
    )(qk, qk, vt, c_gate)


DIL_GROUP = 128
DIL_SUPER = DIL_GROUP * max(d for _, d in DIL_PATTERNS)
N_PAT = len(DIL_PATTERNS)
assert all(w == DIL_GROUP * d and DIL_SUPER % w == 0 for w, d in DIL_PATTERNS)


def _dil_group_order(d):
    cycle = DIL_RES // d
    run = DIL_GROUP // cycle
    i = np.arange(DIL_GROUP)
    return cycle * (i % run) + i // run


def _dil_band_bias():
    tables = []
    for _, d in DIL_PATTERNS:
        order = _dil_group_order(d)
        k = order[:, None]
        j = np.concatenate([order, DIL_GROUP + order])[None, :]
        tables.append(np.where((j >= k) & (j <= k + DIL_GROUP), 0.0, NEG_BIG))
    return jnp.asarray(np.stack(tables), F32)


def _dil_chunks(d, u, c):
    cycle = DIL_RES // d
    run = DIL_GROUP // cycle
    t_rel = u * DIL_GROUP * d
    out = []
    for b in range(cycle):
        for lo in range(0, run, ROWS_PER_RES):
            unit, off = divmod(t_rel + DIL_RES * lo, DIL_UNIT)
            out.append((unit * DIL_UNIT + ROWS_PER_RES * _res_slot(c + d * b) + off // DIL_RES,
                        min(run - lo, ROWS_PER_RES)))
    return out


def _dil_kernel(q_ref, k_ref, v_ref, band_ref, o_ref, num_ref, den_ref, mx_ref, nat_ref):
    sb = pl.program_id(1)
    base = sb * DIL_SUPER
    col = lax.broadcasted_iota(jnp.int32, (DIL_GROUP, 2 * DIL_GROUP), 1)
    ones = jnp.ones((2 * DIL_GROUP, HEAD_DIM), BF16)

    def gather(ref, pieces, dynamic):
        parts = []
        for start, n in pieces:
            if dynamic:
                row = base + start
                row = jnp.maximum(row, 0) if start < 0 else row
                parts.append(ref[pl.ds(pl.multiple_of(row, F32_ROWS), n), :])
            else:
                parts.append(ref[start:start + n, :])
        return jnp.concatenate(parts, axis=0)

    for g, (_, d) in enumerate(DIL_PATTERNS):
        band = band_ref[g]
        band_first = jnp.where((sb == 0) & (col < DIL_GROUP), NEG_BIG, band)
        for u in range(DIL_SUPER // (DIL_GROUP * d)):
            bias = band_first if u == 0 else band
            for c in range(d):
                own = _dil_chunks(d, u, c)
                window = _dil_chunks(d, u - 1, c) + own
                q = gather(q_ref, own, False).astype(BF16)
                k = gather(k_ref, window, True).astype(BF16)
                v = gather(v_ref, window, True).astype(BF16)
                s = lax.dot_general(q, k, NT_DIMS, preferred_element_type=F32) + bias
                m = jnp.max(s, axis=-1, keepdims=True)
                p = jnp.exp2(s - m).astype(BF16)
                r = jnp.dot(p, jnp.concatenate([v, ones], axis=-1), preferred_element_type=F32)
                m_rep = jnp.broadcast_to(m, (DIL_GROUP, HEAD_DIM))
                at = 0
                for start, n in own:
                    num_ref[g, start:start + n, :] = r[at:at + n, :HEAD_DIM]
                    den_ref[g, start:start + n, :] = r[at:at + n, HEAD_DIM:]
                    mx_ref[g, start:start + n, :] = m_rep[at:at + n, :]
                    at += n

    def combine(i, carry):
        r = pl.ds(pl.multiple_of(i * ROWS_PER_RES, ROWS_PER_RES), ROWS_PER_RES)
        mx = [mx_ref[g, r, :] for g in range(N_PAT)]
        top = functools.reduce(jnp.maximum, mx)
        wgt = [jnp.exp2(t - top) for t in mx]
        num = functools.reduce(jnp.add, [wgt[g] * num_ref[g, r, :] for g in range(N_PAT)])
        den = functools.reduce(jnp.add, [wgt[g] * den_ref[g, r, :] for g in range(N_PAT)])
        unit, res = i // DIL_RES, _res_slot(i % DIL_RES)
        nat_ref[pl.ds(unit * DIL_UNIT + res, ROWS_PER_RES, stride=DIL_RES), :] = num / den
        return carry

    lax.fori_loop(0, DIL_SUPER // ROWS_PER_RES, combine, 0)
    o_ref[...] = nat_ref[...].astype(BF16)


def _dil_attn(qkv, band):
    steps = SEQ // DIL_SUPER
    per_pattern = pltpu.VMEM((N_PAT, DIL_SUPER, HEAD_DIM), F32)
    return pl.pallas_call(
        _dil_kernel,
        grid=(BATCH * N_HEADS, steps),
        in_specs=[
            pl.BlockSpec((None, None, None, DIL_SUPER, HEAD_DIM),
                         lambda bh, sb: (QI, bh // N_HEADS, bh % N_HEADS, sb, 0)),
            pl.BlockSpec((None, None, None, SEQ, HEAD_DIM),
                         lambda bh, sb: (KI, bh // N_HEADS, bh % N_HEADS, 0, 0)),
            pl.BlockSpec((None, None, None, SEQ, HEAD_DIM),
                         lambda bh, sb: (VI, bh // N_HEADS, bh % N_HEADS, 0, 0)),
            pl.BlockSpec((N_PAT, DIL_GROUP, 2 * DIL_GROUP), lambda bh, sb: (0, 0, 0)),
        ],
        out_specs=pl.BlockSpec((DIL_SUPER, HEAD_DIM),
                               lambda bh, sb: ((bh // N_HEADS) * steps + sb, bh % N_HEADS)),
        out_shape=jax.ShapeDtypeStruct((N_TOK, D_ATT), BF16),
        scratch_shapes=[per_pattern, per_pattern, per_pattern,
                        pltpu.VMEM((DIL_SUPER, HEAD_DIM), F32)],
        compiler_params=pltpu.CompilerParams(
            dimension_semantics=("parallel", "arbitrary"), vmem_limit_bytes=48 * MIB),
        name="dil_attn",
    )(qkv, qkv, qkv, band)


MRG_TM = 512
MRG_TJ = 512


MRG_STEPS = D_MODEL // MRG_TJ


def _merge_kernel(x_ref, gain_ref, yd_ref, yf_ref, wgd_ref, wgf_ref, bgd_ref, bgf_ref,
                  wpd_ref, wpf_ref, wo_ref, wo_last_ref, o_ref, h_ref, m_ref):
    j = pl.program_id(1)
    last = MRG_STEPS - 1

    def merge_tile():
        h = h_ref[...]
        gd = lax.dot_general(h, wgd_ref[...], NT_DIMS, preferred_element_type=F32) + bgd_ref[...]
        gf = lax.dot_general(h, wgf_ref[...], NT_DIMS, preferred_element_type=F32) + bgf_ref[...]
        pd = jnp.dot(yd_ref[...], wpd_ref[...], preferred_element_type=F32)
        pf = jnp.dot(yf_ref[...], wpf_ref[...], preferred_element_type=F32)
        m_ref[j % 2] = (jax.nn.sigmoid(gd) * pd + jax.nn.sigmoid(gf) * pf).astype(BF16)

    def project():
        o_ref[...] += jnp.dot(m_ref[(j + 1) % 2], wo_ref[...], preferred_element_type=F32)

    @pl.when(j == 0)
    def _():
        h_ref[...] = _rmsnorm(x_ref[...], gain_ref[...]).astype(BF16)
        o_ref[...] = jnp.zeros_like(o_ref)
        merge_tile()

    @pl.when((j > 0) & (j < last))
    def _():
        project()
        merge_tile()

    @pl.when(j == last)
    def _():
        project()
        merge_tile()
        o_ref[...] += jnp.dot(m_ref[last % 2], wo_last_ref[...], preferred_element_type=F32)
        o_ref[...] = x_ref[...] + o_ref[...]


def _merge(x, gain, yd, yf, w_gates, bgd, bgf, wpd, wpf, wo):
    grid = (N_TOK // MRG_TM, MRG_STEPS)
    last = MRG_STEPS - 1
    return pl.pallas_call(
        _merge_kernel,
        grid=grid,
        in_specs=[
            pl.BlockSpec((MRG_TM, D_MODEL), lambda i, j: (i, 0)),
            pl.BlockSpec((1, D_MODEL), lambda i, j: (0, 0)),
            pl.BlockSpec((MRG_TM, D_ATT), lambda i, j: (i, 0)),
            pl.BlockSpec((MRG_TM, D_ATT), lambda i, j: (i, 0)),
            pl.BlockSpec((None, MRG_TJ, D_MODEL), lambda i, j: (j, 0, 0)),
            pl.BlockSpec((None, MRG_TJ, D_MODEL), lambda i, j: (MRG_STEPS + j, 0, 0)),
            pl.BlockSpec((1, MRG_TJ), lambda i, j: (0, j)),
            pl.BlockSpec((1, MRG_TJ), lambda i, j: (0, j)),
            pl.BlockSpec((None, D_ATT, MRG_TJ), lambda i, j: (j, 0, 0)),
            pl.BlockSpec((None, D_ATT, MRG_TJ), lambda i, j: (j, 0, 0)),
            pl.BlockSpec((MRG_TJ, D_MODEL), lambda i, j: (jnp.maximum(j - 1, 0), 0)),
            pl.BlockSpec((MRG_TJ, D_MODEL), lambda i, j: (last, 0), pipeline_mode=pl.Buffered(1)),
        ],
        out_specs=pl.BlockSpec((MRG_TM, D_MODEL), lambda i, j: (i, 0)),
        out_shape=jax.ShapeDtypeStruct((N_TOK, D_MODEL), F32),
        scratch_shapes=[pltpu.VMEM((MRG_TM, D_MODEL), BF16),
                        pltpu.VMEM((2, MRG_TM, MRG_TJ), BF16)],
        compiler_params=pltpu.CompilerParams(
            dimension_semantics=("parallel", "arbitrary"), vmem_limit_bytes=56 * MIB),
        name="merge_proj",
    )(x, gain, yd, yf, w_gates, w_gates, bgd, bgf, wpd, wpf, wo, wo)


def _rope_tables():
    pos = np.arange(SEQ, dtype=np.float64)
    inv_freq = ROPE_THETA ** (-np.arange(0, ROPE_DIM, 2, dtype=np.float64) / ROPE_DIM)
    ang = pos[:, None] * inv_freq[None, :]
    cos, sin = np.cos(ang), np.sin(ang)
    gap = ROPE_PARTNER - ROPE_HALF
    tail = HEAD_DIM - ROPE_PARTNER - ROPE_HALF
    one, zero = (lambda n: np.ones((SEQ, n))), (lambda n: np.zeros((SEQ, n)))
    cos_full = np.concatenate([cos, one(gap), cos, one(tail)], axis=-1)
    sin_signed = np.concatenate([-sin, zero(gap), sin, zero(tail)], axis=-1)
    assert cos_full.shape == (SEQ, HEAD_DIM)

    def residue_major(t):
        t = t.reshape(SEQ // DIL_UNIT, ROWS_PER_RES, DIL_RES, HEAD_DIM).transpose(0, 2, 1, 3)
        t = t[:, [_res_slot(slot) for slot in range(DIL_RES)]]
        return jnp.asarray(t.reshape(SEQ, HEAD_DIM), F32)

    return residue_major(cos_full), residue_major(sin_signed)


PREP_ROWS = 512
C_QKV = 2 * N_MIX_QKV * D_ATT
C_GATES = C_QKV + N_HEADS
_PREP_PARAMS = pltpu.CompilerParams(dimension_semantics=("parallel",), vmem_limit_bytes=40 * MIB)


def _cast_kernel(w_ref, o_ref):
    o_ref[...] = w_ref[...].astype(BF16)


def _cast_rows(w):
    rows, n_cols = w.shape
    return pl.pallas_call(
        _cast_kernel,
        grid=(rows // PREP_ROWS,),
        in_specs=[pl.BlockSpec((PREP_ROWS, n_cols), lambda i: (i, 0))],
        out_specs=pl.BlockSpec((PREP_ROWS, n_cols), lambda i: (i, 0)),
        out_shape=jax.ShapeDtypeStruct(w.shape, BF16),
        compiler_params=_PREP_PARAMS,
        name="cast_rows",
    )(w)


def _cast_column_tiles(w, tile):
    rows, n_cols = w.shape
    return pl.pallas_call(
        _cast_kernel,
        grid=(n_cols // tile,),
        in_specs=[pl.BlockSpec((rows, tile), lambda j: (0, j))],
        out_specs=pl.BlockSpec((None, rows, tile), lambda j: (j, 0, 0)),
        out_shape=jax.ShapeDtypeStruct((n_cols // tile, rows, tile), BF16),
        compiler_params=_PREP_PARAMS,
        name="cast_column_tiles",
    )(w)


def _qkv_weight_kernel(w_ref, o_ref):
    j = pl.program_id(0)
    rotary = (j % 2 == 0) & (j < 2 * (N_MIX_QKV - 1))

    @pl.when(rotary)
    def _():
        for hh in range(N_HEADS):
            base = hh * HEAD_DIM
            for dst, src, n in ((0, 0, ROPE_HALF),
                                (ROPE_HALF, ROPE_PARTNER, ROPE_HALF),
                                (ROPE_DIM, ROPE_DIM, ROPE_PARTNER - ROPE_DIM),
                                (ROPE_PARTNER, ROPE_HALF, ROPE_HALF),
                                (ROPE_PARTNER + ROPE_HALF, ROPE_PARTNER + ROPE_HALF,
                                 HEAD_DIM - ROPE_PARTNER - ROPE_HALF)):
                o_ref[base + dst:base + dst + n, :] = w_ref[base + src:base + src + n, :].astype(BF16)

    @pl.when(jnp.logical_not(rotary))
    def _():
        o_ref[...] = w_ref[...].astype(BF16)


def _qkv_weights(w_t):
    return pl.pallas_call(
        _qkv_weight_kernel,
        grid=(N_QKV_TILES,),
        in_specs=[pl.BlockSpec((D_ATT, D_MODEL), lambda j: ((j % 2) * N_MIX_QKV + j // 2, 0))],
        out_specs=pl.BlockSpec((None, D_ATT, D_MODEL), lambda j: (j, 0, 0)),
        out_shape=jax.ShapeDtypeStruct((N_QKV_TILES, D_ATT, D_MODEL), BF16),
        compiler_params=_PREP_PARAMS,
        name="qkv_weights",
    )(w_t)


F32_ROWS = 8
GATE_ROW_OFF = C_GATES % MRG_TJ


def _gate_weight_kernel(a_ref, b_ref, f_ref, o_ref, wf_ref):
    o_ref[...] = jnp.concatenate([a_ref[GATE_ROW_OFF:, :], b_ref[...]], axis=0).astype(BF16)

    @pl.when(pl.program_id(0) == 0)
    def _():
        pad = jnp.zeros((LANES - N_HEADS, D_MODEL), F32)
        wf_ref[...] = jnp.concatenate([f_ref[...], pad], axis=0).astype(BF16)


def _gate_weights(w_t):
    assert GATE_ROW_OFF == N_HEADS == F32_ROWS and C_QKV % MRG_TJ == 0
    first_tile = C_QKV // MRG_TJ
    return pl.pallas_call(
        _gate_weight_kernel,
        grid=(2 * MRG_STEPS,),
        in_specs=[
            pl.BlockSpec((MRG_TJ, D_MODEL), lambda t: (first_tile + t, 0)),
            pl.BlockSpec((GATE_ROW_OFF, D_MODEL),
                         lambda t: ((C_QKV + MRG_TJ * (t + 1)) // GATE_ROW_OFF, 0)),
            pl.BlockSpec((N_HEADS, D_MODEL), lambda t: (C_QKV // N_HEADS, 0)),
        ],
        out_specs=[
            pl.BlockSpec((None, MRG_TJ, D_MODEL), lambda t: (t, 0, 0)),
            pl.BlockSpec((LANES, D_MODEL), lambda t: (0, 0)),
        ],
        out_shape=[
            jax.ShapeDtypeStruct((2 * MRG_STEPS, MRG_TJ, D_MODEL), BF16),
            jax.ShapeDtypeStruct((LANES, D_MODEL), BF16),
        ],
        compiler_params=pltpu.CompilerParams(dimension_semantics=("arbitrary",),
                                             vmem_limit_bytes=40 * MIB),
        name="gate_weights",
    )(w_t, w_t, w_t)


def kernel(x, ffn1_norm, ffn1_w_gate, ffn1_w_up, ffn1_w_down, mix_norm, w_in, b_forget, b_gate_dil, b_gate_fox, w_proj_dil, w_proj_fox, w_out, ffn2_norm, ffn2_w_gate, ffn2_w_up, ffn2_w_down, final_norm):
    assert x.shape == (BATCH, SEQ, D_MODEL) and x.dtype == F32
    assert w_in.shape == (DEPTH, D_MODEL, C_GATES + 2 * D_MODEL)
    cos_full, sin_signed = _rope_tables()
    dil_band = _dil_band_bias()
    fgain = final_norm.reshape(1, D_MODEL)

    xt = x.reshape(N_TOK, D_MODEL)
    for l in range(DEPTH):
        last = l == DEPTH - 1
        w_t = jnp.swapaxes(w_in[l], 0, 1)
        w_qkv = _qkv_weights(w_t)
        w_gates, w_f = _gate_weights(w_t)
        b_f = jnp.pad(b_forget[l], (0, LANES - N_HEADS)).reshape(1, LANES)

        xt = _ffn(xt, ffn1_norm[l].reshape(1, D_MODEL), _cast_column_tiles(ffn1_w_gate[l], FFN_TF),
                  _cast_column_tiles(ffn1_w_up[l], FFN_TF), _cast_rows(ffn1_w_down[l]), fgain, False)

        qkv_dil, qk_fox, vt_fox, logf = _qkv_proj(xt, mix_norm[l].reshape(1, D_MODEL), w_qkv, w_f,
                                                  b_f, cos_full, sin_signed)
        c_gate = _cumsum(logf.reshape(BATCH, SEQ, LANES))
        y_fox = _fox_attn(qk_fox, vt_fox, c_gate)
        y_dil = _dil_attn(qkv_dil, dil_band)

        xt = _merge(xt, mix_norm[l].reshape(1, D_MODEL), y_dil, y_fox, w_gates,
                    b_gate_dil[l].reshape(1, D_MODEL), b_gate_fox[l].reshape(1, D_MODEL),
                    _cast_column_tiles(w_proj_dil[l], MRG_TJ),
                    _cast_column_tiles(w_proj_fox[l], MRG_TJ), _cast_rows(w_out[l]))

        xt = _ffn(xt, ffn2_norm[l].reshape(1, D_MODEL), _cast_column_tiles(ffn2_w_gate[l], FFN_TF),
                  _cast_column_tiles(ffn2_w_up[l], FFN_TF), _cast_rows(ffn2_w_down[l]), fgain, last)
    return xt.reshape(BATCH, SEQ, D_MODEL)
```

```python
import functools

import numpy as np
import jax
import jax.numpy as jnp
from jax import lax
from jax.experimental import pallas as pl
from jax.experimental.pallas import tpu as pltpu

D_MODEL = 2048
BATCH = 2
SEQ = 8192
DEPTH = 1
HEAD_DIM = 128
N_HEADS = 8
D_ATT = N_HEADS * HEAD_DIM
DIL_PATTERNS = ((128, 1), (512, 4), (2048, 16))
MAX_WINDOW = 2048
ROPE_THETA = 500000.0
ROPE_DIM = HEAD_DIM // 4
D_FF = 5632
NORM_EPS = 1e-6
N_TOK = BATCH * SEQ
LANES = 128

NEG_BIG = -1e30
LOG2E = 1.4426950408889634
MIB = 1024 * 1024

F32 = jnp.float32
BF16 = jnp.bfloat16


def _rmsnorm(x, gain):
    ms = jnp.mean(x * x, axis=-1, keepdims=True)
    return x * lax.rsqrt(ms + NORM_EPS) * gain


FFN_TM = 1024
FFN_TF = 512
FFN_STEPS = D_FF // FFN_TF


def _ffn_kernel(x_ref, gain_ref, wg_ref, wu_ref, wd_ref, wd_last_ref, fgain_ref, o_ref, h_ref, a_ref,
                *, final_norm):
    j = pl.program_id(1)
    last = FFN_STEPS - 1

    def gate_up():
        h = h_ref[...]
        g = jnp.dot(h, wg_ref[...], preferred_element_type=F32)
        u = jnp.dot(h, wu_ref[...], preferred_element_type=F32)
        a_ref[j % 2] = ((g * jax.nn.sigmoid(g)) * u).astype(BF16)

    def down():
        o_ref[...] += jnp.dot(a_ref[(j + 1) % 2], wd_ref[...], preferred_element_type=F32)

    @pl.when(j == 0)
    def _():
        h_ref[...] = _rmsnorm(x_ref[...], gain_ref[...]).astype(BF16)
        o_ref[...] = jnp.zeros_like(o_ref)
        gate_up()

    @pl.when((j > 0) & (j < last))
    def _():
        down()
        gate_up()

    @pl.when(j == last)
    def _():
        down()
        gate_up()
        o_ref[...] += jnp.dot(a_ref[last % 2], wd_last_ref[...], preferred_element_type=F32)
        y = x_ref[...] + 0.5 * o_ref[...]
        if final_norm:
            y = _rmsnorm(y, fgain_ref[...])
        o_ref[...] = y


def _ffn(x, gain, wg, wu, wd, fgain, final_norm):
    grid = (N_TOK // FFN_TM, FFN_STEPS)
    last = FFN_STEPS - 1
    return pl.pallas_call(
        functools.partial(_ffn_kernel, final_norm=final_norm),
        grid=grid,
        in_specs=[
            pl.BlockSpec((FFN_TM, D_MODEL), lambda i, j: (i, 0)),
            pl.BlockSpec((1, D_MODEL), lambda i, j: (0, 0)),
            pl.BlockSpec((None, D_MODEL, FFN_TF), lambda i, j: (j, 0, 0)),
            pl.BlockSpec((None, D_MODEL, FFN_TF), lambda i, j: (j, 0, 0)),
            pl.BlockSpec((FFN_TF, D_MODEL), lambda i, j: (jnp.maximum(j - 1, 0), 0)),
            pl.BlockSpec((FFN_TF, D_MODEL), lambda i, j: (last, 0), pipeline_mode=pl.Buffered(1)),
            pl.BlockSpec((1, D_MODEL), lambda i, j: (0, 0)),
        ],
        out_specs=pl.BlockSpec((FFN_TM, D_MODEL), lambda i, j: (i, 0)),
        out_shape=jax.ShapeDtypeStruct((N_TOK, D_MODEL), F32),
        scratch_shapes=[pltpu.VMEM((FFN_TM, D_MODEL), BF16),
                        pltpu.VMEM((2, FFN_TM, FFN_TF), BF16)],
        compiler_params=pltpu.CompilerParams(
            dimension_semantics=("parallel", "arbitrary"), vmem_limit_bytes=61 * MIB),
        name="ffn",
    )(x, gain, wg, wu, wd, wd, fgain)


QKV_TM = 1024
N_MIX_QKV = 3
QI, KI, VI = range(N_MIX_QKV)
N_QKV_TILES = 2 * N_MIX_QKV


ROPE_HALF = ROPE_DIM // 2
ROPE_PARTNER = LANES // 2

DIL_RES = max(d for _, d in DIL_PATTERNS)
DIL_RES_LO = 4
assert DIL_RES == DIL_RES_LO * DIL_RES_LO
DIL_UNIT = QKV_TM
ROWS_PER_RES = DIL_UNIT // DIL_RES


def _res_slot(res):
    return DIL_RES_LO * (res % DIL_RES_LO) + res // DIL_RES_LO


def _rope(x, cos, sin_signed):
    return x * cos + pltpu.roll(x, ROPE_PARTNER, 1) * sin_signed


NT_DIMS = (((1,), (1,)), ((), ()))


def _qkv_kernel(x_ref, gain_ref, w_ref, wf_ref, bf_ref, cos_ref, sin_ref,
                dil_ref, fox_ref, vt_ref, logf_ref, h_ref, r_ref, tmp_ref):
    n = pl.program_id(1)
    scale = (HEAD_DIM ** -0.5) * LOG2E
    is_q = n <= 2

    def matmul():
        r = lax.dot_general(h_ref[...], w_ref[...], NT_DIMS, preferred_element_type=F32)
        for hh in range(N_HEADS):
            r_ref[hh] = r[:, hh * HEAD_DIM:(hh + 1) * HEAD_DIM]

    def heads():
        return [r_ref[hh] for hh in range(N_HEADS)]

    def finish_dil(rotary, scaled):
        s = jnp.where(scaled, scale, 1.0)
        quarter = DIL_UNIT // DIL_RES_LO
        for hh in range(N_HEADS):
            for a in range(DIL_RES_LO):
                tmp_ref[a * quarter:(a + 1) * quarter, :] = r_ref[
                    hh, pl.ds(a, quarter, stride=DIL_RES_LO), :]
            for a in range(DIL_RES_LO):
                for b in range(DIL_RES_LO):
                    slot = DIL_RES_LO * a + b
                    rows = slice(slot * ROWS_PER_RES, (slot + 1) * ROWS_PER_RES)
                    t = tmp_ref[pl.ds(a * quarter + b, ROWS_PER_RES, stride=DIL_RES_LO), :]
                    cos = jnp.where(rotary, cos_ref[rows, :], 1.0) * s
                    sin = jnp.where(rotary, sin_ref[rows, :], 0.0) * s
                    dil_ref[hh, rows, :] = _rope(t, cos, sin)

    def finish_fox(scaled):
        s = jnp.where(scaled, scale, 1.0)
        for hh, t in enumerate(heads()):
            fox_ref[hh] = (t * s).astype(BF16)

    @pl.when(n == 0)
    def _():
        h = _rmsnorm(x_ref[...], gain_ref[...]).astype(BF16)
        h_ref[...] = h
        f = lax.dot_general(h, wf_ref[...], NT_DIMS, preferred_element_type=F32) + bf_ref[...]
        logf_ref[...] = jnp.minimum(f, 0.0) - jnp.log1p(jnp.exp(-jnp.abs(f)))
        matmul()

    @pl.when((n == 1) | (n == 3))
    def _():
        finish_dil(True, is_q)
        matmul()

    @pl.when((n == 2) | (n == 4))
    def _():
        finish_fox(is_q)
        matmul()

    @pl.when(n == N_QKV_TILES - 1)
    def _():
        finish_dil(False, False)
        matmul()
        for hh, t in enumerate(heads()):
            vt_ref[hh] = t.T.astype(BF16)


def _qkv_proj(x, gain, w_qkv, w_f, b_f, cos, sin_signed):
    s_tiles = SEQ // QKV_TM
    grid = (N_TOK // QKV_TM, N_QKV_TILES)

    def head_block(m, which):
        return (which, m // s_tiles, 0, m % s_tiles, 0)

    return pl.pallas_call(
        _qkv_kernel,
        grid=grid,
        in_specs=[
            pl.BlockSpec((QKV_TM, D_MODEL), lambda m, n: (m, 0)),
            pl.BlockSpec((1, D_MODEL), lambda m, n: (0, 0)),
            pl.BlockSpec((None, D_ATT, D_MODEL), lambda m, n: (n, 0, 0)),
            pl.BlockSpec((LANES, D_MODEL), lambda m, n: (0, 0)),
            pl.BlockSpec((1, LANES), lambda m, n: (0, 0)),
            pl.BlockSpec((QKV_TM, LANES), lambda m, n: (m % s_tiles, 0)),
            pl.BlockSpec((QKV_TM, LANES), lambda m, n: (m % s_tiles, 0)),
        ],
        out_specs=[
            pl.BlockSpec((None, None, N_HEADS, QKV_TM, HEAD_DIM),
                         lambda m, n: head_block(m, jnp.maximum(n - 1, 0) // 2)),
            pl.BlockSpec((None, None, N_HEADS, QKV_TM, HEAD_DIM),
                         lambda m, n: head_block(m, jnp.clip(n - 3, 0, 1))),
            pl.BlockSpec((None, N_HEADS, HEAD_DIM, QKV_TM),
                         lambda m, n: (m // s_tiles, 0, 0, m % s_tiles)),
            pl.BlockSpec((QKV_TM, LANES), lambda m, n: (m, 0)),
        ],
        out_shape=[
            jax.ShapeDtypeStruct((N_MIX_QKV, BATCH, N_HEADS, SEQ, HEAD_DIM), F32),
            jax.ShapeDtypeStruct((2, BATCH, N_HEADS, SEQ, HEAD_DIM), BF16),
            jax.ShapeDtypeStruct((BATCH, N_HEADS, HEAD_DIM, SEQ), BF16),
            jax.ShapeDtypeStruct((N_TOK, LANES), F32),
        ],
        scratch_shapes=[pltpu.VMEM((QKV_TM, D_MODEL), BF16),
                        pltpu.VMEM((N_HEADS, QKV_TM, HEAD_DIM), F32),
                        pltpu.VMEM((QKV_TM, HEAD_DIM), F32)],
        compiler_params=pltpu.CompilerParams(
            dimension_semantics=("parallel", "arbitrary"), vmem_limit_bytes=61 * MIB),
        name="qkv_proj",
    )(x, gain, w_qkv, w_f, b_f, cos, sin_signed)


CUM_TS = 1024
CUM_CHUNK = 128


def _cumsum_kernel(lf_ref, c_ref, carry_ref):
    @pl.when(pl.program_id(1) == 0)
    def _():
        carry_ref[...] = jnp.zeros_like(carry_ref)

    r = lax.broadcasted_iota(jnp.int32, (CUM_CHUNK, CUM_CHUNK), 0)
    c = lax.broadcasted_iota(jnp.int32, (CUM_CHUNK, CUM_CHUNK), 1)
    tri = (c <= r).astype(F32)
    carry = carry_ref[...]
    for i in range(CUM_TS // CUM_CHUNK):
        rows = slice(i * CUM_CHUNK, (i + 1) * CUM_CHUNK)
        cs = jnp.dot(tri, lf_ref[rows, :], preferred_element_type=F32,
                     precision=lax.Precision.HIGHEST) + carry
        c_ref[rows, :] = cs
        carry = cs[CUM_CHUNK - 1:CUM_CHUNK, :]
    carry_ref[...] = carry


def _cumsum(logf):
    return pl.pallas_call(
        _cumsum_kernel,
        grid=(BATCH, SEQ // CUM_TS),
        in_specs=[pl.BlockSpec((None, CUM_TS, LANES), lambda b, s: (b, s, 0))],
        out_specs=pl.BlockSpec((None, CUM_TS, LANES), lambda b, s: (b, s, 0)),
        out_shape=jax.ShapeDtypeStruct((BATCH, SEQ, LANES), F32),
        scratch_shapes=[pltpu.VMEM((1, LANES), F32)],
        compiler_params=pltpu.CompilerParams(dimension_semantics=("parallel", "arbitrary")),
        name="gate_cumsum",
    )(logf)


FOX_TQ = 2048
FOX_TK = 512
FOX_DIAG = FOX_TQ // FOX_TK
BF16_ROWS = 16
FOX_VT_ROWS = HEAD_DIM + BF16_ROWS
N_SPLIT = 3


def _gate_columns(c_tile, head, key_side):
    lane = lax.broadcasted_iota(jnp.int32, c_tile.shape, 1)
    c = jnp.sum(jnp.where(lane == head, c_tile, 0.0), axis=-1, keepdims=True) * LOG2E
    if key_side:
        c = -c
    base = N_SPLIT if key_side else 0
    cols = jnp.where((lane >= N_SPLIT - base) & (lane < 2 * N_SPLIT - base), 1.0, 0.0)
    rest = c
    for t in range(N_SPLIT):
        term = rest.astype(BF16).astype(F32)
        cols = jnp.where(lane == base + t, term, cols)
        rest = rest - term
    return cols.astype(BF16)


def _fox_kernel(q_ref, k_ref, vt_ref, c_ref, o_ref, kx_ref, st_ref, mx_ref, m_ref, acc_ref):
    head = pl.program_id(0) % N_HEADS
    qi = pl.program_id(1)
    ones_tile = (lax.broadcasted_iota(jnp.int32, (BF16_ROWS, FOX_TK), 0) == 0).astype(BF16)

    @pl.when(qi == 0)
    def _():
        def build(i, carry):
            rows = pl.ds(pl.multiple_of(i * FOX_TK, FOX_TK), FOX_TK)
            kx_ref[rows, :] = _gate_columns(c_ref[rows, :], head, True)
            return carry

        lax.fori_loop(0, SEQ // FOX_TK, build, 0)

    q_rows = pl.ds(pl.multiple_of(qi * FOX_TQ, FOX_TQ), FOX_TQ)
    q_aug = jnp.concatenate([q_ref[...], _gate_columns(c_ref[q_rows, :], head, False)], axis=-1)

    def key_rows(kb):
        return pl.ds(pl.multiple_of(kb * FOX_TK, FOX_TK), FOX_TK)

    def scores(kb, slot, q_lo=0):
        rows = key_rows(kb)
        k_aug = jnp.concatenate([k_ref[rows, :], kx_ref[rows, :]], axis=-1)
        st = lax.dot_general(k_aug, q_aug[q_lo:, :], (((1,), (1,)), ((), ())),
                             preferred_element_type=F32)
        st_ref[slot, :, q_lo:] = st
        mx_ref[slot, :, q_lo:] = jnp.max(st, axis=0, keepdims=True)

    def accumulate(kb, slot, q_lo=0, key_lo=None):
        st = st_ref[slot, :, q_lo:]
        if key_lo is None:
            st_max = mx_ref[slot, :, q_lo:]
        else:
            key = lax.broadcasted_iota(jnp.int32, st.shape, 0) + key_lo
            qry = lax.broadcasted_iota(jnp.int32, st.shape, 1) + q_lo
            st = jnp.where(key <= qry, st, NEG_BIG)
            st_max = jnp.max(st, axis=0, keepdims=True)
        m = m_ref[:, q_lo:]
        m_new = jnp.maximum(m, st_max)
        p = jnp.exp2(st - m_new).astype(BF16)
        vt = jnp.concatenate([vt_ref[:, key_rows(kb)], ones_tile], axis=0)
        pv = jnp.dot(vt, p, preferred_element_type=F32)
        acc_ref[:, q_lo:] = jnp.exp2(m - m_new) * acc_ref[:, q_lo:] + pv
        m_ref[:, q_lo:] = m_new

    m_ref[...] = jnp.full(m_ref.shape, NEG_BIG, F32)
    acc_ref[...] = jnp.zeros_like(acc_ref)
    scores(0, 0)

    def block(j, carry):
        for t in range(FOX_DIAG):
            tile = FOX_DIAG * j + t
            scores(tile + 1, (t + 1) % 2)
            accumulate(tile, t % 2)
        return carry

    first = qi * FOX_DIAG
    lax.fori_loop(0, qi, block, 0)
    for d in range(FOX_DIAG):
        if d + 1 < FOX_DIAG:
            scores(first + d + 1, (d + 1) % 2, q_lo=(d + 1) * FOX_TK)
        accumulate(first + d, d % 2, q_lo=d * FOX_TK, key_lo=d * FOX_TK)

    out_t = acc_ref[0:HEAD_DIM, :] / acc_ref[HEAD_DIM:HEAD_DIM + 1, :]
    o_ref[...] = out_t.T.astype(BF16)


def _fox_attn(qk, vt, c_gate):
    assert FOX_TQ % (2 * FOX_TK) == 0
    q_tiles = SEQ // FOX_TQ
    return pl.pallas_call(
        _fox_kernel,
        grid=(BATCH * N_HEADS, q_tiles),
        in_specs=[
            pl.BlockSpec((None, None, None, FOX_TQ, HEAD_DIM),
                         lambda bh, qi: (QI, bh // N_HEADS, bh % N_HEADS, qi, 0)),
            pl.BlockSpec((None, None, None, SEQ, HEAD_DIM),
                         lambda bh, qi: (KI, bh // N_HEADS, bh % N_HEADS, 0, 0)),
            pl.BlockSpec((None, None, HEAD_DIM, SEQ),
                         lambda bh, qi: (bh // N_HEADS, bh % N_HEADS, 0, 0)),
            pl.BlockSpec((None, SEQ, LANES), lambda bh, qi: (bh // N_HEADS, 0, 0)),
        ],
        out_specs=pl.BlockSpec((FOX_TQ, HEAD_DIM),
                               lambda bh, qi: ((bh // N_HEADS) * q_tiles + qi, bh % N_HEADS)),
        out_shape=jax.ShapeDtypeStruct((N_TOK, D_ATT), BF16),
        scratch_shapes=[
            pltpu.VMEM((SEQ, LANES), BF16),
            pltpu.VMEM((2, FOX_TK, FOX_TQ), F32),
            pltpu.VMEM((2, 1, FOX_TQ), F32),
            pltpu.VMEM((1, FOX_TQ), F32),
            pltpu.VMEM((FOX_VT_ROWS, FOX_TQ), F32),
        ],
        compiler_params=pltpu.CompilerParams(
            dimension_semantics=("parallel", "arbitrary"), vmem_limit_bytes=48 * MIB),
        name="fox_attn",
    )(qk, qk, vt, c_gate)


DIL_GROUP = 128
DIL_SUPER = DIL_GROUP * max(d for _, d in DIL_PATTERNS)
N_PAT = len(DIL_PATTERNS)
assert all(w == DIL_GROUP * d and DIL_SUPER % w == 0 for w, d in DIL_PATTERNS)


def _dil_group_order(d):
    cycle = DIL_RES // d
    run = DIL_GROUP // cycle
    i = np.arange(DIL_GROUP)
    return cycle * (i % run) + i // run


def _dil_band_bias():
    tables = []
    for _, d in DIL_PATTERNS:
        order = _dil_group_order(d)
        k = order[:, None]
        j = np.concatenate([order, DIL_GROUP + order])[None, :]
        tables.append(np.where((j >= k) & (j <= k + DIL_GROUP), 0.0, NEG_BIG))
    return jnp.asarray(np.stack(tables), F32)


def _dil_chunks(d, u, c):
    cycle = DIL_RES // d
    run = DIL_GROUP // cycle
    t_rel = u * DIL_GROUP * d
    out = []
    for b in range(cycle):
        for lo in range(0, run, ROWS_PER_RES):
            unit, off = divmod(t_rel + DIL_RES * lo, DIL_UNIT)
            out.append((unit * DIL_UNIT + ROWS_PER_RES * _res_slot(c + d * b) + off // DIL_RES,
                        min(run - lo, ROWS_PER_RES)))
    return out


def _dil_kernel(q_ref, k_ref, v_ref, band_ref, o_ref, num_ref, den_ref, mx_ref, nat_ref):
    sb = pl.program_id(1)
    base = sb * DIL_SUPER
    col = lax.broadcasted_iota(jnp.int32, (DIL_GROUP, 2 * DIL_GROUP), 1)
    ones = jnp.ones((2 * DIL_GROUP, HEAD_DIM), BF16)

    def gather(ref, pieces, dynamic):
        parts = []
        for start, n in pieces:
            if dynamic:
                row = base + start
                row = jnp.maximum(row, 0) if start < 0 else row
                parts.append(ref[pl.ds(pl.multiple_of(row, F32_ROWS), n), :])
            else:
                parts.append(ref[start:start + n, :])
        return jnp.concatenate(parts, axis=0)

    for g, (_, d) in enumerate(DIL_PATTERNS):
        band = band_ref[g]
        band_first = jnp.where((sb == 0) & (col < DIL_GROUP), NEG_BIG, band)
        for u in range(DIL_SUPER // (DIL_GROUP * d)):
            bias = band_first if u == 0 else band
            for c in range(d):
                own = _dil_chunks(d, u, c)
                window = _dil_chunks(d, u - 1, c) + own
                q = gather(q_ref, own, False).astype(BF16)
                k = gather(k_ref, window, True).astype(BF16)
                v = gather(v_ref, window, True).astype(BF16)
                s = lax.dot_general(q, k, NT_DIMS, preferred_element_type=F32) + bias
                m = jnp.max(s, axis=-1, keepdims=True)
                p = jnp.exp2(s - m).astype(BF16)
                r = jnp.dot(p, jnp.concatenate([v, ones], axis=-1), preferred_element_type=F32)
                m_rep = jnp.broadcast_to(m, (DIL_GROUP, HEAD_DIM))
                at = 0
                for start, n in own:
                    num_ref[g, start:start + n, :] = r[at:at + n, :HEAD_DIM]
                    den_ref[g, start:start + n, :] = r[at:at + n, HEAD_DIM:]
                    mx_ref[g, start:start + n, :] = m_rep[at:at + n, :]
                    at += n

    def combine(i, carry):
        r = pl.ds(pl.multiple_of(i * ROWS_PER_RES, ROWS_PER_RES), ROWS_PER_RES)
        mx = [mx_ref[g, r, :] for g in range(N_PAT)]
        top = functools.reduce(jnp.maximum, mx)
        wgt = [jnp.exp2(t - top) for t in mx]
        num = functools.reduce(jnp.add, [wgt[g] * num_ref[g, r, :] for g in range(N_PAT)])
        den = functools.reduce(jnp.add, [wgt[g] * den_ref[g, r, :] for g in range(N_PAT)])
        unit, res = i // DIL_RES, _res_slot(i % DIL_RES)
        nat_ref[pl.ds(unit * DIL_UNIT + res, ROWS_PER_RES, stride=DIL_RES), :] = num / den
        return carry

    lax.fori_loop(0, DIL_SUPER // ROWS_PER_RES, combine, 0)
    o_ref[...] = nat_ref[...].astype(BF16)


def _dil_attn(qkv, band):
    steps = SEQ // DIL_SUPER
    per_pattern = pltpu.VMEM((N_PAT, DIL_SUPER, HEAD_DIM), F32)
    return pl.pallas_call(
        _dil_kernel,
        grid=(BATCH * N_HEADS, steps),
        in_specs=[
            pl.BlockSpec((None, None, None, DIL_SUPER, HEAD_DIM),
                         lambda bh, sb: (QI, bh // N_HEADS, bh % N_HEADS, sb, 0)),
            pl.BlockSpec((None, None, None, SEQ, HEAD_DIM),
                         lambda bh, sb: (KI, bh // N_HEADS, bh % N_HEADS, 0, 0)),
            pl.BlockSpec((None, None, None, SEQ, HEAD_DIM),
                         lambda bh, sb: (VI, bh // N_HEADS, bh % N_HEADS, 0, 0)),
            pl.BlockSpec((N_PAT, DIL_GROUP, 2 * DIL_GROUP), lambda bh, sb: (0, 0, 0)),
        ],
        out_specs=pl.BlockSpec((DIL_SUPER, HEAD_DIM),
                               lambda bh, sb: ((bh // N_HEADS) * steps + sb, bh % N_HEADS)),
        out_shape=jax.ShapeDtypeStruct((N_TOK, D_ATT), BF16),
        scratch_shapes=[per_pattern, per_pattern, per_pattern,
                        pltpu.VMEM((DIL_SUPER, HEAD_DIM), F32)],
        compiler_params=pltpu.CompilerParams(
            dimension_semantics=("parallel", "arbitrary"), vmem_limit_bytes=48 * MIB),
        name="dil_attn",
    )(qkv, qkv, qkv, band)


MRG_TM = 1024
MRG_TJ = 256


MRG_STEPS = D_MODEL // MRG_TJ


def _merge_kernel(x_ref, gain_ref, yd_ref, yf_ref, wgd_ref, wgf_ref, bgd_ref, bgf_ref,
                  wpd_ref, wpf_ref, wo_ref, wo_last_ref, o_ref, h_ref, m_ref):
    j = pl.program_id(1)
    last = MRG_STEPS - 1

    def merge_tile():
        h = h_ref[...]
        gd = lax.dot_general(h, wgd_ref[...], NT_DIMS, preferred_element_type=F32) + bgd_ref[...]
        gf = lax.dot_general(h, wgf_ref[...], NT_DIMS, preferred_element_type=F32) + bgf_ref[...]
        pd = jnp.dot(yd_ref[...], wpd_ref[...], preferred_element_type=F32)
        pf = jnp.dot(yf_ref[...], wpf_ref[...], preferred_element_type=F32)
        m_ref[j % 2] = (jax.nn.sigmoid(gd) * pd + jax.nn.sigmoid(gf) * pf).astype(BF16)

    def project():
        o_ref[...] += jnp.dot(m_ref[(j + 1) % 2], wo_ref[...], preferred_element_type=F32)

    @pl.when(j == 0)
    def _():
        h_ref[...] = _rmsnorm(x_ref[...], gain_ref[...]).astype(BF16)
        o_ref[...] = jnp.zeros_like(o_ref)
        merge_tile()

    @pl.when((j > 0) & (j < last))
    def _():
        project()
        merge_tile()

    @pl.when(j == last)
    def _():
        project()
        merge_tile()
        o_ref[...] += jnp.dot(m_ref[last % 2], wo_last_ref[...], preferred_element_type=F32)
        o_ref[...] = x_ref[...] + o_ref[...]


def _merge(x, gain, yd, yf, w_gates, bgd, bgf, wpd, wpf, wo):
    grid = (N_TOK // MRG_TM, MRG_STEPS)
    last = MRG_STEPS - 1
    return pl.pallas_call(
        _merge_kernel,
        grid=grid,
        in_specs=[
            pl.BlockSpec((MRG_TM, D_MODEL), lambda i, j: (i, 0)),
            pl.BlockSpec((1, D_MODEL), lambda i, j: (0, 0)),
            pl.BlockSpec((MRG_TM, D_ATT), lambda i, j: (i, 0)),
            pl.BlockSpec((MRG_TM, D_ATT), lambda i, j: (i, 0)),
            pl.BlockSpec((None, MRG_TJ, D_MODEL), lambda i, j: (j, 0, 0)),
            pl.BlockSpec((None, MRG_TJ, D_MODEL), lambda i, j: (MRG_STEPS + j, 0, 0)),
            pl.BlockSpec((1, MRG_TJ), lambda i, j: (0, j)),
            pl.BlockSpec((1, MRG_TJ), lambda i, j: (0, j)),
            pl.BlockSpec((None, D_ATT, MRG_TJ), lambda i, j: (j, 0, 0)),
            pl.BlockSpec((None, D_ATT, MRG_TJ), lambda i, j: (j, 0, 0)),
            pl.BlockSpec((MRG_TJ, D_MODEL), lambda i, j: (jnp.maximum(j - 1, 0), 0)),
            pl.BlockSpec((MRG_TJ, D_MODEL), lambda i, j: (last, 0), pipeline_mode=pl.Buffered(1)),
        ],
        out_specs=pl.BlockSpec((MRG_TM, D_MODEL), lambda i, j: (i, 0)),
        out_shape=jax.ShapeDtypeStruct((N_TOK, D_MODEL), F32),
        scratch_shapes=[pltpu.VMEM((MRG_TM, D_MODEL), BF16),
                        pltpu.VMEM((2, MRG_TM, MRG_TJ), BF16)],
        compiler_params=pltpu.CompilerParams(
            dimension_semantics=("parallel", "arbitrary"), vmem_limit_bytes=61 * MIB),
        name="merge_proj",
    )(x, gain, yd, yf, w_gates, w_gates, bgd, bgf, wpd, wpf, wo, wo)


def _rope_tables():
    pos = np.arange(SEQ, dtype=np.float64)
    inv_freq = ROPE_THETA ** (-np.arange(0, ROPE_DIM, 2, dtype=np.float64) / ROPE_DIM)
    ang = pos[:, None] * inv_freq[None, :]
    cos, sin = np.cos(ang), np.sin(ang)
    gap = ROPE_PARTNER - ROPE_HALF
    tail = HEAD_DIM - ROPE_PARTNER - ROPE_HALF
    one, zero = (lambda n: np.ones((SEQ, n))), (lambda n: np.zeros((SEQ, n)))
    cos_full = np.concatenate([cos, one(gap), cos, one(tail)], axis=-1)
    sin_signed = np.concatenate([-sin, zero(gap), sin, zero(tail)], axis=-1)
    assert cos_full.shape == (SEQ, HEAD_DIM)

    def residue_major(t):
        t = t.reshape(SEQ // DIL_UNIT, ROWS_PER_RES, DIL_RES, HEAD_DIM).transpose(0, 2, 1, 3)
        t = t[:, [_res_slot(slot) for slot in range(DIL_RES)]]
        return jnp.asarray(t.reshape(SEQ, HEAD_DIM), F32)

    return residue_major(cos_full), residue_major(sin_signed)


PREP_ROWS = 512
C_QKV = 2 * N_MIX_QKV * D_ATT
C_GATES = C_QKV + N_HEADS
_PREP_PARAMS = pltpu.CompilerParams(dimension_semantics=("parallel",), vmem_limit_bytes=40 * MIB)


def _cast_kernel(w_ref, o_ref):
    o_ref[...] = w_ref[...].astype(BF16)


def _cast_rows(w):
    rows, n_cols = w.shape
    return pl.pallas_call(
        _cast_kernel,
        grid=(rows // PREP_ROWS,),
        in_specs=[pl.BlockSpec((PREP_ROWS, n_cols), lambda i: (i, 0))],
        out_specs=pl.BlockSpec((PREP_ROWS, n_cols), lambda i: (i, 0)),
        out_shape=jax.ShapeDtypeStruct(w.shape, BF16),
        compiler_params=_PREP_PARAMS,
        name="cast_rows",
    )(w)


def _cast_column_tiles(w, tile):
    rows, n_cols = w.shape
    return pl.pallas_call(
        _cast_kernel,
        grid=(n_cols // tile,),
        in_specs=[pl.BlockSpec((rows, tile), lambda j: (0, j))],
        out_specs=pl.BlockSpec((None, rows, tile), lambda j: (j, 0, 0)),
        out_shape=jax.ShapeDtypeStruct((n_cols // tile, rows, tile), BF16),
        compiler_params=_PREP_PARAMS,
        name="cast_column_tiles",
    )(w)


def _qkv_weight_kernel(w_ref, o_ref):
    j = pl.program_id(0)
    rotary = (j % 2 == 0) & (j < 2 * (N_MIX_QKV - 1))

    @pl.when(rotary)
    def _():
        for hh in range(N_HEADS):
            base = hh * HEAD_DIM
            for dst, src, n in ((0, 0, ROPE_HALF),
                                (ROPE_HALF, ROPE_PARTNER, ROPE_HALF),
                                (ROPE_DIM, ROPE_DIM, ROPE_PARTNER - ROPE_DIM),
                                (ROPE_PARTNER, ROPE_HALF, ROPE_HALF),
                                (ROPE_PARTNER + ROPE_HALF, ROPE_PARTNER + ROPE_HALF,
                                 HEAD_DIM - ROPE_PARTNER - ROPE_HALF)):
                o_ref[base + dst:base + dst + n, :] = w_ref[base + src:base + src + n, :].astype(BF16)

    @pl.when(jnp.logical_not(rotary))
    def _():
        o_ref[...] = w_ref[...].astype(BF16)


def _qkv_weights(w_t):
    return pl.pallas_call(
        _qkv_weight_kernel,
        grid=(N_QKV_TILES,),
        in_specs=[pl.BlockSpec((D_ATT, D_MODEL), lambda j: ((j % 2) * N_MIX_QKV + j // 2, 0))],
        out_specs=pl.BlockSpec((None, D_ATT, D_MODEL), lambda j: (j, 0, 0)),
        out_shape=jax.ShapeDtypeStruct((N_QKV_TILES, D_ATT, D_MODEL), BF16),
        compiler_params=_PREP_PARAMS,
        name="qkv_weights",
    )(w_t)


F32_ROWS = 8
GATE_ROW_OFF = C_GATES % MRG_TJ


def _gate_weight_kernel(a_ref, b_ref, f_ref, o_ref, wf_ref):
    o_ref[...] = jnp.concatenate([a_ref[GATE_ROW_OFF:, :], b_ref[...]], axis=0).astype(BF16)

    @pl.when(pl.program_id(0) == 0)
    def _():
        pad = jnp.zeros((LANES - N_HEADS, D_MODEL), F32)
        wf_ref[...] = jnp.concatenate([f_ref[...], pad], axis=0).astype(BF16)


def _gate_weights(w_t):
    assert GATE_ROW_OFF == N_HEADS == F32_ROWS and C_QKV % MRG_TJ == 0
    first_tile = C_QKV // MRG_TJ
    return pl.pallas_call(
        _gate_weight_kernel,
        grid=(2 * MRG_STEPS,),
        in_specs=[
            pl.BlockSpec((MRG_TJ, D_MODEL), lambda t: (first_tile + t, 0)),
            pl.BlockSpec((GATE_ROW_OFF, D_MODEL),
                         lambda t: ((C_QKV + MRG_TJ * (t + 1)) // GATE_ROW_OFF, 0)),
            pl.BlockSpec((N_HEADS, D_MODEL), lambda t: (C_QKV // N_HEADS, 0)),
        ],
        out_specs=[
            pl.BlockSpec((None, MRG_TJ, D_MODEL), lambda t: (t, 0, 0)),
            pl.BlockSpec((LANES, D_MODEL), lambda t: (0, 0)),
        ],
        out_shape=[
            jax.ShapeDtypeStruct((2 * MRG_STEPS, MRG_TJ, D_MODEL), BF16),
            jax.ShapeDtypeStruct((LANES, D_MODEL), BF16),
        ],
        compiler_params=pltpu.CompilerParams(dimension_semantics=("arbitrary",),
                                             vmem_limit_bytes=40 * MIB),
        name="gate_weights",
    )(w_t, w_t, w_t)


def kernel(x, ffn1_norm, ffn1_w_gate, ffn1_w_up, ffn1_w_down, mix_norm, w_in, b_forget, b_gate_dil, b_gate_fox, w_proj_dil, w_proj_fox, w_out, ffn2_norm, ffn2_w_gate, ffn2_w_up, ffn2_w_down, final_norm):
    assert x.shape == (BATCH, SEQ, D_MODEL) and x.dtype == F32
    assert w_in.shape == (DEPTH, D_MODEL, C_GATES + 2 * D_MODEL)
    cos_full, sin_signed = _rope_tables()
    dil_band = _dil_band_bias()
    fgain = final_norm.reshape(1, D_MODEL)

    xt = x.reshape(N_TOK, D_MODEL)
    for l in range(DEPTH):
        last = l == DEPTH - 1
        w_t = jnp.swapaxes(w_in[l], 0, 1)
        w_qkv = _qkv_weights(w_t)
        w_gates, w_f = _gate_weights(w_t)
        b_f = jnp.pad(b_forget[l], (0, LANES - N_HEADS)).reshape(1, LANES)

        xt = _ffn(xt, ffn1_norm[l].reshape(1, D_MODEL), _cast_column_tiles(ffn1_w_gate[l], FFN_TF),
                  _cast_column_tiles(ffn1_w_up[l], FFN_TF), _cast_rows(ffn1_w_down[l]), fgain, False)

        qkv_dil, qk_fox, vt_fox, logf = _qkv_proj(xt, mix_norm[l].reshape(1, D_MODEL), w_qkv, w_f,
                                                  b_f, cos_full, sin_signed)
        c_gate = _cumsum(logf.reshape(BATCH, SEQ, LANES))
        y_fox = _fox_attn(qk_fox, vt_fox, c_gate)
        y_dil = _dil_attn(qkv_dil, dil_band)

        xt = _merge(xt, mix_norm[l].reshape(1, D_MODEL), y_dil, y_fox, w_gates,
                    b_gate_dil[l].reshape(1, D_MODEL), b_gate_fox[l].reshape(1, D_MODEL),
                    _cast_column_tiles(w_proj_dil[l], MRG_TJ),
                    _cast_column_tiles(w_proj_fox[l], MRG_TJ), _cast_rows(w_out[l]))

        xt = _ffn(xt, ffn2_norm[l].reshape(1, D_MODEL), _cast_column_tiles(ffn2_w_gate[l], FFN_TF),
                  _cast_column_tiles(ffn2_w_up[l], FFN_TF), _cast_rows(ffn2_w_down[l]), fgain, last)
    return xt.reshape(BATCH, SEQ, D_MODEL)
```

```python
import functools

import numpy as np
import jax
import jax.numpy as jnp
from jax import lax
from jax.experimental import pallas as pl
from jax.experimental.pallas import tpu as pltpu

D_MODEL = 2048
BATCH = 2
SEQ = 8192
DEPTH = 1
HEAD_DIM = 128
N_HEADS = 8
D_ATT = N_HEADS * HEAD_DIM
DIL_PATTERNS = ((128, 1), (512, 4), (2048, 16))
MAX_WINDOW = 2048
ROPE_THETA = 500000.0
ROPE_DIM = HEAD_DIM // 4
D_FF = 5632
NORM_EPS = 1e-6
N_TOK = BATCH * SEQ
LANES = 128

NEG_BIG = -1e30
LOG2E = 1.4426950408889634
MIB = 1024 * 1024

F32 = jnp.float32
BF16 = jnp.bfloat16


def _rmsnorm(x, gain):
    ms = jnp.mean(x * x, axis=-1, keepdims=True)
    return x * lax.rsqrt(ms + NORM_EPS) * gain


FFN_TM = 1024
FFN_TF = 512
FFN_STEPS = D_FF // FFN_TF
SLABS_PER_TILE = 2
FFN_SLABS = SLABS_PER_TILE * FFN_STEPS
FFN_SLAB_ROWS = D_MODEL // SLABS_PER_TILE


def _ffn_kernel(x_ref, gain_ref, wg_ref, wu_ref, wd_ref, wd_last_ref, fgain_ref, o_ref, h_ref, a_ref,
                *, final_norm):
    j = pl.program_id(1)
    last = FFN_STEPS - 1

    def gate_up():
        h = h_ref[...]
        g = jnp.dot(h, wg_ref[...], preferred_element_type=F32)
        u = jnp.dot(h, wu_ref[...], preferred_element_type=F32)
        a_ref[j % 2] = ((g * jax.nn.sigmoid(g)) * u).astype(BF16)

    def down():
        o_ref[...] += jnp.dot(a_ref[(j + 1) % 2], wd_ref[...], preferred_element_type=F32)

    @pl.when(j == 0)
    def _():
        h_ref[...] = _rmsnorm(x_ref[...], gain_ref[...]).astype(BF16)
        o_ref[...] = jnp.zeros_like(o_ref)
        gate_up()

    @pl.when((j > 0) & (j < last))
    def _():
        down()
        gate_up()

    @pl.when(j == last)
    def _():
        down()
        gate_up()
        o_ref[...] += jnp.dot(a_ref[last % 2], wd_last_ref[...], preferred_element_type=F32)
        y = x_ref[...] + 0.5 * o_ref[...]
        if final_norm:
            y = _rmsnorm(y, fgain_ref[...])
        o_ref[...] = y


def _ffn(x, gain, wg, wu, wd, fgain, final_norm):
    grid = (N_TOK // FFN_TM, FFN_STEPS)
    last = FFN_STEPS - 1
    return pl.pallas_call(
        functools.partial(_ffn_kernel, final_norm=final_norm),
        grid=grid,
        in_specs=[
            pl.BlockSpec((FFN_TM, D_MODEL), lambda i, j: (i, 0)),
            pl.BlockSpec((1, D_MODEL), lambda i, j: (0, 0)),
            pl.BlockSpec((None, D_MODEL, FFN_TF), lambda i, j: (j, 0, 0)),
            pl.BlockSpec((None, D_MODEL, FFN_TF), lambda i, j: (j, 0, 0)),
            pl.BlockSpec((FFN_TF, D_MODEL), lambda i, j: (jnp.maximum(j - 1, 0), 0)),
            pl.BlockSpec((FFN_TF, D_MODEL), lambda i, j: (last, 0), pipeline_mode=pl.Buffered(1)),
            pl.BlockSpec((1, D_MODEL), lambda i, j: (0, 0)),
        ],
        out_specs=pl.BlockSpec((FFN_TM, D_MODEL), lambda i, j: (i, 0)),
        out_shape=jax.ShapeDtypeStruct((N_TOK, D_MODEL), F32),
        scratch_shapes=[pltpu.VMEM((FFN_TM, D_MODEL), BF16),
                        pltpu.VMEM((2, FFN_TM, FFN_TF), BF16)],
        compiler_params=pltpu.CompilerParams(
            dimension_semantics=("parallel", "arbitrary"), vmem_limit_bytes=61 * MIB),
        name="ffn",
    )(x, gain, wg, wu, wd, wd, fgain)


QKV_TM = 1024
N_MIX_QKV = 3
QI, KI, VI = range(N_MIX_QKV)
N_QKV_TILES = 2 * N_MIX_QKV


ROPE_HALF = ROPE_DIM // 2
ROPE_PARTNER = LANES // 2

DIL_RES = max(d for _, d in DIL_PATTERNS)
DIL_RES_LO = 4
assert DIL_RES == DIL_RES_LO * DIL_RES_LO
DIL_UNIT = QKV_TM
ROWS_PER_RES = DIL_UNIT // DIL_RES


def _res_slot(res):
    return DIL_RES_LO * (res % DIL_RES_LO) + res // DIL_RES_LO


def _rope(x, cos, sin_signed):
    return x * cos + pltpu.roll(x, ROPE_PARTNER, 1) * sin_signed


NT_DIMS = (((1,), (1,)), ((), ()))


def _qkv_kernel(x_ref, gain_ref, w_ref, wf_ref, bf_ref, cos_ref, sin_ref,
                dil_ref, fox_ref, vt_ref, logf_ref, h_ref, r_ref, tmp_ref):
    n = pl.program_id(1)
    scale = (HEAD_DIM ** -0.5) * LOG2E
    is_q = n <= 2

    def matmul():
        r = lax.dot_general(h_ref[...], w_ref[...], NT_DIMS, preferred_element_type=F32)
        for hh in range(N_HEADS):
            r_ref[hh] = r[:, hh * HEAD_DIM:(hh + 1) * HEAD_DIM]

    def heads():
        return [r_ref[hh] for hh in range(N_HEADS)]

    def finish_dil(rotary, scaled):
        s = jnp.where(scaled, scale, 1.0)
        quarter = DIL_UNIT // DIL_RES_LO
        for hh in range(N_HEADS):
            for a in range(DIL_RES_LO):
                tmp_ref[a * quarter:(a + 1) * quarter, :] = r_ref[
                    hh, pl.ds(a, quarter, stride=DIL_RES_LO), :]
            for a in range(DIL_RES_LO):
                for b in range(DIL_RES_LO):
                    slot = DIL_RES_LO * a + b
                    rows = slice(slot * ROWS_PER_RES, (slot + 1) * ROWS_PER_RES)
                    t = tmp_ref[pl.ds(a * quarter + b, ROWS_PER_RES, stride=DIL_RES_LO), :]
                    cos = jnp.where(rotary, cos_ref[rows, :], 1.0) * s
                    sin = jnp.where(rotary, sin_ref[rows, :], 0.0) * s
                    dil_ref[hh, rows, :] = _rope(t, cos, sin)

    def finish_fox(scaled):
        s = jnp.where(scaled, scale, 1.0)
        for hh, t in enumerate(heads()):
            fox_ref[hh] = (t * s).astype(BF16)

    @pl.when(n == 0)
    def _():
        h = _rmsnorm(x_ref[...], gain_ref[...]).astype(BF16)
        h_ref[...] = h
        f = lax.dot_general(h, wf_ref[...], NT_DIMS, preferred_element_type=F32) + bf_ref[...]
        logf_ref[...] = jnp.minimum(f, 0.0) - jnp.log1p(jnp.exp(-jnp.abs(f)))
        matmul()

    @pl.when((n == 1) | (n == 3))
    def _():
        finish_dil(True, is_q)
        matmul()

    @pl.when((n == 2) | (n == 4))
    def _():
        finish_fox(is_q)
        matmul()

    @pl.when(n == N_QKV_TILES - 1)
    def _():
        finish_dil(False, False)
        matmul()
        for hh, t in enumerate(heads()):
            vt_ref[hh] = t.T.astype(BF16)


def _qkv_proj(x, gain, w_qkv, w_f, b_f, cos, sin_signed):
    s_tiles = SEQ // QKV_TM
    grid = (N_TOK // QKV_TM, N_QKV_TILES)

    def head_block(m, which):
        return (which, m // s_tiles, 0, m % s_tiles, 0)

    return pl.pallas_call(
        _qkv_kernel,
        grid=grid,
        in_specs=[
            pl.BlockSpec((QKV_TM, D_MODEL), lambda m, n: (m, 0)),
            pl.BlockSpec((1, D_MODEL), lambda m, n: (0, 0)),
            pl.BlockSpec((None, D_ATT, D_MODEL), lambda m, n: (n, 0, 0)),
            pl.BlockSpec((LANES, D_MODEL), lambda m, n: (0, 0)),
            pl.BlockSpec((1, LANES), lambda m, n: (0, 0)),
            pl.BlockSpec((QKV_TM, LANES), lambda m, n: (m % s_tiles, 0)),
            pl.BlockSpec((QKV_TM, LANES), lambda m, n: (m % s_tiles, 0)),
        ],
        out_specs=[
            pl.BlockSpec((None, None, N_HEADS, QKV_TM, HEAD_DIM),
                         lambda m, n: head_block(m, jnp.maximum(n - 1, 0) // 2)),
            pl.BlockSpec((None, None, N_HEADS, QKV_TM, HEAD_DIM),
                         lambda m, n: head_block(m, jnp.clip(n - 3, 0, 1))),
            pl.BlockSpec((None, N_HEADS, HEAD_DIM, QKV_TM),
                         lambda m, n: (m // s_tiles, 0, 0, m % s_tiles)),
            pl.BlockSpec((QKV_TM, LANES), lambda m, n: (m, 0)),
        ],
        out_shape=[
            jax.ShapeDtypeStruct((N_MIX_QKV, BATCH, N_HEADS, SEQ, HEAD_DIM), F32),
            jax.ShapeDtypeStruct((2, BATCH, N_HEADS, SEQ, HEAD_DIM), BF16),
            jax.ShapeDtypeStruct((BATCH, N_HEADS, HEAD_DIM, SEQ), BF16),
            jax.ShapeDtypeStruct((N_TOK, LANES), F32),
        ],
        scratch_shapes=[pltpu.VMEM((QKV_TM, D_MODEL), BF16),
                        pltpu.VMEM((N_HEADS, QKV_TM, HEAD_DIM), F32),
                        pltpu.VMEM((QKV_TM, HEAD_DIM), F32)],
        compiler_params=pltpu.CompilerParams(
            dimension_semantics=("parallel", "arbitrary"), vmem_limit_bytes=61 * MIB),
        name="qkv_proj",
    )(x, gain, w_qkv, w_f, b_f, cos, sin_signed)


CUM_TS = 1024
CUM_CHUNK = 128


def _cumsum_kernel(lf_ref, c_ref, carry_ref):
    @pl.when(pl.program_id(1) == 0)
    def _():
        carry_ref[...] = jnp.zeros_like(carry_ref)

    r = lax.broadcasted_iota(jnp.int32, (CUM_CHUNK, CUM_CHUNK), 0)
    c = lax.broadcasted_iota(jnp.int32, (CUM_CHUNK, CUM_CHUNK), 1)
    tri = (c <= r).astype(F32)
    carry = carry_ref[...]
    for i in range(CUM_TS // CUM_CHUNK):
        rows = slice(i * CUM_CHUNK, (i + 1) * CUM_CHUNK)
        cs = jnp.dot(tri, lf_ref[rows, :], preferred_element_type=F32,
                     precision=lax.Precision.HIGHEST) + carry
        c_ref[rows, :] = cs
        carry = cs[CUM_CHUNK - 1:CUM_CHUNK, :]
    carry_ref[...] = carry


def _cumsum(logf):
    return pl.pallas_call(
        _cumsum_kernel,
        grid=(BATCH, SEQ // CUM_TS),
        in_specs=[pl.BlockSpec((None, CUM_TS, LANES), lambda b, s: (b, s, 0))],
        out_specs=pl.BlockSpec((None, CUM_TS, LANES), lambda b, s: (b, s, 0)),
        out_shape=jax.ShapeDtypeStruct((BATCH, SEQ, LANES), F32),
        scratch_shapes=[pltpu.VMEM((1, LANES), F32)],
        compiler_params=pltpu.CompilerParams(dimension_semantics=("parallel", "arbitrary")),
        name="gate_cumsum",
    )(logf)


FOX_TQ = 2048
FOX_TK = 512
FOX_DIAG = FOX_TQ // FOX_TK
BF16_ROWS = 16
FOX_VT_ROWS = HEAD_DIM + BF16_ROWS
N_SPLIT = 3


def _gate_columns(c_tile, head, key_side):
    lane = lax.broadcasted_iota(jnp.int32, c_tile.shape, 1)
    c = jnp.sum(jnp.where(lane == head, c_tile, 0.0), axis=-1, keepdims=True) * LOG2E
    if key_side:
        c = -c
    base = N_SPLIT if key_side else 0
    cols = jnp.where((lane >= N_SPLIT - base) & (lane < 2 * N_SPLIT - base), 1.0, 0.0)
    rest = c
    for t in range(N_SPLIT):
        term = rest.astype(BF16).astype(F32)
        cols = jnp.where(lane == base + t, term, cols)
        rest = rest - term
    return cols.astype(BF16)


def _fox_kernel(q_ref, k_ref, vt_ref, c_ref, wg_ref, wu_ref, o_ref, wg_out, wu_out,
                kx_ref, st_ref, mx_ref, m_ref, acc_ref):
    head = pl.program_id(0) % N_HEADS
    qi = pl.program_id(1)

    step = pl.program_id(0) * pl.num_programs(1) + qi

    @pl.when(step < FFN_SLABS)
    def _():
        wg_out[...] = wg_ref[...].astype(BF16)

    @pl.when((step >= FFN_SLABS) & (step < 2 * FFN_SLABS))
    def _():
        wu_out[...] = wu_ref[...].astype(BF16)

    ones_tile = (lax.broadcasted_iota(jnp.int32, (BF16_ROWS, FOX_TK), 0) == 0).astype(BF16)

    @pl.when(qi == 0)
    def _():
        def build(i, carry):
            rows = pl.ds(pl.multiple_of(i * FOX_TK, FOX_TK), FOX_TK)
            kx_ref[rows, :] = _gate_columns(c_ref[rows, :], head, True)
            return carry

        lax.fori_loop(0, SEQ // FOX_TK, build, 0)

    q_rows = pl.ds(pl.multiple_of(qi * FOX_TQ, FOX_TQ), FOX_TQ)
    q_aug = jnp.concatenate([q_ref[...], _gate_columns(c_ref[q_rows, :], head, False)], axis=-1)

    def key_rows(kb):
        return pl.ds(pl.multiple_of(kb * FOX_TK, FOX_TK), FOX_TK)

    def scores(kb, slot, q_lo=0):
        rows = key_rows(kb)
        k_aug = jnp.concatenate([k_ref[rows, :], kx_ref[rows, :]], axis=-1)
        st = lax.dot_general(k_aug, q_aug[q_lo:, :], (((1,), (1,)), ((), ())),
                             preferred_element_type=F32)
        st_ref[slot, :, q_lo:] = st
        mx_ref[slot, :, q_lo:] = jnp.max(st, axis=0, keepdims=True)

    def accumulate(kb, slot, q_lo=0, key_lo=None):
        st = st_ref[slot, :, q_lo:]
        if key_lo is None:
            st_max = mx_ref[slot, :, q_lo:]
        else:
            key = lax.broadcasted_iota(jnp.int32, st.shape, 0) + key_lo
            qry = lax.broadcasted_iota(jnp.int32, st.shape, 1) + q_lo
            st = jnp.where(key <= qry, st, NEG_BIG)
            st_max = jnp.max(st, axis=0, keepdims=True)
        m = m_ref[:, q_lo:]
        m_new = jnp.maximum(m, st_max)
        p = jnp.exp2(st - m_new).astype(BF16)
        vt = jnp.concatenate([vt_ref[:, key_rows(kb)], ones_tile], axis=0)
        pv = jnp.dot(vt, p, preferred_element_type=F32)
        acc_ref[:, q_lo:] = jnp.exp2(m - m_new) * acc_ref[:, q_lo:] + pv
        m_ref[:, q_lo:] = m_new

    m_ref[...] = jnp.full(m_ref.shape, NEG_BIG, F32)
    acc_ref[...] = jnp.zeros_like(acc_ref)
    scores(0, 0)

    def block(j, carry):
        for t in range(FOX_DIAG):
            tile = FOX_DIAG * j + t
            scores(tile + 1, (t + 1) % 2)
            accumulate(tile, t % 2)
        return carry

    first = qi * FOX_DIAG
    lax.fori_loop(0, qi, block, 0)
    for d in range(FOX_DIAG):
        if d + 1 < FOX_DIAG:
            scores(first + d + 1, (d + 1) % 2, q_lo=(d + 1) * FOX_TK)
        accumulate(first + d, d % 2, q_lo=d * FOX_TK, key_lo=d * FOX_TK)

    out_t = acc_ref[0:HEAD_DIM, :] / acc_ref[HEAD_DIM:HEAD_DIM + 1, :]
    o_ref[...] = out_t.T.astype(BF16)


def _fox_attn(qk, vt, c_gate, w_gate, w_up):
    assert FOX_TQ % (2 * FOX_TK) == 0
    q_tiles = SEQ // FOX_TQ
    assert BATCH * N_HEADS * q_tiles >= 2 * FFN_SLABS

    def slab(bh, qi, first):
        return jnp.clip(bh * q_tiles + qi - first, 0, FFN_SLABS - 1)

    def slab_in(first):
        return pl.BlockSpec((FFN_SLAB_ROWS, FFN_TF), lambda bh, qi: (
            slab(bh, qi, first) % SLABS_PER_TILE, slab(bh, qi, first) // SLABS_PER_TILE))

    def slab_out(first):
        return pl.BlockSpec((None, FFN_SLAB_ROWS, FFN_TF), lambda bh, qi: (
            slab(bh, qi, first) // SLABS_PER_TILE, slab(bh, qi, first) % SLABS_PER_TILE, 0))

    tiles = jax.ShapeDtypeStruct((FFN_STEPS, D_MODEL, FFN_TF), BF16)
    return pl.pallas_call(
        _fox_kernel,
        grid=(BATCH * N_HEADS, q_tiles),
        in_specs=[
            pl.BlockSpec((None, None, None, FOX_TQ, HEAD_DIM),
                         lambda bh, qi: (QI, bh // N_HEADS, bh % N_HEADS, qi, 0)),
            pl.BlockSpec((None, None, None, SEQ, HEAD_DIM),
                         lambda bh, qi: (KI, bh // N_HEADS, bh % N_HEADS, 0, 0)),
            pl.BlockSpec((None, None, HEAD_DIM, SEQ),
                         lambda bh, qi: (bh // N_HEADS, bh % N_HEADS, 0, 0)),
            pl.BlockSpec((None, SEQ, LANES), lambda bh, qi: (bh // N_HEADS, 0, 0)),
            slab_in(0),
            slab_in(FFN_SLABS),
        ],
        out_specs=[
            pl.BlockSpec((FOX_TQ, HEAD_DIM),
                         lambda bh, qi: ((bh // N_HEADS) * q_tiles + qi, bh % N_HEADS)),
            slab_out(0),
            slab_out(FFN_SLABS),
        ],
        out_shape=[jax.ShapeDtypeStruct((N_TOK, D_ATT), BF16), tiles, tiles],
        scratch_shapes=[
            pltpu.VMEM((SEQ, LANES), BF16),
            pltpu.VMEM((2, FOX_TK, FOX_TQ), F32),
            pltpu.VMEM((2, 1, FOX_TQ), F32),
            pltpu.VMEM((1, FOX_TQ), F32),
            pltpu.VMEM((FOX_VT_ROWS, FOX_TQ), F32),
        ],
        compiler_params=pltpu.CompilerParams(
            dimension_semantics=("arbitrary", "arbitrary"), vmem_limit_bytes=52 * MIB),
        name="fox_attn",
    )(qk, qk, vt, c_gate, w_gate, w_up)


DIL_GROUP = 128
DIL_SUPER = DIL_GROUP * max(d for _, d in DIL_PATTERNS)
N_PAT = len(DIL_PATTERNS)
assert all(w == DIL_GROUP * d and DIL_SUPER % w == 0 for w, d in DIL_PATTERNS)


def _dil_group_order(d):
    cycle = DIL_RES // d
    run = DIL_GROUP // cycle
    i = np.arange(DIL_GROUP)
    return cycle * (i % run) + i // run


def _dil_band_bias():
    tables = []
    for _, d in DIL_PATTERNS:
        order = _dil_group_order(d)
        k = order[:, None]
        j = np.concatenate([order, DIL_GROUP + order])[None, :]
        tables.append(np.where((j >= k) & (j <= k + DIL_GROUP), 0.0, NEG_BIG))
    return jnp.asarray(np.stack(tables), F32)


def _dil_chunks(d, u, c):
    cycle = DIL_RES // d
    run = DIL_GROUP // cycle
    t_rel = u * DIL_GROUP * d
    out = []
    for b in range(cycle):
        for lo in range(0, run, ROWS_PER_RES):
            unit, off = divmod(t_rel + DIL_RES * lo, DIL_UNIT)
            out.append((unit * DIL_UNIT + ROWS_PER_RES * _res_slot(c + d * b) + off // DIL_RES,
                        min(run - lo, ROWS_PER_RES)))
    return out


def _dil_kernel(q_ref, k_ref, v_ref, band_ref, wd_ref, o_ref, wd_out,
                num_ref, den_ref, mx_ref, nat_ref):
    sb = pl.program_id(1)

    @pl.when(pl.program_id(0) * pl.num_programs(1) + sb < FFN_SLABS)
    def _():
        wd_out[...] = wd_ref[...].astype(BF16)

    base = sb * DIL_SUPER
    col = lax.broadcasted_iota(jnp.int32, (DIL_GROUP, 2 * DIL_GROUP), 1)
    ones = jnp.ones((2 * DIL_GROUP, HEAD_DIM), BF16)

    def gather(ref, pieces, dynamic):
        parts = []
        for start, n in pieces:
            if dynamic:
                row = base + start
                row = jnp.maximum(row, 0) if start < 0 else row
                parts.append(ref[pl.ds(pl.multiple_of(row, F32_ROWS), n), :])
            else:
                parts.append(ref[start:start + n, :])
        return jnp.concatenate(parts, axis=0)

    for g, (_, d) in enumerate(DIL_PATTERNS):
        band = band_ref[g]
        band_first = jnp.where((sb == 0) & (col < DIL_GROUP), NEG_BIG, band)
        for u in range(DIL_SUPER // (DIL_GROUP * d)):
            bias = band_first if u == 0 else band
            for c in range(d):
                own = _dil_chunks(d, u, c)
                window = _dil_chunks(d, u - 1, c) + own
                q = gather(q_ref, own, False).astype(BF16)
                k = gather(k_ref, window, True).astype(BF16)
                v = gather(v_ref, window, True).astype(BF16)
                s = lax.dot_general(q, k, NT_DIMS, preferred_element_type=F32) + bias
                m = jnp.max(s, axis=-1, keepdims=True)
                p = jnp.exp2(s - m).astype(BF16)
                r = jnp.dot(p, jnp.concatenate([v, ones], axis=-1), preferred_element_type=F32)
                m_rep = jnp.broadcast_to(m, (DIL_GROUP, HEAD_DIM))
                at = 0
                for start, n in own:
                    num_ref[g, start:start + n, :] = r[at:at + n, :HEAD_DIM]
                    den_ref[g, start:start + n, :] = r[at:at + n, HEAD_DIM:]
                    mx_ref[g, start:start + n, :] = m_rep[at:at + n, :]
                    at += n

    def combine(i, carry):
        r = pl.ds(pl.multiple_of(i * ROWS_PER_RES, ROWS_PER_RES), ROWS_PER_RES)
        mx = [mx_ref[g, r, :] for g in range(N_PAT)]
        top = functools.reduce(jnp.maximum, mx)
        wgt = [jnp.exp2(t - top) for t in mx]
        num = functools.reduce(jnp.add, [wgt[g] * num_ref[g, r, :] for g in range(N_PAT)])
        den = functools.reduce(jnp.add, [wgt[g] * den_ref[g, r, :] for g in range(N_PAT)])
        unit, res = i // DIL_RES, _res_slot(i % DIL_RES)
        nat_ref[pl.ds(unit * DIL_UNIT + res, ROWS_PER_RES, stride=DIL_RES), :] = num / den
        return carry

    lax.fori_loop(0, DIL_SUPER // ROWS_PER_RES, combine, 0)
    o_ref[...] = nat_ref[...].astype(BF16)


def _dil_attn(qkv, band, w_down):
    steps = SEQ // DIL_SUPER
    assert BATCH * N_HEADS * steps >= FFN_SLABS
    down_rows = D_FF // FFN_SLABS
    down_slab = pl.BlockSpec((down_rows, D_MODEL),
                             lambda bh, sb: (jnp.minimum(bh * steps + sb, FFN_SLABS - 1), 0))
    per_pattern = pltpu.VMEM((N_PAT, DIL_SUPER, HEAD_DIM), F32)
    return pl.pallas_call(
        _dil_kernel,
        grid=(BATCH * N_HEADS, steps),
        in_specs=[
            pl.BlockSpec((None, None, None, DIL_SUPER, HEAD_DIM),
                         lambda bh, sb: (QI, bh // N_HEADS, bh % N_HEADS, sb, 0)),
            pl.BlockSpec((None, None, None, SEQ, HEAD_DIM),
                         lambda bh, sb: (KI, bh // N_HEADS, bh % N_HEADS, 0, 0)),
            pl.BlockSpec((None, None, None, SEQ, HEAD_DIM),
                         lambda bh, sb: (VI, bh // N_HEADS, bh % N_HEADS, 0, 0)),
            pl.BlockSpec((N_PAT, DIL_GROUP, 2 * DIL_GROUP), lambda bh, sb: (0, 0, 0)),
            down_slab,
        ],
        out_specs=[
            pl.BlockSpec((DIL_SUPER, HEAD_DIM),
                         lambda bh, sb: ((bh // N_HEADS) * steps + sb, bh % N_HEADS)),
            down_slab,
        ],
        out_shape=[jax.ShapeDtypeStruct((N_TOK, D_ATT), BF16),
                   jax.ShapeDtypeStruct((D_FF, D_MODEL), BF16)],
        scratch_shapes=[per_pattern, per_pattern, per_pattern,
                        pltpu.VMEM((DIL_SUPER, HEAD_DIM), F32)],
        compiler_params=pltpu.CompilerParams(
            dimension_semantics=("arbitrary", "arbitrary"), vmem_limit_bytes=48 * MIB),
        name="dil_attn",
    )(qkv, qkv, qkv, band, w_down)


MRG_TM = 1024
MRG_TJ = 256


MRG_STEPS = D_MODEL // MRG_TJ


def _merge_kernel(x_ref, gain_ref, yd_ref, yf_ref, wgd_ref, wgf_ref, bgd_ref, bgf_ref,
                  wpd_ref, wpf_ref, wo_ref, wo_last_ref, o_ref, h_ref, m_ref):
    j = pl.program_id(1)
    last = MRG_STEPS - 1

    def merge_tile():
        h = h_ref[...]
        gd = lax.dot_general(h, wgd_ref[...], NT_DIMS, preferred_element_type=F32) + bgd_ref[...]
        gf = lax.dot_general(h, wgf_ref[...], NT_DIMS, preferred_element_type=F32) + bgf_ref[...]
        pd = jnp.dot(yd_ref[...], wpd_ref[...], preferred_element_type=F32)
        pf = jnp.dot(yf_ref[...], wpf_ref[...], preferred_element_type=F32)
        m_ref[j % 2] = (jax.nn.sigmoid(gd) * pd + jax.nn.sigmoid(gf) * pf).astype(BF16)

    def project():
        o_ref[...] += jnp.dot(m_ref[(j + 1) % 2], wo_ref[...], preferred_element_type=F32)

    @pl.when(j == 0)
    def _():
        h_ref[...] = _rmsnorm(x_ref[...], gain_ref[...]).astype(BF16)
        o_ref[...] = jnp.zeros_like(o_ref)
        merge_tile()

    @pl.when((j > 0) & (j < last))
    def _():
        project()
        merge_tile()

    @pl.when(j == last)
    def _():
        project()
        merge_tile()
        o_ref[...] += jnp.dot(m_ref[last % 2], wo_last_ref[...], preferred_element_type=F32)
        o_ref[...] = x_ref[...] + o_ref[...]


def _merge(x, gain, yd, yf, w_gates, bgd, bgf, wpd, wpf, wo):
    grid = (N_TOK // MRG_TM, MRG_STEPS)
    last = MRG_STEPS - 1
    return pl.pallas_call(
        _merge_kernel,
        grid=grid,
        in_specs=[
            pl.BlockSpec((MRG_TM, D_MODEL), lambda i, j: (i, 0)),
            pl.BlockSpec((1, D_MODEL), lambda i, j: (0, 0)),
            pl.BlockSpec((MRG_TM, D_ATT), lambda i, j: (i, 0)),
            pl.BlockSpec((MRG_TM, D_ATT), lambda i, j: (i, 0)),
            pl.BlockSpec((None, MRG_TJ, D_MODEL), lambda i, j: (j, 0, 0)),
            pl.BlockSpec((None, MRG_TJ, D_MODEL), lambda i, j: (MRG_STEPS + j, 0, 0)),
            pl.BlockSpec((1, MRG_TJ), lambda i, j: (0, j)),
            pl.BlockSpec((1, MRG_TJ), lambda i, j: (0, j)),
            pl.BlockSpec((None, D_ATT, MRG_TJ), lambda i, j: (j, 0, 0)),
            pl.BlockSpec((None, D_ATT, MRG_TJ), lambda i, j: (j, 0, 0)),
            pl.BlockSpec((MRG_TJ, D_MODEL), lambda i, j: (jnp.maximum(j - 1, 0), 0)),
            pl.BlockSpec((MRG_TJ, D_MODEL), lambda i, j: (last, 0), pipeline_mode=pl.Buffered(1)),
        ],
        out_specs=pl.BlockSpec((MRG_TM, D_MODEL), lambda i, j: (i, 0)),
        out_shape=jax.ShapeDtypeStruct((N_TOK, D_MODEL), F32),
        scratch_shapes=[pltpu.VMEM((MRG_TM, D_MODEL), BF16),
                        pltpu.VMEM((2, MRG_TM, MRG_TJ), BF16)],
        compiler_params=pltpu.CompilerParams(
            dimension_semantics=("parallel", "arbitrary"), vmem_limit_bytes=61 * MIB),
        name="merge_proj",
    )(x, gain, yd, yf, w_gates, w_gates, bgd, bgf, wpd, wpf, wo, wo)


def _rope_tables():
    pos = np.arange(SEQ, dtype=np.float64)
    inv_freq = ROPE_THETA ** (-np.arange(0, ROPE_DIM, 2, dtype=np.float64) / ROPE_DIM)
    ang = pos[:, None] * inv_freq[None, :]
    cos, sin = np.cos(ang), np.sin(ang)
    gap = ROPE_PARTNER - ROPE_HALF
    tail = HEAD_DIM - ROPE_PARTNER - ROPE_HALF
    one, zero = (lambda n: np.ones((SEQ, n))), (lambda n: np.zeros((SEQ, n)))
    cos_full = np.concatenate([cos, one(gap), cos, one(tail)], axis=-1)
    sin_signed = np.concatenate([-sin, zero(gap), sin, zero(tail)], axis=-1)
    assert cos_full.shape == (SEQ, HEAD_DIM)

    def residue_major(t):
        t = t.reshape(SEQ // DIL_UNIT, ROWS_PER_RES, DIL_RES, HEAD_DIM).transpose(0, 2, 1, 3)
        t = t[:, [_res_slot(slot) for slot in range(DIL_RES)]]
        return jnp.asarray(t.reshape(SEQ, HEAD_DIM), F32)

    return residue_major(cos_full), residue_major(sin_signed)


PREP_ROWS = 512
C_QKV = 2 * N_MIX_QKV * D_ATT
C_GATES = C_QKV + N_HEADS
_PREP_PARAMS = pltpu.CompilerParams(dimension_semantics=("parallel",), vmem_limit_bytes=40 * MIB)


def _cast_kernel(w_ref, o_ref):
    o_ref[...] = w_ref[...].astype(BF16)


def _cast_rows(w):
    rows, n_cols = w.shape
    return pl.pallas_call(
        _cast_kernel,
        grid=(rows // PREP_ROWS,),
        in_specs=[pl.BlockSpec((PREP_ROWS, n_cols), lambda i: (i, 0))],
        out_specs=pl.BlockSpec((PREP_ROWS, n_cols), lambda i: (i, 0)),
        out_shape=jax.ShapeDtypeStruct(w.shape, BF16),
        compiler_params=_PREP_PARAMS,
        name="cast_rows",
    )(w)


def _cast_column_tiles(w, tile):
    rows, n_cols = w.shape
    return pl.pallas_call(
        _cast_kernel,
        grid=(n_cols // tile,),
        in_specs=[pl.BlockSpec((rows, tile), lambda j: (0, j))],
        out_specs=pl.BlockSpec((None, rows, tile), lambda j: (j, 0, 0)),
        out_shape=jax.ShapeDtypeStruct((n_cols // tile, rows, tile), BF16),
        compiler_params=_PREP_PARAMS,
        name="cast_column_tiles",
    )(w)


def _qkv_weight_kernel(w_ref, o_ref):
    j = pl.program_id(0)
    rotary = (j % 2 == 0) & (j < 2 * (N_MIX_QKV - 1))

    @pl.when(rotary)
    def _():
        for hh in range(N_HEADS):
            base = hh * HEAD_DIM
            for dst, src, n in ((0, 0, ROPE_HALF),
                                (ROPE_HALF, ROPE_PARTNER, ROPE_HALF),
                                (ROPE_DIM, ROPE_DIM, ROPE_PARTNER - ROPE_DIM),
                                (ROPE_PARTNER, ROPE_HALF, ROPE_HALF),
                                (ROPE_PARTNER + ROPE_HALF, ROPE_PARTNER + ROPE_HALF,
                                 HEAD_DIM - ROPE_PARTNER - ROPE_HALF)):
                o_ref[base + dst:base + dst + n, :] = w_ref[base + src:base + src + n, :].astype(BF16)

    @pl.when(jnp.logical_not(rotary))
    def _():
        o_ref[...] = w_ref[...].astype(BF16)


def _qkv_weights(w_t):
    return pl.pallas_call(
        _qkv_weight_kernel,
        grid=(N_QKV_TILES,),
        in_specs=[pl.BlockSpec((D_ATT, D_MODEL), lambda j: ((j % 2) * N_MIX_QKV + j // 2, 0))],
        out_specs=pl.BlockSpec((None, D_ATT, D_MODEL), lambda j: (j, 0, 0)),
        out_shape=jax.ShapeDtypeStruct((N_QKV_TILES, D_ATT, D_MODEL), BF16),
        compiler_params=_PREP_PARAMS,
        name="qkv_weights",
    )(w_t)


F32_ROWS = 8
GATE_ROW_OFF = C_GATES % MRG_TJ


def _gate_weight_kernel(a_ref, b_ref, f_ref, o_ref, wf_ref):
    o_ref[...] = jnp.concatenate([a_ref[GATE_ROW_OFF:, :], b_ref[...]], axis=0).astype(BF16)

    @pl.when(pl.program_id(0) == 0)
    def _():
        pad = jnp.zeros((LANES - N_HEADS, D_MODEL), F32)
        wf_ref[...] = jnp.concatenate([f_ref[...], pad], axis=0).astype(BF16)


def _gate_weights(w_t):
    assert GATE_ROW_OFF == N_HEADS == F32_ROWS and C_QKV % MRG_TJ == 0
    first_tile = C_QKV // MRG_TJ
    return pl.pallas_call(
        _gate_weight_kernel,
        grid=(2 * MRG_STEPS,),
        in_specs=[
            pl.BlockSpec((MRG_TJ, D_MODEL), lambda t: (first_tile + t, 0)),
            pl.BlockSpec((GATE_ROW_OFF, D_MODEL),
                         lambda t: ((C_QKV + MRG_TJ * (t + 1)) // GATE_ROW_OFF, 0)),
            pl.BlockSpec((N_HEADS, D_MODEL), lambda t: (C_QKV // N_HEADS, 0)),
        ],
        out_specs=[
            pl.BlockSpec((None, MRG_TJ, D_MODEL), lambda t: (t, 0, 0)),
            pl.BlockSpec((LANES, D_MODEL), lambda t: (0, 0)),
        ],
        out_shape=[
            jax.ShapeDtypeStruct((2 * MRG_STEPS, MRG_TJ, D_MODEL), BF16),
            jax.ShapeDtypeStruct((LANES, D_MODEL), BF16),
        ],
        compiler_params=pltpu.CompilerParams(dimension_semantics=("arbitrary",),
                                             vmem_limit_bytes=40 * MIB),
        name="gate_weights",
    )(w_t, w_t, w_t)


def kernel(x, ffn1_norm, ffn1_w_gate, ffn1_w_up, ffn1_w_down, mix_norm, w_in, b_forget, b_gate_dil, b_gate_fox, w_proj_dil, w_proj_fox, w_out, ffn2_norm, ffn2_w_gate, ffn2_w_up, ffn2_w_down, final_norm):
    assert x.shape == (BATCH, SEQ, D_MODEL) and x.dtype == F32
    assert w_in.shape == (DEPTH, D_MODEL, C_GATES + 2 * D_MODEL)
    cos_full, sin_signed = _rope_tables()
    dil_band = _dil_band_bias()
    fgain = final_norm.reshape(1, D_MODEL)

    xt = x.reshape(N_TOK, D_MODEL)
    for l in range(DEPTH):
        last = l == DEPTH - 1
        w_t = jnp.swapaxes(w_in[l], 0, 1)
        w_qkv = _qkv_weights(w_t)
        w_gates, w_f = _gate_weights(w_t)
        b_f = jnp.pad(b_forget[l], (0, LANES - N_HEADS)).reshape(1, LANES)

        xt = _ffn(xt, ffn1_norm[l].reshape(1, D_MODEL), _cast_column_tiles(ffn1_w_gate[l], FFN_TF),
                  _cast_column_tiles(ffn1_w_up[l], FFN_TF), _cast_rows(ffn1_w_down[l]), fgain, False)

        qkv_dil, qk_fox, vt_fox, logf = _qkv_proj(xt, mix_norm[l].reshape(1, D_MODEL), w_qkv, w_f,
                                                  b_f, cos_full, sin_signed)
        c_gate = _cumsum(logf.reshape(BATCH, SEQ, LANES))
        y_fox, w_gate2, w_up2 = _fox_attn(qk_fox, vt_fox, c_gate, ffn2_w_gate[l], ffn2_w_up[l])
        y_dil, w_down2 = _dil_attn(qkv_dil, dil_band, ffn2_w_down[l])

        xt = _merge(xt, mix_norm[l].reshape(1, D_MODEL), y_dil, y_fox, w_gates,
                    b_gate_dil[l].reshape(1, D_MODEL), b_gate_fox[l].reshape(1, D_MODEL),
                    _cast_column_tiles(w_proj_dil[l], MRG_TJ),
                    _cast_column_tiles(w_proj_fox[l], MRG_TJ), _cast_rows(w_out[l]))

        xt = _ffn(xt, ffn2_norm[l].reshape(1, D_MODEL), w_gate2, w_up2, w_down2, fgain, last)
    return xt.reshape(BATCH, SEQ, D_MODEL)
```

```python
import functools

import numpy as np
import jax
import jax.numpy as jnp
from jax import lax
from jax.experimental import pallas as pl
from jax.experimental.pallas import tpu as pltpu

D_MODEL = 2048
BATCH = 2
SEQ = 8192
DEPTH = 1
HEAD_DIM = 128
N_HEADS = 8
D_ATT = N_HEADS * HEAD_DIM
DIL_PATTERNS = ((128, 1), (512, 4), (2048, 16))
MAX_WINDOW = 2048
ROPE_THETA = 500000.0
ROPE_DIM = HEAD_DIM // 4
D_FF = 5632
NORM_EPS = 1e-6
N_TOK = BATCH * SEQ
LANES = 128

NEG_BIG = -1e30
LOG2E = 1.4426950408889634
MIB = 1024 * 1024

V7X_VMEM_BYTES = 64 * MIB
VMEM_ROW_TILE_KERNELS = V7X_VMEM_BYTES - 3 * MIB
VMEM_ATTENTION = 52 * MIB
VMEM_WEIGHT_PREP = 40 * MIB

F32 = jnp.float32
BF16 = jnp.bfloat16


def _rmsnorm(x, gain):
    ms = jnp.mean(x * x, axis=-1, keepdims=True)
    return x * lax.rsqrt(ms + NORM_EPS) * gain


FFN_TM = 1024
FFN_TF = 512
FFN_STEPS = D_FF // FFN_TF
SLABS_PER_TILE = 2
FFN_SLABS = SLABS_PER_TILE * FFN_STEPS
FFN_SLAB_ROWS = D_MODEL // SLABS_PER_TILE


def _ffn_kernel(x_ref, gain_ref, wg_ref, wu_ref, wd_ref, wd_last_ref, fgain_ref, o_ref, h_ref, a_ref,
                *, final_norm):
    j = pl.program_id(1)
    last = FFN_STEPS - 1

    def gate_up():
        h = h_ref[...]
        g = jnp.dot(h, wg_ref[...], preferred_element_type=F32)
        u = jnp.dot(h, wu_ref[...], preferred_element_type=F32)
        a_ref[j % 2] = ((g * jax.nn.sigmoid(g)) * u).astype(BF16)

    def down():
        o_ref[...] += jnp.dot(a_ref[(j + 1) % 2], wd_ref[...], preferred_element_type=F32)

    @pl.when(j == 0)
    def _():
        h_ref[...] = _rmsnorm(x_ref[...], gain_ref[...]).astype(BF16)
        o_ref[...] = jnp.zeros_like(o_ref)
        gate_up()

    @pl.when((j > 0) & (j < last))
    def _():
        down()
        gate_up()

    @pl.when(j == last)
    def _():
        down()
        gate_up()
        o_ref[...] += jnp.dot(a_ref[last % 2], wd_last_ref[...], preferred_element_type=F32)
        y = x_ref[...] + 0.5 * o_ref[...]
        if final_norm:
            y = _rmsnorm(y, fgain_ref[...])
        o_ref[...] = y


def _ffn(x, gain, wg, wu, wd, fgain, final_norm):
    grid = (N_TOK // FFN_TM, FFN_STEPS)
    last = FFN_STEPS - 1
    return pl.pallas_call(
        functools.partial(_ffn_kernel, final_norm=final_norm),
        grid=grid,
        in_specs=[
            pl.BlockSpec((FFN_TM, D_MODEL), lambda i, j: (i, 0)),
            pl.BlockSpec((1, D_MODEL), lambda i, j: (0, 0)),
            pl.BlockSpec((None, D_MODEL, FFN_TF), lambda i, j: (j, 0, 0)),
            pl.BlockSpec((None, D_MODEL, FFN_TF), lambda i, j: (j, 0, 0)),
            pl.BlockSpec((FFN_TF, D_MODEL), lambda i, j: (jnp.maximum(j - 1, 0), 0)),
            pl.BlockSpec((FFN_TF, D_MODEL), lambda i, j: (last, 0), pipeline_mode=pl.Buffered(1)),
            pl.BlockSpec((1, D_MODEL), lambda i, j: (0, 0)),
        ],
        out_specs=pl.BlockSpec((FFN_TM, D_MODEL), lambda i, j: (i, 0)),
        out_shape=jax.ShapeDtypeStruct((N_TOK, D_MODEL), F32),
        scratch_shapes=[pltpu.VMEM((FFN_TM, D_MODEL), BF16),
                        pltpu.VMEM((2, FFN_TM, FFN_TF), BF16)],
        compiler_params=pltpu.CompilerParams(
            dimension_semantics=("parallel", "arbitrary"), vmem_limit_bytes=VMEM_ROW_TILE_KERNELS),
        name="ffn",
    )(x, gain, wg, wu, wd, wd, fgain)


QKV_TM = 1024
N_MIX_QKV = 3
QI, KI, VI = range(N_MIX_QKV)
N_QKV_TILES = 2 * N_MIX_QKV


ROPE_HALF = ROPE_DIM // 2
ROPE_PARTNER = LANES // 2

DIL_RES = max(d for _, d in DIL_PATTERNS)
DIL_RES_LO = 4
assert DIL_RES == DIL_RES_LO * DIL_RES_LO
DIL_UNIT = QKV_TM
ROWS_PER_RES = DIL_UNIT // DIL_RES


def _res_slot(res):
    return DIL_RES_LO * (res % DIL_RES_LO) + res // DIL_RES_LO


def _rope(x, cos, sin_signed):
    return x * cos + pltpu.roll(x, ROPE_PARTNER, 1) * sin_signed


NT_DIMS = (((1,), (1,)), ((), ()))


def _qkv_kernel(x_ref, gain_ref, w_ref, wf_ref, bf_ref, cos_ref, sin_ref,
                dil_ref, fox_ref, vt_ref, logf_ref, h_ref, r_ref, tmp_ref):
    n = pl.program_id(1)
    scale = (HEAD_DIM ** -0.5) * LOG2E
    is_q = n <= 2

    def matmul():
        r = lax.dot_general(h_ref[...], w_ref[...], NT_DIMS, preferred_element_type=F32)
        for hh in range(N_HEADS):
            r_ref[hh] = r[:, hh * HEAD_DIM:(hh + 1) * HEAD_DIM]

    def heads():
        return [r_ref[hh] for hh in range(N_HEADS)]

    def finish_dil(rotary, scaled):
        s = jnp.where(scaled, scale, 1.0)
        quarter = DIL_UNIT // DIL_RES_LO
        for hh in range(N_HEADS):
            for a in range(DIL_RES_LO):
                tmp_ref[a * quarter:(a + 1) * quarter, :] = r_ref[
                    hh, pl.ds(a, quarter, stride=DIL_RES_LO), :]
            for a in range(DIL_RES_LO):
                for b in range(DIL_RES_LO):
                    slot = DIL_RES_LO * a + b
                    rows = slice(slot * ROWS_PER_RES, (slot + 1) * ROWS_PER_RES)
                    t = tmp_ref[pl.ds(a * quarter + b, ROWS_PER_RES, stride=DIL_RES_LO), :]
                    cos = jnp.where(rotary, cos_ref[rows, :], 1.0) * s
                    sin = jnp.where(rotary, sin_ref[rows, :], 0.0) * s
                    dil_ref[hh, rows, :] = _rope(t, cos, sin)

    def finish_fox(scaled):
        s = jnp.where(scaled, scale, 1.0)
        for hh, t in enumerate(heads()):
            fox_ref[hh] = (t * s).astype(BF16)

    @pl.when(n == 0)
    def _():
        h = _rmsnorm(x_ref[...], gain_ref[...]).astype(BF16)
        h_ref[...] = h
        f = lax.dot_general(h, wf_ref[...], NT_DIMS, preferred_element_type=F32) + bf_ref[...]
        logf_ref[...] = jnp.minimum(f, 0.0) - jnp.log1p(jnp.exp(-jnp.abs(f)))
        matmul()

    @pl.when((n == 1) | (n == 3))
    def _():
        finish_dil(True, is_q)
        matmul()

    @pl.when((n == 2) | (n == 4))
    def _():
        finish_fox(is_q)
        matmul()

    @pl.when(n == N_QKV_TILES - 1)
    def _():
        finish_dil(False, False)
        matmul()
        for hh, t in enumerate(heads()):
            vt_ref[hh] = t.T.astype(BF16)


def _qkv_proj(x, gain, w_qkv, w_f, b_f, cos, sin_signed):
    s_tiles = SEQ // QKV_TM
    grid = (N_TOK // QKV_TM, N_QKV_TILES)

    def head_block(m, which):
        return (which, m // s_tiles, 0, m % s_tiles, 0)

    return pl.pallas_call(
        _qkv_kernel,
        grid=grid,
        in_specs=[
            pl.BlockSpec((QKV_TM, D_MODEL), lambda m, n: (m, 0)),
            pl.BlockSpec((1, D_MODEL), lambda m, n: (0, 0)),
            pl.BlockSpec((None, D_ATT, D_MODEL), lambda m, n: (n, 0, 0)),
            pl.BlockSpec((LANES, D_MODEL), lambda m, n: (0, 0)),
            pl.BlockSpec((1, LANES), lambda m, n: (0, 0)),
            pl.BlockSpec((QKV_TM, LANES), lambda m, n: (m % s_tiles, 0)),
            pl.BlockSpec((QKV_TM, LANES), lambda m, n: (m % s_tiles, 0)),
        ],
        out_specs=[
            pl.BlockSpec((None, None, N_HEADS, QKV_TM, HEAD_DIM),
                         lambda m, n: head_block(m, jnp.maximum(n - 1, 0) // 2)),
            pl.BlockSpec((None, None, N_HEADS, QKV_TM, HEAD_DIM),
                         lambda m, n: head_block(m, jnp.clip(n - 3, 0, 1))),
            pl.BlockSpec((None, N_HEADS, HEAD_DIM, QKV_TM),
                         lambda m, n: (m // s_tiles, 0, 0, m % s_tiles)),
            pl.BlockSpec((QKV_TM, LANES), lambda m, n: (m, 0)),
        ],
        out_shape=[
            jax.ShapeDtypeStruct((N_MIX_QKV, BATCH, N_HEADS, SEQ, HEAD_DIM), F32),
            jax.ShapeDtypeStruct((2, BATCH, N_HEADS, SEQ, HEAD_DIM), BF16),
            jax.ShapeDtypeStruct((BATCH, N_HEADS, HEAD_DIM, SEQ), BF16),
            jax.ShapeDtypeStruct((N_TOK, LANES), F32),
        ],
        scratch_shapes=[pltpu.VMEM((QKV_TM, D_MODEL), BF16),
                        pltpu.VMEM((N_HEADS, QKV_TM, HEAD_DIM), F32),
                        pltpu.VMEM((QKV_TM, HEAD_DIM), F32)],
        compiler_params=pltpu.CompilerParams(
            dimension_semantics=("parallel", "arbitrary"), vmem_limit_bytes=VMEM_ROW_TILE_KERNELS),
        name="qkv_proj",
    )(x, gain, w_qkv, w_f, b_f, cos, sin_signed)


CUM_TS = 1024
CUM_CHUNK = 128


def _cumsum_kernel(lf_ref, c_ref, carry_ref):
    @pl.when(pl.program_id(1) == 0)
    def _():
        carry_ref[...] = jnp.zeros_like(carry_ref)

    r = lax.broadcasted_iota(jnp.int32, (CUM_CHUNK, CUM_CHUNK), 0)
    c = lax.broadcasted_iota(jnp.int32, (CUM_CHUNK, CUM_CHUNK), 1)
    tri = (c <= r).astype(F32)
    carry = carry_ref[...]
    for i in range(CUM_TS // CUM_CHUNK):
        rows = slice(i * CUM_CHUNK, (i + 1) * CUM_CHUNK)
        cs = jnp.dot(tri, lf_ref[rows, :], preferred_element_type=F32,
                     precision=lax.Precision.HIGHEST) + carry
        c_ref[rows, :] = cs
        carry = cs[CUM_CHUNK - 1:CUM_CHUNK, :]
    carry_ref[...] = carry


def _cumsum(logf):
    return pl.pallas_call(
        _cumsum_kernel,
        grid=(BATCH, SEQ // CUM_TS),
        in_specs=[pl.BlockSpec((None, CUM_TS, LANES), lambda b, s: (b, s, 0))],
        out_specs=pl.BlockSpec((None, CUM_TS, LANES), lambda b, s: (b, s, 0)),
        out_shape=jax.ShapeDtypeStruct((BATCH, SEQ, LANES), F32),
        scratch_shapes=[pltpu.VMEM((1, LANES), F32)],
        compiler_params=pltpu.CompilerParams(dimension_semantics=("parallel", "arbitrary")),
        name="gate_cumsum",
    )(logf)


FOX_TQ = 2048
FOX_TK = 512
FOX_DIAG = FOX_TQ // FOX_TK
BF16_ROWS = 16
FOX_VT_ROWS = HEAD_DIM + BF16_ROWS
N_SPLIT = 3


def _key_gate_columns(c_tile, head):
    lane = lax.broadcasted_iota(jnp.int32, c_tile.shape, 1)
    rest = -jnp.sum(jnp.where(lane == head, c_tile, 0.0), axis=-1, keepdims=True) * LOG2E
    cols = jnp.where(lane < N_SPLIT, 1.0, 0.0)
    for t in range(N_SPLIT):
        term = rest.astype(BF16).astype(F32)
        cols = jnp.where(lane == N_SPLIT + t, term, cols)
        rest = rest - term
    return cols.astype(BF16)


def _query_gate_columns(key_cols):
    k = key_cols.astype(F32)
    lane = lax.broadcasted_iota(jnp.int32, k.shape, 1)
    terms = -pltpu.roll(k, LANES - N_SPLIT, 1)
    cols = jnp.where(lane < N_SPLIT, terms, jnp.where(lane < 2 * N_SPLIT, 1.0, 0.0))
    return cols.astype(BF16)


def _fox_kernel(q_ref, k_ref, vt_ref, c_ref, wg_ref, wu_ref, o_ref, wg_out, wu_out,
                kx_ref, st_ref, mx_ref, m_ref, acc_ref):
    head = pl.program_id(0) % N_HEADS
    qi = pl.program_id(1)

    step = pl.program_id(0) * pl.num_programs(1) + qi

    @pl.when(step < FFN_SLABS)
    def _():
        wg_out[...] = wg_ref[...].astype(BF16)

    @pl.when((step >= FFN_SLABS) & (step < 2 * FFN_SLABS))
    def _():
        wu_out[...] = wu_ref[...].astype(BF16)

    ones_tile = (lax.broadcasted_iota(jnp.int32, (BF16_ROWS, FOX_TK), 0) == 0).astype(BF16)

    @pl.when(qi == 0)
    def _():
        def build(i, carry):
            rows = pl.ds(pl.multiple_of(i * FOX_TK, FOX_TK), FOX_TK)
            kx_ref[rows, :] = _key_gate_columns(c_ref[rows, :], head)
            return carry

        lax.fori_loop(0, SEQ // FOX_TK, build, 0)

    q_rows = pl.ds(pl.multiple_of(qi * FOX_TQ, FOX_TQ), FOX_TQ)
    q_aug = jnp.concatenate([q_ref[...], _query_gate_columns(kx_ref[q_rows, :])], axis=-1)

    def key_rows(kb):
        return pl.ds(pl.multiple_of(kb * FOX_TK, FOX_TK), FOX_TK)

    def scores(kb, slot, q_lo=0):
        rows = key_rows(kb)
        k_aug = jnp.concatenate([k_ref[rows, :], kx_ref[rows, :]], axis=-1)
        st = lax.dot_general(k_aug, q_aug[q_lo:, :], (((1,), (1,)), ((), ())),
                             preferred_element_type=F32)
        st_ref[slot, :, q_lo:] = st
        mx_ref[slot, :, q_lo:] = jnp.max(st, axis=0, keepdims=True)

    def accumulate(kb, slot, q_lo=0, key_lo=None):
        st = st_ref[slot, :, q_lo:]
        if key_lo is None:
            st_max = mx_ref[slot, :, q_lo:]
        else:
            key = lax.broadcasted_iota(jnp.int32, st.shape, 0) + key_lo
            qry = lax.broadcasted_iota(jnp.int32, st.shape, 1) + q_lo
            st = jnp.where(key <= qry, st, NEG_BIG)
            st_max = jnp.max(st, axis=0, keepdims=True)
        m = m_ref[:, q_lo:]
        m_new = jnp.maximum(m, st_max)
        p = jnp.exp2(st - m_new).astype(BF16)
        vt = jnp.concatenate([vt_ref[:, key_rows(kb)], ones_tile], axis=0)
        pv = jnp.dot(vt, p, preferred_element_type=F32)
        acc_ref[:, q_lo:] = jnp.exp2(m - m_new) * acc_ref[:, q_lo:] + pv
        m_ref[:, q_lo:] = m_new

    m_ref[...] = jnp.full(m_ref.shape, NEG_BIG, F32)
    acc_ref[...] = jnp.zeros_like(acc_ref)
    scores(0, 0)

    def block(j, carry):
        for t in range(FOX_DIAG):
            tile = FOX_DIAG * j + t
            scores(tile + 1, (t + 1) % 2)
            accumulate(tile, t % 2)
        return carry

    first = qi * FOX_DIAG
    lax.fori_loop(0, qi, block, 0)
    for d in range(FOX_DIAG):
        if d + 1 < FOX_DIAG:
            scores(first + d + 1, (d + 1) % 2, q_lo=(d + 1) * FOX_TK)
        accumulate(first + d, d % 2, q_lo=d * FOX_TK, key_lo=d * FOX_TK)

    out_t = acc_ref[0:HEAD_DIM, :] / acc_ref[HEAD_DIM:HEAD_DIM + 1, :]
    o_ref[...] = out_t.T.astype(BF16)


def _fox_attn(qk, vt, c_gate, w_gate, w_up):
    assert FOX_TQ % (2 * FOX_TK) == 0
    q_tiles = SEQ // FOX_TQ
    assert BATCH * N_HEADS * q_tiles >= 2 * FFN_SLABS

    def slab(bh, qi, first):
        return jnp.clip(bh * q_tiles + qi - first, 0, FFN_SLABS - 1)

    def slab_in(first):
        return pl.BlockSpec((FFN_SLAB_ROWS, FFN_TF), lambda bh, qi: (
            slab(bh, qi, first) % SLABS_PER_TILE, slab(bh, qi, first) // SLABS_PER_TILE))

    def slab_out(first):
        return pl.BlockSpec((None, FFN_SLAB_ROWS, FFN_TF), lambda bh, qi: (
            slab(bh, qi, first) // SLABS_PER_TILE, slab(bh, qi, first) % SLABS_PER_TILE, 0))

    tiles = jax.ShapeDtypeStruct((FFN_STEPS, D_MODEL, FFN_TF), BF16)
    return pl.pallas_call(
        _fox_kernel,
        grid=(BATCH * N_HEADS, q_tiles),
        in_specs=[
            pl.BlockSpec((None, None, None, FOX_TQ, HEAD_DIM),
                         lambda bh, qi: (QI, bh // N_HEADS, bh % N_HEADS, qi, 0)),
            pl.BlockSpec((None, None, None, SEQ, HEAD_DIM),
                         lambda bh, qi: (KI, bh // N_HEADS, bh % N_HEADS, 0, 0)),
            pl.BlockSpec((None, None, HEAD_DIM, SEQ),
                         lambda bh, qi: (bh // N_HEADS, bh % N_HEADS, 0, 0)),
            pl.BlockSpec((None, SEQ, LANES), lambda bh, qi: (bh // N_HEADS, 0, 0)),
            slab_in(0),
            slab_in(FFN_SLABS),
        ],
        out_specs=[
            pl.BlockSpec((FOX_TQ, HEAD_DIM),
                         lambda bh, qi: ((bh // N_HEADS) * q_tiles + qi, bh % N_HEADS)),
            slab_out(0),
            slab_out(FFN_SLABS),
        ],
        out_shape=[jax.ShapeDtypeStruct((N_TOK, D_ATT), BF16), tiles, tiles],
        scratch_shapes=[
            pltpu.VMEM((SEQ, LANES), BF16),
            pltpu.VMEM((2, FOX_TK, FOX_TQ), F32),
            pltpu.VMEM((2, 1, FOX_TQ), F32),
            pltpu.VMEM((1, FOX_TQ), F32),
            pltpu.VMEM((FOX_VT_ROWS, FOX_TQ), F32),
        ],
        compiler_params=pltpu.CompilerParams(
            dimension_semantics=("arbitrary", "arbitrary"), vmem_limit_bytes=VMEM_ATTENTION),
        name="fox_attn",
    )(qk, qk, vt, c_gate, w_gate, w_up)


DIL_GROUP = 128
DIL_SUPER = DIL_GROUP * max(d for _, d in DIL_PATTERNS)
N_PAT = len(DIL_PATTERNS)
assert all(w == DIL_GROUP * d and DIL_SUPER % w == 0 for w, d in DIL_PATTERNS)


def _dil_group_order(d):
    cycle = DIL_RES // d
    run = DIL_GROUP // cycle
    i = np.arange(DIL_GROUP)
    return cycle * (i % run) + i // run


def _dil_band_bias():
    tables = []
    for _, d in DIL_PATTERNS:
        order = _dil_group_order(d)
        k = order[:, None]
        j = np.concatenate([order, DIL_GROUP + order])[None, :]
        tables.append(np.where((j >= k) & (j <= k + DIL_GROUP), 0.0, NEG_BIG))
    return jnp.asarray(np.stack(tables), F32)


def _dil_chunks(d, u, c):
    cycle = DIL_RES // d
    run = DIL_GROUP // cycle
    t_rel = u * DIL_GROUP * d
    out = []
    for b in range(cycle):
        for lo in range(0, run, ROWS_PER_RES):
            unit, off = divmod(t_rel + DIL_RES * lo, DIL_UNIT)
            out.append((unit * DIL_UNIT + ROWS_PER_RES * _res_slot(c + d * b) + off // DIL_RES,
                        min(run - lo, ROWS_PER_RES)))
    return out


def _dil_kernel(q_ref, k_ref, v_ref, band_ref, wd_ref, o_ref, wd_out,
                num_ref, den_ref, mx_ref, nat_ref):
    sb = pl.program_id(1)

    @pl.when(pl.program_id(0) * pl.num_programs(1) + sb < FFN_SLABS)
    def _():
        wd_out[...] = wd_ref[...].astype(BF16)

    base = sb * DIL_SUPER
    col = lax.broadcasted_iota(jnp.int32, (DIL_GROUP, 2 * DIL_GROUP), 1)
    ones = jnp.ones((2 * DIL_GROUP, HEAD_DIM), BF16)

    def gather(ref, pieces, dynamic):
        parts = []
        for start, n in pieces:
            if dynamic:
                row = base + start
                row = jnp.maximum(row, 0) if start < 0 else row
                parts.append(ref[pl.ds(pl.multiple_of(row, F32_ROWS), n), :])
            else:
                parts.append(ref[start:start + n, :])
        return jnp.concatenate(parts, axis=0)

    for g, (_, d) in enumerate(DIL_PATTERNS):
        band = band_ref[g]
        band_first = jnp.where((sb == 0) & (col < DIL_GROUP), NEG_BIG, band)
        for u in range(DIL_SUPER // (DIL_GROUP * d)):
            bias = band_first if u == 0 else band
            for c in range(d):
                own = _dil_chunks(d, u, c)
                window = _dil_chunks(d, u - 1, c) + own
                q = gather(q_ref, own, False).astype(BF16)
                k = gather(k_ref, window, True).astype(BF16)
                v = gather(v_ref, window, True).astype(BF16)
                s = lax.dot_general(q, k, NT_DIMS, preferred_element_type=F32) + bias
                m = jnp.max(s, axis=-1, keepdims=True)
                p = jnp.exp2(s - m).astype(BF16)
                r = jnp.dot(p, jnp.concatenate([v, ones], axis=-1), preferred_element_type=F32)
                m_rep = jnp.broadcast_to(m, (DIL_GROUP, HEAD_DIM))
                at = 0
                for start, n in own:
                    num_ref[g, start:start + n, :] = r[at:at + n, :HEAD_DIM]
                    den_ref[g, start:start + n, :] = r[at:at + n, HEAD_DIM:]
                    mx_ref[g, start:start + n, :] = m_rep[at:at + n, :]
                    at += n

    def combine(i, carry):
        r = pl.ds(pl.multiple_of(i * ROWS_PER_RES, ROWS_PER_RES), ROWS_PER_RES)
        mx = [mx_ref[g, r, :] for g in range(N_PAT)]
        top = functools.reduce(jnp.maximum, mx)
        wgt = [jnp.exp2(t - top) for t in mx]
        num = functools.reduce(jnp.add, [wgt[g] * num_ref[g, r, :] for g in range(N_PAT)])
        den = functools.reduce(jnp.add, [wgt[g] * den_ref[g, r, :] for g in range(N_PAT)])
        unit, res = i // DIL_RES, _res_slot(i % DIL_RES)
        nat_ref[pl.ds(unit * DIL_UNIT + res, ROWS_PER_RES, stride=DIL_RES), :] = num / den
        return carry

    lax.fori_loop(0, DIL_SUPER // ROWS_PER_RES, combine, 0)
    o_ref[...] = nat_ref[...].astype(BF16)


def _dil_attn(qkv, band, w_down):
    steps = SEQ // DIL_SUPER
    assert BATCH * N_HEADS * steps >= FFN_SLABS
    down_rows = D_FF // FFN_SLABS
    down_slab = pl.BlockSpec((down_rows, D_MODEL),
                             lambda bh, sb: (jnp.minimum(bh * steps + sb, FFN_SLABS - 1), 0))
    per_pattern = pltpu.VMEM((N_PAT, DIL_SUPER, HEAD_DIM), F32)
    return pl.pallas_call(
        _dil_kernel,
        grid=(BATCH * N_HEADS, steps),
        in_specs=[
            pl.BlockSpec((None, None, None, DIL_SUPER, HEAD_DIM),
                         lambda bh, sb: (QI, bh // N_HEADS, bh % N_HEADS, sb, 0)),
            pl.BlockSpec((None, None, None, SEQ, HEAD_DIM),
                         lambda bh, sb: (KI, bh // N_HEADS, bh % N_HEADS, 0, 0)),
            pl.BlockSpec((None, None, None, SEQ, HEAD_DIM),
                         lambda bh, sb: (VI, bh // N_HEADS, bh % N_HEADS, 0, 0)),
            pl.BlockSpec((N_PAT, DIL_GROUP, 2 * DIL_GROUP), lambda bh, sb: (0, 0, 0)),
            down_slab,
        ],
        out_specs=[
            pl.BlockSpec((DIL_SUPER, HEAD_DIM),
                         lambda bh, sb: ((bh // N_HEADS) * steps + sb, bh % N_HEADS)),
            down_slab,
        ],
        out_shape=[jax.ShapeDtypeStruct((N_TOK, D_ATT), BF16),
                   jax.ShapeDtypeStruct((D_FF, D_MODEL), BF16)],
        scratch_shapes=[per_pattern, per_pattern, per_pattern,
                        pltpu.VMEM((DIL_SUPER, HEAD_DIM), F32)],
        compiler_params=pltpu.CompilerParams(
            dimension_semantics=("arbitrary", "arbitrary"), vmem_limit_bytes=VMEM_ATTENTION),
        name="dil_attn",
    )(qkv, qkv, qkv, band, w_down)


MRG_TM = 1024
MRG_TJ = 256


MRG_STEPS = D_MODEL // MRG_TJ


def _merge_kernel(x_ref, gain_ref, yd_ref, yf_ref, wgd_ref, wgf_ref, bgd_ref, bgf_ref,
                  wpd_ref, wpf_ref, wo_ref, wo_last_ref, o_ref, h_ref, m_ref):
    j = pl.program_id(1)
    last = MRG_STEPS - 1

    def merge_tile():
        h = h_ref[...]
        gd = lax.dot_general(h, wgd_ref[...], NT_DIMS, preferred_element_type=F32) + bgd_ref[...]
        gf = lax.dot_general(h, wgf_ref[...], NT_DIMS, preferred_element_type=F32) + bgf_ref[...]
        pd = jnp.dot(yd_ref[...], wpd_ref[...], preferred_element_type=F32)
        pf = jnp.dot(yf_ref[...], wpf_ref[...], preferred_element_type=F32)
        m_ref[j % 2] = (jax.nn.sigmoid(gd) * pd + jax.nn.sigmoid(gf) * pf).astype(BF16)

    def project():
        o_ref[...] += jnp.dot(m_ref[(j + 1) % 2], wo_ref[...], preferred_element_type=F32)

    @pl.when(j == 0)
    def _():
        h_ref[...] = _rmsnorm(x_ref[...], gain_ref[...]).astype(BF16)
        o_ref[...] = jnp.zeros_like(o_ref)
        merge_tile()

    @pl.when((j > 0) & (j < last))
    def _():
        project()
        merge_tile()

    @pl.when(j == last)
    def _():
        project()
        merge_tile()
        o_ref[...] += jnp.dot(m_ref[last % 2], wo_last_ref[...], preferred_element_type=F32)
        o_ref[...] = x_ref[...] + o_ref[...]


def _merge(x, gain, yd, yf, w_gates, bgd, bgf, wpd, wpf, wo):
    grid = (N_TOK // MRG_TM, MRG_STEPS)
    last = MRG_STEPS - 1
    return pl.pallas_call(
        _merge_kernel,
        grid=grid,
        in_specs=[
            pl.BlockSpec((MRG_TM, D_MODEL), lambda i, j: (i, 0)),
            pl.BlockSpec((1, D_MODEL), lambda i, j: (0, 0)),
            pl.BlockSpec((MRG_TM, D_ATT), lambda i, j: (i, 0)),
            pl.BlockSpec((MRG_TM, D_ATT), lambda i, j: (i, 0)),
            pl.BlockSpec((None, MRG_TJ, D_MODEL), lambda i, j: (j, 0, 0)),
            pl.BlockSpec((None, MRG_TJ, D_MODEL), lambda i, j: (MRG_STEPS + j, 0, 0)),
            pl.BlockSpec((1, MRG_TJ), lambda i, j: (0, j)),
            pl.BlockSpec((1, MRG_TJ), lambda i, j: (0, j)),
            pl.BlockSpec((None, D_ATT, MRG_TJ), lambda i, j: (j, 0, 0)),
            pl.BlockSpec((None, D_ATT, MRG_TJ), lambda i, j: (j, 0, 0)),
            pl.BlockSpec((MRG_TJ, D_MODEL), lambda i, j: (jnp.maximum(j - 1, 0), 0)),
            pl.BlockSpec((MRG_TJ, D_MODEL), lambda i, j: (last, 0), pipeline_mode=pl.Buffered(1)),
        ],
        out_specs=pl.BlockSpec((MRG_TM, D_MODEL), lambda i, j: (i, 0)),
        out_shape=jax.ShapeDtypeStruct((N_TOK, D_MODEL), F32),
        scratch_shapes=[pltpu.VMEM((MRG_TM, D_MODEL), BF16),
                        pltpu.VMEM((2, MRG_TM, MRG_TJ), BF16)],
        compiler_params=pltpu.CompilerParams(
            dimension_semantics=("parallel", "arbitrary"), vmem_limit_bytes=VMEM_ROW_TILE_KERNELS),
        name="merge_proj",
    )(x, gain, yd, yf, w_gates, w_gates, bgd, bgf, wpd, wpf, wo, wo)


def _rope_tables():
    pos = np.arange(SEQ, dtype=np.float64)
    inv_freq = ROPE_THETA ** (-np.arange(0, ROPE_DIM, 2, dtype=np.float64) / ROPE_DIM)
    ang = pos[:, None] * inv_freq[None, :]
    cos, sin = np.cos(ang), np.sin(ang)
    gap = ROPE_PARTNER - ROPE_HALF
    tail = HEAD_DIM - ROPE_PARTNER - ROPE_HALF
    one, zero = (lambda n: np.ones((SEQ, n))), (lambda n: np.zeros((SEQ, n)))
    cos_full = np.concatenate([cos, one(gap), cos, one(tail)], axis=-1)
    sin_signed = np.concatenate([-sin, zero(gap), sin, zero(tail)], axis=-1)
    assert cos_full.shape == (SEQ, HEAD_DIM)

    def residue_major(t):
        t = t.reshape(SEQ // DIL_UNIT, ROWS_PER_RES, DIL_RES, HEAD_DIM).transpose(0, 2, 1, 3)
        t = t[:, [_res_slot(slot) for slot in range(DIL_RES)]]
        return jnp.asarray(t.reshape(SEQ, HEAD_DIM), F32)

    return residue_major(cos_full), residue_major(sin_signed)


PREP_ROWS = 512
C_QKV = 2 * N_MIX_QKV * D_ATT
C_GATES = C_QKV + N_HEADS
_PREP_PARAMS = pltpu.CompilerParams(dimension_semantics=("parallel",),
                                    vmem_limit_bytes=VMEM_WEIGHT_PREP)


def _cast_kernel(w_ref, o_ref):
    o_ref[...] = w_ref[...].astype(BF16)


def _cast_rows(w):
    rows, n_cols = w.shape
    return pl.pallas_call(
        _cast_kernel,
        grid=(rows // PREP_ROWS,),
        in_specs=[pl.BlockSpec((PREP_ROWS, n_cols), lambda i: (i, 0))],
        out_specs=pl.BlockSpec((PREP_ROWS, n_cols), lambda i: (i, 0)),
        out_shape=jax.ShapeDtypeStruct(w.shape, BF16),
        compiler_params=_PREP_PARAMS,
        name="cast_rows",
    )(w)


def _cast_column_tiles(w, tile):
    rows, n_cols = w.shape
    return pl.pallas_call(
        _cast_kernel,
        grid=(n_cols // tile,),
        in_specs=[pl.BlockSpec((rows, tile), lambda j: (0, j))],
        out_specs=pl.BlockSpec((None, rows, tile), lambda j: (j, 0, 0)),
        out_shape=jax.ShapeDtypeStruct((n_cols // tile, rows, tile), BF16),
        compiler_params=_PREP_PARAMS,
        name="cast_column_tiles",
    )(w)


def _qkv_weight_kernel(w_ref, o_ref):
    j = pl.program_id(0)
    rotary = (j % 2 == 0) & (j < 2 * (N_MIX_QKV - 1))

    @pl.when(rotary)
    def _():
        for hh in range(N_HEADS):
            base = hh * HEAD_DIM
            for dst, src, n in ((0, 0, ROPE_HALF),
                                (ROPE_HALF, ROPE_PARTNER, ROPE_HALF),
                                (ROPE_DIM, ROPE_DIM, ROPE_PARTNER - ROPE_DIM),
                                (ROPE_PARTNER, ROPE_HALF, ROPE_HALF),
                                (ROPE_PARTNER + ROPE_HALF, ROPE_PARTNER + ROPE_HALF,
                                 HEAD_DIM - ROPE_PARTNER - ROPE_HALF)):
                o_ref[base + dst:base + dst + n, :] = w_ref[base + src:base + src + n, :].astype(BF16)

    @pl.when(jnp.logical_not(rotary))
    def _():
        o_ref[...] = w_ref[...].astype(BF16)


def _qkv_weights(w_t):
    return pl.pallas_call(
        _qkv_weight_kernel,
        grid=(N_QKV_TILES,),
        in_specs=[pl.BlockSpec((D_ATT, D_MODEL), lambda j: ((j % 2) * N_MIX_QKV + j // 2, 0))],
        out_specs=pl.BlockSpec((None, D_ATT, D_MODEL), lambda j: (j, 0, 0)),
        out_shape=jax.ShapeDtypeStruct((N_QKV_TILES, D_ATT, D_MODEL), BF16),
        compiler_params=_PREP_PARAMS,
        name="qkv_weights",
    )(w_t)


F32_ROWS = 8
GATE_ROW_OFF = C_GATES % MRG_TJ


def _gate_weight_kernel(a_ref, b_ref, f_ref, o_ref, wf_ref):
    o_ref[...] = jnp.concatenate([a_ref[GATE_ROW_OFF:, :], b_ref[...]], axis=0).astype(BF16)

    @pl.when(pl.program_id(0) == 0)
    def _():
        pad = jnp.zeros((LANES - N_HEADS, D_MODEL), F32)
        wf_ref[...] = jnp.concatenate([f_ref[...], pad], axis=0).astype(BF16)


def _gate_weights(w_t):
    assert GATE_ROW_OFF == N_HEADS == F32_ROWS and C_QKV % MRG_TJ == 0
    first_tile = C_QKV // MRG_TJ
    return pl.pallas_call(
        _gate_weight_kernel,
        grid=(2 * MRG_STEPS,),
        in_specs=[
            pl.BlockSpec((MRG_TJ, D_MODEL), lambda t: (first_tile + t, 0)),
            pl.BlockSpec((GATE_ROW_OFF, D_MODEL),
                         lambda t: ((C_QKV + MRG_TJ * (t + 1)) // GATE_ROW_OFF, 0)),
            pl.BlockSpec((N_HEADS, D_MODEL), lambda t: (C_QKV // N_HEADS, 0)),
        ],
        out_specs=[
            pl.BlockSpec((None, MRG_TJ, D_MODEL), lambda t: (t, 0, 0)),
            pl.BlockSpec((LANES, D_MODEL), lambda t: (0, 0)),
        ],
        out_shape=[
            jax.ShapeDtypeStruct((2 * MRG_STEPS, MRG_TJ, D_MODEL), BF16),
            jax.ShapeDtypeStruct((LANES, D_MODEL), BF16),
        ],
        compiler_params=pltpu.CompilerParams(dimension_semantics=("arbitrary",),
                                             vmem_limit_bytes=VMEM_WEIGHT_PREP),
        name="gate_weights",
    )(w_t, w_t, w_t)


def kernel(x, ffn1_norm, ffn1_w_gate, ffn1_w_up, ffn1_w_down, mix_norm, w_in, b_forget, b_gate_dil, b_gate_fox, w_proj_dil, w_proj_fox, w_out, ffn2_norm, ffn2_w_gate, ffn2_w_up, ffn2_w_down, final_norm):
    assert x.shape == (BATCH, SEQ, D_MODEL) and x.dtype == F32
    assert w_in.shape == (DEPTH, D_MODEL, C_GATES + 2 * D_MODEL)
    cos_full, sin_signed = _rope_tables()
    dil_band = _dil_band_bias()
    fgain = final_norm.reshape(1, D_MODEL)

    xt = x.reshape(N_TOK, D_MODEL)
    for l in range(DEPTH):
        last = l == DEPTH - 1
        w_t = jnp.swapaxes(w_in[l], 0, 1)
        w_qkv = _qkv_weights(w_t)
        w_gates, w_f = _gate_weights(w_t)
        b_f = jnp.pad(b_forget[l], (0, LANES - N_HEADS)).reshape(1, LANES)

        xt = _ffn(xt, ffn1_norm[l].reshape(1, D_MODEL), _cast_column_tiles(ffn1_w_gate[l], FFN_TF),
                  _cast_column_tiles(ffn1_w_up[l], FFN_TF), _cast_rows(ffn1_w_down[l]), fgain, False)

        qkv_dil, qk_fox, vt_fox, logf = _qkv_proj(xt, mix_norm[l].reshape(1, D_MODEL), w_qkv, w_f,
                                                  b_f, cos_full, sin_signed)
        c_gate = _cumsum(logf.reshape(BATCH, SEQ, LANES))
        y_fox, w_gate2, w_up2 = _fox_attn(qk_fox, vt_fox, c_gate, ffn2_w_gate[l], ffn2_w_up[l])
        y_dil, w_down2 = _dil_attn(qkv_dil, dil_band, ffn2_w_down[l])

        xt = _merge(xt, mix_norm[l].reshape(1, D_MODEL), y_dil, y_fox, w_gates,
                    b_gate_dil[l].reshape(1, D_MODEL), b_gate_fox[l].reshape(1, D_MODEL),
                    _cast_column_tiles(w_proj_dil[l], MRG_TJ),
                    _cast_column_tiles(w_proj_fox[l], MRG_TJ), _cast_rows(w_out[l]))

        xt = _ffn(xt, ffn2_norm[l].reshape(1, D_MODEL), w_gate2, w_up2, w_down2, fgain, last)
    return xt.reshape(BATCH, SEQ, D_MODEL)
```

```python
import functools

import numpy as np
import jax
import jax.numpy as jnp
from jax import lax
from jax.experimental import pallas as pl
from jax.experimental.pallas import tpu as pltpu

D_MODEL = 2048
BATCH = 2
SEQ = 8192
DEPTH = 1
HEAD_DIM = 128
N_HEADS = 8
D_ATT = N_HEADS * HEAD_DIM
DIL_PATTERNS = ((128, 1), (512, 4), (2048, 16))
MAX_WINDOW = 2048
ROPE_THETA = 500000.0
ROPE_DIM = HEAD_DIM // 4
D_FF = 5632
NORM_EPS = 1e-6
N_TOK = BATCH * SEQ
LANES = 128

NEG_BIG = -1e30
LOG2E = 1.4426950408889634
MIB = 1024 * 1024

V7X_VMEM_BYTES = 64 * MIB
VMEM_ROW_TILE_KERNELS = V7X_VMEM_BYTES - 3 * MIB
VMEM_ATTENTION = 56 * MIB
VMEM_WEIGHT_PREP = 40 * MIB

F32 = jnp.float32
BF16 = jnp.bfloat16


def _rmsnorm(x, gain):
    ms = jnp.mean(x * x, axis=-1, keepdims=True)
    return x * lax.rsqrt(ms + NORM_EPS) * gain


FFN_TM = 1024
FFN_TF = 512
FFN_STEPS = D_FF // FFN_TF
SLABS_PER_TILE = 2
FFN_SLABS = SLABS_PER_TILE * FFN_STEPS
FFN_SLAB_ROWS = D_MODEL // SLABS_PER_TILE


def _ffn_kernel(x_ref, gain_ref, wg_ref, wu_ref, wd_ref, wd_last_ref, fgain_ref, o_ref, h_ref, a_ref,
                *, final_norm):
    j = pl.program_id(1)
    last = FFN_STEPS - 1

    def gate_up():
        h = h_ref[...]
        g = jnp.dot(h, wg_ref[...], preferred_element_type=F32)
        u = jnp.dot(h, wu_ref[...], preferred_element_type=F32)
        a_ref[j % 2] = ((g * jax.nn.sigmoid(g)) * u).astype(BF16)

    def down():
        o_ref[...] += jnp.dot(a_ref[(j + 1) % 2], wd_ref[...], preferred_element_type=F32)

    @pl.when(j == 0)
    def _():
        h_ref[...] = _rmsnorm(x_ref[...], gain_ref[...]).astype(BF16)
        o_ref[...] = jnp.zeros_like(o_ref)
        gate_up()

    @pl.when((j > 0) & (j < last))
    def _():
        down()
        gate_up()

    @pl.when(j == last)
    def _():
        down()
        gate_up()
        o_ref[...] += jnp.dot(a_ref[last % 2], wd_last_ref[...], preferred_element_type=F32)
        y = x_ref[...] + 0.5 * o_ref[...]
        if final_norm:
            y = _rmsnorm(y, fgain_ref[...])
        o_ref[...] = y


def _ffn(x, gain, wg, wu, wd, fgain, final_norm):
    grid = (N_TOK // FFN_TM, FFN_STEPS)
    last = FFN_STEPS - 1
    return pl.pallas_call(
        functools.partial(_ffn_kernel, final_norm=final_norm),
        grid=grid,
        in_specs=[
            pl.BlockSpec((FFN_TM, D_MODEL), lambda i, j: (i, 0)),
            pl.BlockSpec((1, D_MODEL), lambda i, j: (0, 0)),
            pl.BlockSpec((None, D_MODEL, FFN_TF), lambda i, j: (j, 0, 0)),
            pl.BlockSpec((None, D_MODEL, FFN_TF), lambda i, j: (j, 0, 0)),
            pl.BlockSpec((FFN_TF, D_MODEL), lambda i, j: (jnp.maximum(j - 1, 0), 0)),
            pl.BlockSpec((FFN_TF, D_MODEL), lambda i, j: (last, 0), pipeline_mode=pl.Buffered(1)),
            pl.BlockSpec((1, D_MODEL), lambda i, j: (0, 0)),
        ],
        out_specs=pl.BlockSpec((FFN_TM, D_MODEL), lambda i, j: (i, 0)),
        out_shape=jax.ShapeDtypeStruct((N_TOK, D_MODEL), F32),
        scratch_shapes=[pltpu.VMEM((FFN_TM, D_MODEL), BF16),
                        pltpu.VMEM((2, FFN_TM, FFN_TF), BF16)],
        compiler_params=pltpu.CompilerParams(
            dimension_semantics=("parallel", "arbitrary"), vmem_limit_bytes=VMEM_ROW_TILE_KERNELS),
        name="ffn",
    )(x, gain, wg, wu, wd, wd, fgain)


QKV_TM = 1024
N_MIX_QKV = 3
QI, KI, VI = range(N_MIX_QKV)
N_QKV_TILES = 2 * N_MIX_QKV


ROPE_HALF = ROPE_DIM // 2
ROPE_PARTNER = LANES // 2

DIL_RES = max(d for _, d in DIL_PATTERNS)
DIL_RES_LO = 4
assert DIL_RES == DIL_RES_LO * DIL_RES_LO
DIL_UNIT = QKV_TM
ROWS_PER_RES = DIL_UNIT // DIL_RES


def _res_slot(res):
    return DIL_RES_LO * (res % DIL_RES_LO) + res // DIL_RES_LO


def _rope(x, cos, sin_signed):
    return x * cos + pltpu.roll(x, ROPE_PARTNER, 1) * sin_signed


NT_DIMS = (((1,), (1,)), ((), ()))


def _qkv_kernel(x_ref, gain_ref, w_ref, wf_ref, bf_ref, cos_ref, sin_ref,
                dil_ref, fox_ref, vt_ref, logf_ref, h_ref, r_ref, tmp_ref):
    n = pl.program_id(1)
    scale = (HEAD_DIM ** -0.5) * LOG2E
    is_q = n <= 2

    def matmul():
        r = lax.dot_general(h_ref[...], w_ref[...], NT_DIMS, preferred_element_type=F32)
        for hh in range(N_HEADS):
            r_ref[hh] = r[:, hh * HEAD_DIM:(hh + 1) * HEAD_DIM]

    def heads():
        return [r_ref[hh] for hh in range(N_HEADS)]

    def finish_dil(rotary, scaled):
        s = jnp.where(scaled, scale, 1.0)
        quarter = DIL_UNIT // DIL_RES_LO
        for hh in range(N_HEADS):
            for a in range(DIL_RES_LO):
                tmp_ref[a * quarter:(a + 1) * quarter, :] = r_ref[
                    hh, pl.ds(a, quarter, stride=DIL_RES_LO), :]
            for a in range(DIL_RES_LO):
                for b in range(DIL_RES_LO):
                    slot = DIL_RES_LO * a + b
                    rows = slice(slot * ROWS_PER_RES, (slot + 1) * ROWS_PER_RES)
                    t = tmp_ref[pl.ds(a * quarter + b, ROWS_PER_RES, stride=DIL_RES_LO), :]
                    cos = jnp.where(rotary, cos_ref[rows, :], 1.0) * s
                    sin = jnp.where(rotary, sin_ref[rows, :], 0.0) * s
                    dil_ref[hh, rows, :] = _rope(t, cos, sin)

    def finish_fox(scaled):
        s = jnp.where(scaled, scale, 1.0)
        for hh, t in enumerate(heads()):
            fox_ref[hh] = (t * s).astype(BF16)

    @pl.when(n == 0)
    def _():
        h = _rmsnorm(x_ref[...], gain_ref[...]).astype(BF16)
        h_ref[...] = h
        f = lax.dot_general(h, wf_ref[...], NT_DIMS, preferred_element_type=F32) + bf_ref[...]
        logf_ref[...] = jnp.minimum(f, 0.0) - jnp.log1p(jnp.exp(-jnp.abs(f)))
        matmul()

    @pl.when((n == 1) | (n == 3))
    def _():
        finish_dil(True, is_q)
        matmul()

    @pl.when((n == 2) | (n == 4))
    def _():
        finish_fox(is_q)
        matmul()

    @pl.when(n == N_QKV_TILES - 1)
    def _():
        finish_dil(False, False)
        matmul()
        for hh, t in enumerate(heads()):
            vt_ref[hh] = t.T.astype(BF16)


def _qkv_proj(x, gain, w_qkv, w_f, b_f, cos, sin_signed):
    s_tiles = SEQ // QKV_TM
    grid = (N_TOK // QKV_TM, N_QKV_TILES)

    def head_block(m, which):
        return (which, m // s_tiles, 0, m % s_tiles, 0)

    return pl.pallas_call(
        _qkv_kernel,
        grid=grid,
        in_specs=[
            pl.BlockSpec((QKV_TM, D_MODEL), lambda m, n: (m, 0)),
            pl.BlockSpec((1, D_MODEL), lambda m, n: (0, 0)),
            pl.BlockSpec((None, D_ATT, D_MODEL), lambda m, n: (n, 0, 0)),
            pl.BlockSpec((LANES, D_MODEL), lambda m, n: (0, 0)),
            pl.BlockSpec((1, LANES), lambda m, n: (0, 0)),
            pl.BlockSpec((QKV_TM, LANES), lambda m, n: (m % s_tiles, 0)),
            pl.BlockSpec((QKV_TM, LANES), lambda m, n: (m % s_tiles, 0)),
        ],
        out_specs=[
            pl.BlockSpec((None, None, N_HEADS, QKV_TM, HEAD_DIM),
                         lambda m, n: head_block(m, jnp.maximum(n - 1, 0) // 2)),
            pl.BlockSpec((None, None, N_HEADS, QKV_TM, HEAD_DIM),
                         lambda m, n: head_block(m, jnp.clip(n - 3, 0, 1))),
            pl.BlockSpec((None, N_HEADS, HEAD_DIM, QKV_TM),
                         lambda m, n: (m // s_tiles, 0, 0, m % s_tiles)),
            pl.BlockSpec((QKV_TM, LANES), lambda m, n: (m, 0)),
        ],
        out_shape=[
            jax.ShapeDtypeStruct((N_MIX_QKV, BATCH, N_HEADS, SEQ, HEAD_DIM), F32),
            jax.ShapeDtypeStruct((2, BATCH, N_HEADS, SEQ, HEAD_DIM), BF16),
            jax.ShapeDtypeStruct((BATCH, N_HEADS, HEAD_DIM, SEQ), BF16),
            jax.ShapeDtypeStruct((N_TOK, LANES), F32),
        ],
        scratch_shapes=[pltpu.VMEM((QKV_TM, D_MODEL), BF16),
                        pltpu.VMEM((N_HEADS, QKV_TM, HEAD_DIM), F32),
                        pltpu.VMEM((QKV_TM, HEAD_DIM), F32)],
        compiler_params=pltpu.CompilerParams(
            dimension_semantics=("parallel", "arbitrary"), vmem_limit_bytes=VMEM_ROW_TILE_KERNELS),
        name="qkv_proj",
    )(x, gain, w_qkv, w_f, b_f, cos, sin_signed)


CUM_TS = 1024
CUM_CHUNK = 128


def _cumsum_kernel(lf_ref, c_ref, carry_ref):
    @pl.when(pl.program_id(1) == 0)
    def _():
        carry_ref[...] = jnp.zeros_like(carry_ref)

    r = lax.broadcasted_iota(jnp.int32, (CUM_CHUNK, CUM_CHUNK), 0)
    c = lax.broadcasted_iota(jnp.int32, (CUM_CHUNK, CUM_CHUNK), 1)
    tri = (c <= r).astype(F32)
    carry = carry_ref[...]
    for i in range(CUM_TS // CUM_CHUNK):
        rows = slice(i * CUM_CHUNK, (i + 1) * CUM_CHUNK)
        cs = jnp.dot(tri, lf_ref[rows, :], preferred_element_type=F32,
                     precision=lax.Precision.HIGHEST) + carry
        c_ref[rows, :] = cs
        carry = cs[CUM_CHUNK - 1:CUM_CHUNK, :]
    carry_ref[...] = carry


def _cumsum(logf):
    return pl.pallas_call(
        _cumsum_kernel,
        grid=(BATCH, SEQ // CUM_TS),
        in_specs=[pl.BlockSpec((None, CUM_TS, LANES), lambda b, s: (b, s, 0))],
        out_specs=pl.BlockSpec((None, CUM_TS, LANES), lambda b, s: (b, s, 0)),
        out_shape=jax.ShapeDtypeStruct((BATCH, SEQ, LANES), F32),
        scratch_shapes=[pltpu.VMEM((1, LANES), F32)],
        compiler_params=pltpu.CompilerParams(dimension_semantics=("parallel", "arbitrary")),
        name="gate_cumsum",
    )(logf)


FOX_TQ = 2048
FOX_TK = 512
FOX_DIAG = FOX_TQ // FOX_TK
BF16_ROWS = 16
FOX_VT_ROWS = HEAD_DIM + BF16_ROWS
N_SPLIT = 3


def _key_gate_columns(c_tile, head):
    lane = lax.broadcasted_iota(jnp.int32, c_tile.shape, 1)
    rest = -jnp.sum(jnp.where(lane == head, c_tile, 0.0), axis=-1, keepdims=True) * LOG2E
    cols = jnp.where(lane < N_SPLIT, 1.0, 0.0)
    for t in range(N_SPLIT):
        term = rest.astype(BF16).astype(F32)
        cols = jnp.where(lane == N_SPLIT + t, term, cols)
        rest = rest - term
    return cols.astype(BF16)


def _query_gate_columns(key_cols):
    k = key_cols.astype(F32)
    lane = lax.broadcasted_iota(jnp.int32, k.shape, 1)
    terms = -pltpu.roll(k, LANES - N_SPLIT, 1)
    cols = jnp.where(lane < N_SPLIT, terms, jnp.where(lane < 2 * N_SPLIT, 1.0, 0.0))
    return cols.astype(BF16)


def _fox_kernel(q_ref, k_ref, vt_ref, c_ref, wg_ref, wu_ref, o_ref, wg_out, wu_out,
                kx_ref, st_ref, mx_ref, m_ref, acc_ref):
    head = pl.program_id(0) % N_HEADS
    qi = pl.program_id(1)

    step = pl.program_id(0) * pl.num_programs(1) + qi

    @pl.when(step < FFN_SLABS)
    def _():
        wg_out[...] = wg_ref[...].astype(BF16)

    @pl.when((step >= FFN_SLABS) & (step < 2 * FFN_SLABS))
    def _():
        wu_out[...] = wu_ref[...].astype(BF16)

    ones_tile = (lax.broadcasted_iota(jnp.int32, (BF16_ROWS, FOX_TK), 0) == 0).astype(BF16)

    @pl.when(qi == 0)
    def _():
        def build(i, carry):
            rows = pl.ds(pl.multiple_of(i * FOX_TK, FOX_TK), FOX_TK)
            kx_ref[rows, :] = _key_gate_columns(c_ref[rows, :], head)
            return carry

        lax.fori_loop(0, SEQ // FOX_TK, build, 0)

    q_rows = pl.ds(pl.multiple_of(qi * FOX_TQ, FOX_TQ), FOX_TQ)
    q_aug = jnp.concatenate([q_ref[...], _query_gate_columns(kx_ref[q_rows, :])], axis=-1)

    def key_rows(kb):
        return pl.ds(pl.multiple_of(kb * FOX_TK, FOX_TK), FOX_TK)

    def scores(kb, slot, q_lo=0):
        rows = key_rows(kb)
        k_aug = jnp.concatenate([k_ref[rows, :], kx_ref[rows, :]], axis=-1)
        st = lax.dot_general(k_aug, q_aug[q_lo:, :], (((1,), (1,)), ((), ())),
                             preferred_element_type=F32)
        st_ref[slot, :, q_lo:] = st
        mx_ref[slot, :, q_lo:] = jnp.max(st, axis=0, keepdims=True)

    def accumulate(kb, slot, q_lo=0, key_lo=None):
        st = st_ref[slot, :, q_lo:]
        if key_lo is None:
            st_max = mx_ref[slot, :, q_lo:]
        else:
            key = lax.broadcasted_iota(jnp.int32, st.shape, 0) + key_lo
            qry = lax.broadcasted_iota(jnp.int32, st.shape, 1) + q_lo
            st = jnp.where(key <= qry, st, NEG_BIG)
            st_max = jnp.max(st, axis=0, keepdims=True)
        m = m_ref[:, q_lo:]
        m_new = jnp.maximum(m, st_max)
        p = jnp.exp2(st - m_new).astype(BF16)
        vt = jnp.concatenate([vt_ref[:, key_rows(kb)], ones_tile], axis=0)
        pv = jnp.dot(vt, p, preferred_element_type=F32)
        acc_ref[:, q_lo:] = jnp.exp2(m - m_new) * acc_ref[:, q_lo:] + pv
        m_ref[:, q_lo:] = m_new

    m_ref[...] = jnp.full(m_ref.shape, NEG_BIG, F32)
    acc_ref[...] = jnp.zeros_like(acc_ref)
    scores(0, 0)

    def block(j, carry):
        for t in range(FOX_DIAG):
            tile = FOX_DIAG * j + t
            scores(tile + 1, (t + 1) % 2)
            accumulate(tile, t % 2)
        return carry

    first = qi * FOX_DIAG
    lax.fori_loop(0, qi, block, 0)
    for d in range(FOX_DIAG):
        if d + 1 < FOX_DIAG:
            scores(first + d + 1, (d + 1) % 2, q_lo=(d + 1) * FOX_TK)
        accumulate(first + d, d % 2, q_lo=d * FOX_TK, key_lo=d * FOX_TK)

    out_t = acc_ref[0:HEAD_DIM, :] / acc_ref[HEAD_DIM:HEAD_DIM + 1, :]
    o_ref[...] = out_t.T.astype(BF16)


def _fox_attn(qk, vt, c_gate, w_gate, w_up):
    assert FOX_TQ % (2 * FOX_TK) == 0
    q_tiles = SEQ // FOX_TQ
    assert BATCH * N_HEADS * q_tiles >= 2 * FFN_SLABS

    def slab(bh, qi, first):
        return jnp.clip(bh * q_tiles + qi - first, 0, FFN_SLABS - 1)

    def slab_in(first):
        return pl.BlockSpec((FFN_SLAB_ROWS, FFN_TF), lambda bh, qi: (
            slab(bh, qi, first) % SLABS_PER_TILE, slab(bh, qi, first) // SLABS_PER_TILE))

    def slab_out(first):
        return pl.BlockSpec((None, FFN_SLAB_ROWS, FFN_TF), lambda bh, qi: (
            slab(bh, qi, first) // SLABS_PER_TILE, slab(bh, qi, first) % SLABS_PER_TILE, 0))

    tiles = jax.ShapeDtypeStruct((FFN_STEPS, D_MODEL, FFN_TF), BF16)
    return pl.pallas_call(
        _fox_kernel,
        grid=(BATCH * N_HEADS, q_tiles),
        in_specs=[
            pl.BlockSpec((None, None, None, FOX_TQ, HEAD_DIM),
                         lambda bh, qi: (QI, bh // N_HEADS, bh % N_HEADS, qi, 0)),
            pl.BlockSpec((None, None, None, SEQ, HEAD_DIM),
                         lambda bh, qi: (KI, bh // N_HEADS, bh % N_HEADS, 0, 0)),
            pl.BlockSpec((None, None, HEAD_DIM, SEQ),
                         lambda bh, qi: (bh // N_HEADS, bh % N_HEADS, 0, 0)),
            pl.BlockSpec((None, SEQ, LANES), lambda bh, qi: (bh // N_HEADS, 0, 0)),
            slab_in(0),
            slab_in(FFN_SLABS),
        ],
        out_specs=[
            pl.BlockSpec((FOX_TQ, HEAD_DIM),
                         lambda bh, qi: ((bh // N_HEADS) * q_tiles + qi, bh % N_HEADS)),
            slab_out(0),
            slab_out(FFN_SLABS),
        ],
        out_shape=[jax.ShapeDtypeStruct((N_TOK, D_ATT), BF16), tiles, tiles],
        scratch_shapes=[
            pltpu.VMEM((SEQ, LANES), BF16),
            pltpu.VMEM((2, FOX_TK, FOX_TQ), F32),
            pltpu.VMEM((2, 1, FOX_TQ), F32),
            pltpu.VMEM((1, FOX_TQ), F32),
            pltpu.VMEM((FOX_VT_ROWS, FOX_TQ), F32),
        ],
        compiler_params=pltpu.CompilerParams(
            dimension_semantics=("arbitrary", "arbitrary"), vmem_limit_bytes=VMEM_ATTENTION),
        name="fox_attn",
    )(qk, qk, vt, c_gate, w_gate, w_up)


DIL_GROUP = 128
DIL_SUPER = DIL_GROUP * max(d for _, d in DIL_PATTERNS)
N_PAT = len(DIL_PATTERNS)
assert all(w == DIL_GROUP * d and DIL_SUPER % w == 0 for w, d in DIL_PATTERNS)


def _dil_group_order(d):
    cycle = DIL_RES // d
    run = DIL_GROUP // cycle
    i = np.arange(DIL_GROUP)
    return cycle * (i % run) + i // run


def _dil_band_bias():
    tables = []
    for _, d in DIL_PATTERNS:
        order = _dil_group_order(d)
        k = order[:, None]
        j = np.concatenate([order, DIL_GROUP + order])[None, :]
        tables.append(np.where((j >= k) & (j <= k + DIL_GROUP), 0.0, NEG_BIG))
    return jnp.asarray(np.stack(tables), F32)


def _dil_chunks(d, u, c):
    cycle = DIL_RES // d
    run = DIL_GROUP // cycle
    t_rel = u * DIL_GROUP * d
    out = []
    for b in range(cycle):
        for lo in range(0, run, ROWS_PER_RES):
            unit, off = divmod(t_rel + DIL_RES * lo, DIL_UNIT)
            out.append((unit * DIL_UNIT + ROWS_PER_RES * _res_slot(c + d * b) + off // DIL_RES,
                        min(run - lo, ROWS_PER_RES)))
    return out


def _dil_kernel(q_ref, k_ref, v_ref, band_ref, wd_ref, ga_ref, gb_ref, wo_ref,
                o_ref, wd_out, gates_out, wo_out, num_ref, den_ref, mx_ref, nat_ref):
    sb = pl.program_id(1)

    step = pl.program_id(0) * pl.num_programs(1) + sb

    @pl.when(step < FFN_SLABS)
    def _():
        wd_out[...] = wd_ref[...].astype(BF16)

    @pl.when((step >= FFN_SLABS) & (step < FFN_SLABS + N_GATE_TILES))
    def _():
        gates_out[...] = jnp.concatenate([ga_ref[GATE_ROW_OFF:, :], gb_ref[...]],
                                         axis=0).astype(BF16)

    @pl.when((step >= FFN_SLABS + N_GATE_TILES) & (step < FFN_SLABS + N_GATE_TILES + MRG_STEPS))
    def _():
        wo_out[...] = wo_ref[...].astype(BF16)

    base = sb * DIL_SUPER
    col = lax.broadcasted_iota(jnp.int32, (DIL_GROUP, 2 * DIL_GROUP), 1)
    ones = jnp.ones((2 * DIL_GROUP, HEAD_DIM), BF16)

    def gather(ref, pieces, dynamic):
        parts = []
        for start, n in pieces:
            if dynamic:
                row = base + start
                row = jnp.maximum(row, 0) if start < 0 else row
                parts.append(ref[pl.ds(pl.multiple_of(row, F32_ROWS), n), :])
            else:
                parts.append(ref[start:start + n, :])
        return jnp.concatenate(parts, axis=0)

    for g, (_, d) in enumerate(DIL_PATTERNS):
        band = band_ref[g]
        band_first = jnp.where((sb == 0) & (col < DIL_GROUP), NEG_BIG, band)
        for u in range(DIL_SUPER // (DIL_GROUP * d)):
            bias = band_first if u == 0 else band
            for c in range(d):
                own = _dil_chunks(d, u, c)
                window = _dil_chunks(d, u - 1, c) + own
                q = gather(q_ref, own, False).astype(BF16)
                k = gather(k_ref, window, True).astype(BF16)
                v = gather(v_ref, window, True).astype(BF16)
                s = lax.dot_general(q, k, NT_DIMS, preferred_element_type=F32) + bias
                m = jnp.max(s, axis=-1, keepdims=True)
                p = jnp.exp2(s - m).astype(BF16)
                r = jnp.dot(p, jnp.concatenate([v, ones], axis=-1), preferred_element_type=F32)
                m_rep = jnp.broadcast_to(m, (DIL_GROUP, HEAD_DIM))
                at = 0
                for start, n in own:
                    num_ref[g, start:start + n, :] = r[at:at + n, :HEAD_DIM]
                    den_ref[g, start:start + n, :] = r[at:at + n, HEAD_DIM:]
                    mx_ref[g, start:start + n, :] = m_rep[at:at + n, :]
                    at += n

    def combine(i, carry):
        r = pl.ds(pl.multiple_of(i * ROWS_PER_RES, ROWS_PER_RES), ROWS_PER_RES)
        mx = [mx_ref[g, r, :] for g in range(N_PAT)]
        top = functools.reduce(jnp.maximum, mx)
        wgt = [jnp.exp2(t - top) for t in mx]
        num = functools.reduce(jnp.add, [wgt[g] * num_ref[g, r, :] for g in range(N_PAT)])
        den = functools.reduce(jnp.add, [wgt[g] * den_ref[g, r, :] for g in range(N_PAT)])
        unit, res = i // DIL_RES, _res_slot(i % DIL_RES)
        nat_ref[pl.ds(unit * DIL_UNIT + res, ROWS_PER_RES, stride=DIL_RES), :] = num / den
        return carry

    lax.fori_loop(0, DIL_SUPER // ROWS_PER_RES, combine, 0)
    o_ref[...] = nat_ref[...].astype(BF16)


def _dil_attn(qkv, band, w_down, w_t, w_out):
    steps = SEQ // DIL_SUPER
    assert BATCH * N_HEADS * steps >= FFN_SLABS + N_GATE_TILES + MRG_STEPS
    assert GATE_ROW_OFF == F32_ROWS and C_QKV % MRG_TJ == 0
    down_rows = D_FF // FFN_SLABS
    down_slab = pl.BlockSpec((down_rows, D_MODEL),
                             lambda bh, sb: (jnp.minimum(bh * steps + sb, FFN_SLABS - 1), 0))

    def gate_tile(bh, sb):
        return jnp.clip(bh * steps + sb - FFN_SLABS, 0, N_GATE_TILES - 1)

    def out_slab(bh, sb):
        return jnp.clip(bh * steps + sb - FFN_SLABS - N_GATE_TILES, 0, MRG_STEPS - 1)

    wo_slab = pl.BlockSpec((MRG_TJ, D_MODEL), lambda bh, sb: (out_slab(bh, sb), 0))
    per_pattern = pltpu.VMEM((N_PAT, DIL_SUPER, HEAD_DIM), F32)
    return pl.pallas_call(
        _dil_kernel,
        grid=(BATCH * N_HEADS, steps),
        in_specs=[
            pl.BlockSpec((None, None, None, DIL_SUPER, HEAD_DIM),
                         lambda bh, sb: (QI, bh // N_HEADS, bh % N_HEADS, sb, 0)),
            pl.BlockSpec((None, None, None, SEQ, HEAD_DIM),
                         lambda bh, sb: (KI, bh // N_HEADS, bh % N_HEADS, 0, 0)),
            pl.BlockSpec((None, None, None, SEQ, HEAD_DIM),
                         lambda bh, sb: (VI, bh // N_HEADS, bh % N_HEADS, 0, 0)),
            pl.BlockSpec((N_PAT, DIL_GROUP, 2 * DIL_GROUP), lambda bh, sb: (0, 0, 0)),
            down_slab,
            pl.BlockSpec((MRG_TJ, D_MODEL), lambda bh, sb: (C_QKV // MRG_TJ + gate_tile(bh, sb), 0)),
            pl.BlockSpec((GATE_ROW_OFF, D_MODEL), lambda bh, sb: (
                (C_QKV + MRG_TJ * (gate_tile(bh, sb) + 1)) // GATE_ROW_OFF, 0)),
            wo_slab,
        ],
        out_specs=[
            pl.BlockSpec((DIL_SUPER, HEAD_DIM),
                         lambda bh, sb: ((bh // N_HEADS) * steps + sb, bh % N_HEADS)),
            down_slab,
            pl.BlockSpec((None, MRG_TJ, D_MODEL), lambda bh, sb: (gate_tile(bh, sb), 0, 0)),
            wo_slab,
        ],
        out_shape=[jax.ShapeDtypeStruct((N_TOK, D_ATT), BF16),
                   jax.ShapeDtypeStruct((D_FF, D_MODEL), BF16),
                   jax.ShapeDtypeStruct((N_GATE_TILES, MRG_TJ, D_MODEL), BF16),
                   jax.ShapeDtypeStruct((D_MODEL, D_MODEL), BF16)],
        scratch_shapes=[per_pattern, per_pattern, per_pattern,
                        pltpu.VMEM((DIL_SUPER, HEAD_DIM), F32)],
        compiler_params=pltpu.CompilerParams(
            dimension_semantics=("arbitrary", "arbitrary"), vmem_limit_bytes=VMEM_ATTENTION),
        name="dil_attn",
    )(qkv, qkv, qkv, band, w_down, w_t, w_t, w_out)


MRG_TM = 1024
MRG_TJ = 256


MRG_STEPS = D_MODEL // MRG_TJ
N_GATE_TILES = 2 * MRG_STEPS


def _merge_kernel(x_ref, gain_ref, yd_ref, yf_ref, wgd_ref, wgf_ref, bgd_ref, bgf_ref,
                  wpd_ref, wpf_ref, wo_ref, wo_last_ref, o_ref, h_ref, m_ref):
    j = pl.program_id(1)
    last = MRG_STEPS - 1

    def merge_tile():
        h = h_ref[...]
        gd = lax.dot_general(h, wgd_ref[...], NT_DIMS, preferred_element_type=F32) + bgd_ref[...]
        gf = lax.dot_general(h, wgf_ref[...], NT_DIMS, preferred_element_type=F32) + bgf_ref[...]
        pd = jnp.dot(yd_ref[...], wpd_ref[...], preferred_element_type=F32)
        pf = jnp.dot(yf_ref[...], wpf_ref[...], preferred_element_type=F32)
        m_ref[j % 2] = (jax.nn.sigmoid(gd) * pd + jax.nn.sigmoid(gf) * pf).astype(BF16)

    def project():
        o_ref[...] += jnp.dot(m_ref[(j + 1) % 2], wo_ref[...], preferred_element_type=F32)

    @pl.when(j == 0)
    def _():
        h_ref[...] = _rmsnorm(x_ref[...], gain_ref[...]).astype(BF16)
        o_ref[...] = jnp.zeros_like(o_ref)
        merge_tile()

    @pl.when((j > 0) & (j < last))
    def _():
        project()
        merge_tile()

    @pl.when(j == last)
    def _():
        project()
        merge_tile()
        o_ref[...] += jnp.dot(m_ref[last % 2], wo_last_ref[...], preferred_element_type=F32)
        o_ref[...] = x_ref[...] + o_ref[...]


def _merge(x, gain, yd, yf, w_gates, bgd, bgf, wpd, wpf, wo):
    grid = (N_TOK // MRG_TM, MRG_STEPS)
    last = MRG_STEPS - 1
    return pl.pallas_call(
        _merge_kernel,
        grid=grid,
        in_specs=[
            pl.BlockSpec((MRG_TM, D_MODEL), lambda i, j: (i, 0)),
            pl.BlockSpec((1, D_MODEL), lambda i, j: (0, 0)),
            pl.BlockSpec((MRG_TM, D_ATT), lambda i, j: (i, 0)),
            pl.BlockSpec((MRG_TM, D_ATT), lambda i, j: (i, 0)),
            pl.BlockSpec((None, MRG_TJ, D_MODEL), lambda i, j: (j, 0, 0)),
            pl.BlockSpec((None, MRG_TJ, D_MODEL), lambda i, j: (MRG_STEPS + j, 0, 0)),
            pl.BlockSpec((1, MRG_TJ), lambda i, j: (0, j)),
            pl.BlockSpec((1, MRG_TJ), lambda i, j: (0, j)),
            pl.BlockSpec((None, D_ATT, MRG_TJ), lambda i, j: (j, 0, 0)),
            pl.BlockSpec((None, D_ATT, MRG_TJ), lambda i, j: (j, 0, 0)),
            pl.BlockSpec((MRG_TJ, D_MODEL), lambda i, j: (jnp.maximum(j - 1, 0), 0)),
            pl.BlockSpec((MRG_TJ, D_MODEL), lambda i, j: (last, 0), pipeline_mode=pl.Buffered(1)),
        ],
        out_specs=pl.BlockSpec((MRG_TM, D_MODEL), lambda i, j: (i, 0)),
        out_shape=jax.ShapeDtypeStruct((N_TOK, D_MODEL), F32),
        scratch_shapes=[pltpu.VMEM((MRG_TM, D_MODEL), BF16),
                        pltpu.VMEM((2, MRG_TM, MRG_TJ), BF16)],
        compiler_params=pltpu.CompilerParams(
            dimension_semantics=("parallel", "arbitrary"), vmem_limit_bytes=VMEM_ROW_TILE_KERNELS),
        name="merge_proj",
    )(x, gain, yd, yf, w_gates, w_gates, bgd, bgf, wpd, wpf, wo, wo)


def _rope_tables():
    pos = np.arange(SEQ, dtype=np.float64)
    inv_freq = ROPE_THETA ** (-np.arange(0, ROPE_DIM, 2, dtype=np.float64) / ROPE_DIM)
    ang = pos[:, None] * inv_freq[None, :]
    cos, sin = np.cos(ang), np.sin(ang)
    gap = ROPE_PARTNER - ROPE_HALF
    tail = HEAD_DIM - ROPE_PARTNER - ROPE_HALF
    one, zero = (lambda n: np.ones((SEQ, n))), (lambda n: np.zeros((SEQ, n)))
    cos_full = np.concatenate([cos, one(gap), cos, one(tail)], axis=-1)
    sin_signed = np.concatenate([-sin, zero(gap), sin, zero(tail)], axis=-1)
    assert cos_full.shape == (SEQ, HEAD_DIM)

    def residue_major(t):
        t = t.reshape(SEQ // DIL_UNIT, ROWS_PER_RES, DIL_RES, HEAD_DIM).transpose(0, 2, 1, 3)
        t = t[:, [_res_slot(slot) for slot in range(DIL_RES)]]
        return jnp.asarray(t.reshape(SEQ, HEAD_DIM), F32)

    return residue_major(cos_full), residue_major(sin_signed)


PREP_ROWS = 512
C_QKV = 2 * N_MIX_QKV * D_ATT
C_GATES = C_QKV + N_HEADS
_PREP_PARAMS = pltpu.CompilerParams(dimension_semantics=("parallel",),
                                    vmem_limit_bytes=VMEM_WEIGHT_PREP)


def _cast_kernel(w_ref, o_ref):
    o_ref[...] = w_ref[...].astype(BF16)


def _cast_rows(w):
    rows, n_cols = w.shape
    return pl.pallas_call(
        _cast_kernel,
        grid=(rows // PREP_ROWS,),
        in_specs=[pl.BlockSpec((PREP_ROWS, n_cols), lambda i: (i, 0))],
        out_specs=pl.BlockSpec((PREP_ROWS, n_cols), lambda i: (i, 0)),
        out_shape=jax.ShapeDtypeStruct(w.shape, BF16),
        compiler_params=_PREP_PARAMS,
        name="cast_rows",
    )(w)


def _cast_column_tiles(w, tile):
    rows, n_cols = w.shape
    return pl.pallas_call(
        _cast_kernel,
        grid=(n_cols // tile,),
        in_specs=[pl.BlockSpec((rows, tile), lambda j: (0, j))],
        out_specs=pl.BlockSpec((None, rows, tile), lambda j: (j, 0, 0)),
        out_shape=jax.ShapeDtypeStruct((n_cols // tile, rows, tile), BF16),
        compiler_params=_PREP_PARAMS,
        name="cast_column_tiles",
    )(w)


def _qkv_weight_kernel(w_ref, o_ref):
    j = pl.program_id(0)
    rotary = (j % 2 == 0) & (j < 2 * (N_MIX_QKV - 1))

    @pl.when(rotary)
    def _():
        for hh in range(N_HEADS):
            base = hh * HEAD_DIM
            for dst, src, n in ((0, 0, ROPE_HALF),
                                (ROPE_HALF, ROPE_PARTNER, ROPE_HALF),
                                (ROPE_DIM, ROPE_DIM, ROPE_PARTNER - ROPE_DIM),
                                (ROPE_PARTNER, ROPE_HALF, ROPE_HALF),
                                (ROPE_PARTNER + ROPE_HALF, ROPE_PARTNER + ROPE_HALF,
                                 HEAD_DIM - ROPE_PARTNER - ROPE_HALF)):
                o_ref[base + dst:base + dst + n, :] = w_ref[base + src:base + src + n, :].astype(BF16)

    @pl.when(jnp.logical_not(rotary))
    def _():
        o_ref[...] = w_ref[...].astype(BF16)


def _qkv_weights(w_t):
    return pl.pallas_call(
        _qkv_weight_kernel,
        grid=(N_QKV_TILES,),
        in_specs=[pl.BlockSpec((D_ATT, D_MODEL), lambda j: ((j % 2) * N_MIX_QKV + j // 2, 0))],
        out_specs=pl.BlockSpec((None, D_ATT, D_MODEL), lambda j: (j, 0, 0)),
        out_shape=jax.ShapeDtypeStruct((N_QKV_TILES, D_ATT, D_MODEL), BF16),
        compiler_params=_PREP_PARAMS,
        name="qkv_weights",
    )(w_t)


F32_ROWS = 8
GATE_ROW_OFF = C_GATES % MRG_TJ


def _forget_weight_kernel(f_ref, wf_ref):
    pad = jnp.zeros((LANES - N_HEADS, D_MODEL), F32)
    wf_ref[...] = jnp.concatenate([f_ref[...], pad], axis=0).astype(BF16)


def _forget_weights(w_t):
    assert N_HEADS == F32_ROWS and C_QKV % N_HEADS == 0
    return pl.pallas_call(
        _forget_weight_kernel,
        grid=(1,),
        in_specs=[pl.BlockSpec((N_HEADS, D_MODEL), lambda i: (C_QKV // N_HEADS, 0))],
        out_specs=pl.BlockSpec((LANES, D_MODEL), lambda i: (0, 0)),
        out_shape=jax.ShapeDtypeStruct((LANES, D_MODEL), BF16),
        name="forget_weights",
    )(w_t)


def kernel(x, ffn1_norm, ffn1_w_gate, ffn1_w_up, ffn1_w_down, mix_norm, w_in, b_forget, b_gate_dil, b_gate_fox, w_proj_dil, w_proj_fox, w_out, ffn2_norm, ffn2_w_gate, ffn2_w_up, ffn2_w_down, final_norm):
    assert x.shape == (BATCH, SEQ, D_MODEL) and x.dtype == F32
    assert w_in.shape == (DEPTH, D_MODEL, C_GATES + 2 * D_MODEL)
    cos_full, sin_signed = _rope_tables()
    dil_band = _dil_band_bias()
    fgain = final_norm.reshape(1, D_MODEL)

    xt = x.reshape(N_TOK, D_MODEL)
    for l in range(DEPTH):
        last = l == DEPTH - 1
        w_t = jnp.swapaxes(w_in[l], 0, 1)
        w_qkv = _qkv_weights(w_t)
        w_f = _forget_weights(w_t)
        b_f = jnp.pad(b_forget[l], (0, LANES - N_HEADS)).reshape(1, LANES)

        xt = _ffn(xt, ffn1_norm[l].reshape(1, D_MODEL), _cast_column_tiles(ffn1_w_gate[l], FFN_TF),
                  _cast_column_tiles(ffn1_w_up[l], FFN_TF), _cast_rows(ffn1_w_down[l]), fgain, False)

        qkv_dil, qk_fox, vt_fox, logf = _qkv_proj(xt, mix_norm[l].reshape(1, D_MODEL), w_qkv, w_f,
                                                  b_f, cos_full, sin_signed)
        c_gate = _cumsum(logf.reshape(BATCH, SEQ, LANES))
        y_fox, w_gate2, w_up2 = _fox_attn(qk_fox, vt_fox, c_gate, ffn2_w_gate[l], ffn2_w_up[l])
        y_dil, w_down2, w_gates, w_out_b = _dil_attn(qkv_dil, dil_band, ffn2_w_down[l], w_t,
                                                     w_out[l])

        xt = _merge(xt, mix_norm[l].reshape(1, D_MODEL), y_dil, y_fox, w_gates,
                    b_gate_dil[l].reshape(1, D_MODEL), b_gate_fox[l].reshape(1, D_MODEL),
                    _cast_column_tiles(w_proj_dil[l], MRG_TJ),
                    _cast_column_tiles(w_proj_fox[l], MRG_TJ), w_out_b)

        xt = _ffn(xt, ffn2_norm[l].reshape(1, D_MODEL), w_gate2, w_up2, w_down2, fgain, last)
    return xt.reshape(BATCH, SEQ, D_MODEL)
```

```python
import functools

import numpy as np
import jax
import jax.numpy as jnp
from jax import lax
from jax.experimental import pallas as pl
from jax.experimental.pallas import tpu as pltpu

D_MODEL = 2048
BATCH = 2
SEQ = 8192
DEPTH = 1
HEAD_DIM = 128
N_HEADS = 8
D_ATT = N_HEADS * HEAD_DIM
DIL_PATTERNS = ((128, 1), (512, 4), (2048, 16))
MAX_WINDOW = 2048
ROPE_THETA = 500000.0
ROPE_DIM = HEAD_DIM // 4
D_FF = 5632
NORM_EPS = 1e-6
N_TOK = BATCH * SEQ
LANES = 128

NEG_BIG = -1e30
LOG2E = 1.4426950408889634
MIB = 1024 * 1024

V7X_VMEM_BYTES = 64 * MIB
VMEM_ROW_TILE_KERNELS = V7X_VMEM_BYTES - 3 * MIB
VMEM_ATTENTION = 56 * MIB
VMEM_WEIGHT_PREP = 40 * MIB

F32 = jnp.float32
BF16 = jnp.bfloat16


def _rmsnorm(x, gain):
    ms = jnp.mean(x * x, axis=-1, keepdims=True)
    return x * lax.rsqrt(ms + NORM_EPS) * gain


FFN_TM = 1024
FFN_TF = 512
FFN_STEPS = D_FF // FFN_TF
SLABS_PER_TILE = 2
FFN_SLABS = SLABS_PER_TILE * FFN_STEPS
FFN_SLAB_ROWS = D_MODEL // SLABS_PER_TILE


def _ffn_kernel(x_ref, gain_ref, wg_ref, wu_ref, wd_ref, wd_last_ref, fgain_ref, o_ref, h_ref, a_ref,
                *, final_norm):
    j = pl.program_id(1)
    last = FFN_STEPS - 1

    def gate_up():
        h = h_ref[...]
        g = jnp.dot(h, wg_ref[...], preferred_element_type=F32)
        u = jnp.dot(h, wu_ref[...], preferred_element_type=F32)
        a_ref[j % 2] = ((g * jax.nn.sigmoid(g)) * u).astype(BF16)

    def down():
        o_ref[...] += jnp.dot(a_ref[(j + 1) % 2], wd_ref[...], preferred_element_type=F32)

    @pl.when(j == 0)
    def _():
        h_ref[...] = _rmsnorm(x_ref[...], gain_ref[...]).astype(BF16)
        o_ref[...] = jnp.zeros_like(o_ref)
        gate_up()

    @pl.when((j > 0) & (j < last))
    def _():
        down()
        gate_up()

    @pl.when(j == last)
    def _():
        down()
        gate_up()
        o_ref[...] += jnp.dot(a_ref[last % 2], wd_last_ref[...], preferred_element_type=F32)
        y = x_ref[...] + 0.5 * o_ref[...]
        if final_norm:
            y = _rmsnorm(y, fgain_ref[...])
        o_ref[...] = y


def _ffn(x, gain, wg, wu, wd, fgain, final_norm):
    grid = (N_TOK // FFN_TM, FFN_STEPS)
    last = FFN_STEPS - 1
    return pl.pallas_call(
        functools.partial(_ffn_kernel, final_norm=final_norm),
        grid=grid,
        in_specs=[
            pl.BlockSpec((FFN_TM, D_MODEL), lambda i, j: (i, 0)),
            pl.BlockSpec((1, D_MODEL), lambda i, j: (0, 0)),
            pl.BlockSpec((None, D_MODEL, FFN_TF), lambda i, j: (j, 0, 0)),
            pl.BlockSpec((None, D_MODEL, FFN_TF), lambda i, j: (j, 0, 0)),
            pl.BlockSpec((FFN_TF, D_MODEL), lambda i, j: (jnp.maximum(j - 1, 0), 0)),
            pl.BlockSpec((FFN_TF, D_MODEL), lambda i, j: (last, 0), pipeline_mode=pl.Buffered(1)),
            pl.BlockSpec((1, D_MODEL), lambda i, j: (0, 0)),
        ],
        out_specs=pl.BlockSpec((FFN_TM, D_MODEL), lambda i, j: (i, 0)),
        out_shape=jax.ShapeDtypeStruct((N_TOK, D_MODEL), F32),
        scratch_shapes=[pltpu.VMEM((FFN_TM, D_MODEL), BF16),
                        pltpu.VMEM((2, FFN_TM, FFN_TF), BF16)],
        compiler_params=pltpu.CompilerParams(
            dimension_semantics=("parallel", "arbitrary"), vmem_limit_bytes=VMEM_ROW_TILE_KERNELS),
        name="ffn",
    )(x, gain, wg, wu, wd, wd, fgain)


QKV_TM = 1024
N_MIX_QKV = 3
QI, KI, VI = range(N_MIX_QKV)
N_QKV_TILES = 2 * N_MIX_QKV


ROPE_HALF = ROPE_DIM // 2
ROPE_PARTNER = LANES // 2

DIL_RES = max(d for _, d in DIL_PATTERNS)
DIL_RES_LO = 4
assert DIL_RES == DIL_RES_LO * DIL_RES_LO
DIL_UNIT = QKV_TM
ROWS_PER_RES = DIL_UNIT // DIL_RES


def _res_slot(res):
    return DIL_RES_LO * (res % DIL_RES_LO) + res // DIL_RES_LO


def _rope(x, cos, sin_signed):
    return x * cos + pltpu.roll(x, ROPE_PARTNER, 1) * sin_signed


NT_DIMS = (((1,), (1,)), ((), ()))


def _qkv_kernel(x_ref, gain_ref, w_ref, wf_ref, bf_ref, cos_ref, sin_ref,
                dil_ref, fox_ref, vt_ref, logf_ref, h_ref, r_ref, tmp_ref):
    n = pl.program_id(1)
    scale = (HEAD_DIM ** -0.5) * LOG2E
    is_q = n <= 2

    def matmul():
        r = lax.dot_general(h_ref[...], w_ref[...], NT_DIMS, preferred_element_type=F32)
        for hh in range(N_HEADS):
            r_ref[hh] = r[:, hh * HEAD_DIM:(hh + 1) * HEAD_DIM]

    def heads():
        return [r_ref[hh] for hh in range(N_HEADS)]

    def finish_dil(rotary, scaled):
        s = jnp.where(scaled, scale, 1.0)
        quarter = DIL_UNIT // DIL_RES_LO
        for hh in range(N_HEADS):
            for a in range(DIL_RES_LO):
                tmp_ref[a * quarter:(a + 1) * quarter, :] = r_ref[
                    hh, pl.ds(a, quarter, stride=DIL_RES_LO), :]
            for a in range(DIL_RES_LO):
                for b in range(DIL_RES_LO):
                    slot = DIL_RES_LO * a + b
                    rows = slice(slot * ROWS_PER_RES, (slot + 1) * ROWS_PER_RES)
                    t = tmp_ref[pl.ds(a * quarter + b, ROWS_PER_RES, stride=DIL_RES_LO), :]
                    cos = jnp.where(rotary, cos_ref[rows, :], 1.0) * s
                    sin = jnp.where(rotary, sin_ref[rows, :], 0.0) * s
                    dil_ref[hh, rows, :] = _rope(t, cos, sin)

    def finish_fox(scaled):
        s = jnp.where(scaled, scale, 1.0)
        for hh, t in enumerate(heads()):
            fox_ref[hh] = (t * s).astype(BF16)

    @pl.when(n == 0)
    def _():
        h = _rmsnorm(x_ref[...], gain_ref[...]).astype(BF16)
        h_ref[...] = h
        f = lax.dot_general(h, wf_ref[...], NT_DIMS, preferred_element_type=F32) + bf_ref[...]
        logf_ref[...] = jnp.minimum(f, 0.0) - jnp.log1p(jnp.exp(-jnp.abs(f)))
        matmul()

    @pl.when((n == 1) | (n == 3))
    def _():
        finish_dil(True, is_q)
        matmul()

    @pl.when((n == 2) | (n == 4))
    def _():
        finish_fox(is_q)
        matmul()

    @pl.when(n == N_QKV_TILES - 1)
    def _():
        finish_dil(False, False)
        matmul()
        for hh, t in enumerate(heads()):
            vt_ref[hh] = t.T.astype(BF16)


def _qkv_proj(x, gain, w_qkv, w_f, b_f, cos, sin_signed):
    s_tiles = SEQ // QKV_TM
    grid = (N_TOK // QKV_TM, N_QKV_TILES)

    def head_block(m, which):
        return (which, m // s_tiles, 0, m % s_tiles, 0)

    return pl.pallas_call(
        _qkv_kernel,
        grid=grid,
        in_specs=[
            pl.BlockSpec((QKV_TM, D_MODEL), lambda m, n: (m, 0)),
            pl.BlockSpec((1, D_MODEL), lambda m, n: (0, 0)),
            pl.BlockSpec((None, D_ATT, D_MODEL), lambda m, n: (n, 0, 0)),
            pl.BlockSpec((LANES, D_MODEL), lambda m, n: (0, 0)),
            pl.BlockSpec((1, LANES), lambda m, n: (0, 0)),
            pl.BlockSpec((QKV_TM, LANES), lambda m, n: (m % s_tiles, 0)),
            pl.BlockSpec((QKV_TM, LANES), lambda m, n: (m % s_tiles, 0)),
        ],
        out_specs=[
            pl.BlockSpec((None, None, N_HEADS, QKV_TM, HEAD_DIM),
                         lambda m, n: head_block(m, jnp.maximum(n - 1, 0) // 2)),
            pl.BlockSpec((None, None, N_HEADS, QKV_TM, HEAD_DIM),
                         lambda m, n: head_block(m, jnp.clip(n - 3, 0, 1))),
            pl.BlockSpec((None, N_HEADS, HEAD_DIM, QKV_TM),
                         lambda m, n: (m // s_tiles, 0, 0, m % s_tiles)),
            pl.BlockSpec((QKV_TM, LANES), lambda m, n: (m, 0)),
        ],
        out_shape=[
            jax.ShapeDtypeStruct((N_MIX_QKV, BATCH, N_HEADS, SEQ, HEAD_DIM), F32),
            jax.ShapeDtypeStruct((2, BATCH, N_HEADS, SEQ, HEAD_DIM), BF16),
            jax.ShapeDtypeStruct((BATCH, N_HEADS, HEAD_DIM, SEQ), BF16),
            jax.ShapeDtypeStruct((N_TOK, LANES), F32),
        ],
        scratch_shapes=[pltpu.VMEM((QKV_TM, D_MODEL), BF16),
                        pltpu.VMEM((N_HEADS, QKV_TM, HEAD_DIM), F32),
                        pltpu.VMEM((QKV_TM, HEAD_DIM), F32)],
        compiler_params=pltpu.CompilerParams(
            dimension_semantics=("parallel", "arbitrary"), vmem_limit_bytes=VMEM_ROW_TILE_KERNELS),
        name="qkv_proj",
    )(x, gain, w_qkv, w_f, b_f, cos, sin_signed)


CUM_TS = 1024
CUM_CHUNK = 128


def _cumsum_kernel(lf_ref, c_ref, carry_ref):
    @pl.when(pl.program_id(1) == 0)
    def _():
        carry_ref[...] = jnp.zeros_like(carry_ref)

    r = lax.broadcasted_iota(jnp.int32, (CUM_CHUNK, CUM_CHUNK), 0)
    c = lax.broadcasted_iota(jnp.int32, (CUM_CHUNK, CUM_CHUNK), 1)
    tri = (c <= r).astype(F32)
    carry = carry_ref[...]
    for i in range(CUM_TS // CUM_CHUNK):
        rows = slice(i * CUM_CHUNK, (i + 1) * CUM_CHUNK)
        cs = jnp.dot(tri, lf_ref[rows, :], preferred_element_type=F32,
                     precision=lax.Precision.HIGHEST) + carry
        c_ref[rows, :] = cs
        carry = cs[CUM_CHUNK - 1:CUM_CHUNK, :]
    carry_ref[...] = carry


def _cumsum(logf):
    return pl.pallas_call(
        _cumsum_kernel,
        grid=(BATCH, SEQ // CUM_TS),
        in_specs=[pl.BlockSpec((None, CUM_TS, LANES), lambda b, s: (b, s, 0))],
        out_specs=pl.BlockSpec((None, CUM_TS, LANES), lambda b, s: (b, s, 0)),
        out_shape=jax.ShapeDtypeStruct((BATCH, SEQ, LANES), F32),
        scratch_shapes=[pltpu.VMEM((1, LANES), F32)],
        compiler_params=pltpu.CompilerParams(dimension_semantics=("parallel", "arbitrary")),
        name="gate_cumsum",
    )(logf)


FOX_TQ = 2048
FOX_TK = 512
FOX_DIAG = FOX_TQ // FOX_TK
BF16_ROWS = 16
FOX_VT_ROWS = HEAD_DIM + BF16_ROWS
N_SPLIT = 3


def _key_gate_columns(c_tile, head):
    lane = lax.broadcasted_iota(jnp.int32, c_tile.shape, 1)
    rest = -jnp.sum(jnp.where(lane == head, c_tile, 0.0), axis=-1, keepdims=True) * LOG2E
    cols = jnp.where(lane < N_SPLIT, 1.0, 0.0)
    for t in range(N_SPLIT):
        term = rest.astype(BF16).astype(F32)
        cols = jnp.where(lane == N_SPLIT + t, term, cols)
        rest = rest - term
    return cols.astype(BF16)


def _query_gate_columns(key_cols):
    k = key_cols.astype(F32)
    lane = lax.broadcasted_iota(jnp.int32, k.shape, 1)
    terms = -pltpu.roll(k, LANES - N_SPLIT, 1)
    cols = jnp.where(lane < N_SPLIT, terms, jnp.where(lane < 2 * N_SPLIT, 1.0, 0.0))
    return cols.astype(BF16)


def _fox_kernel(q_ref, k_ref, vt_ref, c_ref, wg_ref, wu_ref, o_ref, wg_out, wu_out,
                kx_ref, st_ref, mx_ref, m_ref, acc_ref):
    head = pl.program_id(0) % N_HEADS
    qi = pl.program_id(1)

    step = pl.program_id(0) * pl.num_programs(1) + qi

    @pl.when(step < FFN_SLABS)
    def _():
        wg_out[...] = wg_ref[...].astype(BF16)

    @pl.when((step >= FFN_SLABS) & (step < 2 * FFN_SLABS))
    def _():
        wu_out[...] = wu_ref[...].astype(BF16)

    ones_tile = (lax.broadcasted_iota(jnp.int32, (BF16_ROWS, FOX_TK), 0) == 0).astype(BF16)

    @pl.when(qi == 0)
    def _():
        def build(i, carry):
            rows = pl.ds(pl.multiple_of(i * FOX_TK, FOX_TK), FOX_TK)
            kx_ref[rows, :] = _key_gate_columns(c_ref[rows, :], head)
            return carry

        lax.fori_loop(0, SEQ // FOX_TK, build, 0, unroll=2)

    q_rows = pl.ds(pl.multiple_of(qi * FOX_TQ, FOX_TQ), FOX_TQ)
    q_aug = jnp.concatenate([q_ref[...], _query_gate_columns(kx_ref[q_rows, :])], axis=-1)

    def key_rows(kb):
        return pl.ds(pl.multiple_of(kb * FOX_TK, FOX_TK), FOX_TK)

    def scores(kb, slot, q_lo=0):
        rows = key_rows(kb)
        k_aug = jnp.concatenate([k_ref[rows, :], kx_ref[rows, :]], axis=-1)
        st = lax.dot_general(k_aug, q_aug[q_lo:, :], (((1,), (1,)), ((), ())),
                             preferred_element_type=F32)
        st_ref[slot, :, q_lo:] = st
        mx_ref[slot, :, q_lo:] = jnp.max(st, axis=0, keepdims=True)

    def accumulate(kb, slot, q_lo=0, key_lo=None):
        st = st_ref[slot, :, q_lo:]
        if key_lo is None:
            st_max = mx_ref[slot, :, q_lo:]
        else:
            key = lax.broadcasted_iota(jnp.int32, st.shape, 0) + key_lo
            qry = lax.broadcasted_iota(jnp.int32, st.shape, 1) + q_lo
            st = jnp.where(key <= qry, st, NEG_BIG)
            st_max = jnp.max(st, axis=0, keepdims=True)
        m = m_ref[:, q_lo:]
        m_new = jnp.maximum(m, st_max)
        p = jnp.exp2(st - m_new).astype(BF16)
        vt = jnp.concatenate([vt_ref[:, key_rows(kb)], ones_tile], axis=0)
        pv = jnp.dot(vt, p, preferred_element_type=F32)
        acc_ref[:, q_lo:] = jnp.exp2(m - m_new) * acc_ref[:, q_lo:] + pv
        m_ref[:, q_lo:] = m_new

    m_ref[...] = jnp.full(m_ref.shape, NEG_BIG, F32)
    acc_ref[...] = jnp.zeros_like(acc_ref)
    scores(0, 0)

    def block(j, carry):
        for t in range(FOX_DIAG):
            tile = FOX_DIAG * j + t
            scores(tile + 1, (t + 1) % 2)
            accumulate(tile, t % 2)
        return carry

    first = qi * FOX_DIAG
    lax.fori_loop(0, qi, block, 0)
    for d in range(FOX_DIAG):
        if d + 1 < FOX_DIAG:
            scores(first + d + 1, (d + 1) % 2, q_lo=(d + 1) * FOX_TK)
        accumulate(first + d, d % 2, q_lo=d * FOX_TK, key_lo=d * FOX_TK)

    out_t = acc_ref[0:HEAD_DIM, :] / acc_ref[HEAD_DIM:HEAD_DIM + 1, :]
    o_ref[...] = out_t.T.astype(BF16)


def _fox_attn(qk, vt, c_gate, w_gate, w_up):
    assert FOX_TQ % (2 * FOX_TK) == 0
    q_tiles = SEQ // FOX_TQ
    assert BATCH * N_HEADS * q_tiles >= 2 * FFN_SLABS

    def slab(bh, qi, first):
        return jnp.clip(bh * q_tiles + qi - first, 0, FFN_SLABS - 1)

    def slab_in(first):
        return pl.BlockSpec((FFN_SLAB_ROWS, FFN_TF), lambda bh, qi: (
            slab(bh, qi, first) % SLABS_PER_TILE, slab(bh, qi, first) // SLABS_PER_TILE))

    def slab_out(first):
        return pl.BlockSpec((None, FFN_SLAB_ROWS, FFN_TF), lambda bh, qi: (
            slab(bh, qi, first) // SLABS_PER_TILE, slab(bh, qi, first) % SLABS_PER_TILE, 0))

    tiles = jax.ShapeDtypeStruct((FFN_STEPS, D_MODEL, FFN_TF), BF16)
    return pl.pallas_call(
        _fox_kernel,
        grid=(BATCH * N_HEADS, q_tiles),
        in_specs=[
            pl.BlockSpec((None, None, None, FOX_TQ, HEAD_DIM),
                         lambda bh, qi: (QI, bh // N_HEADS, bh % N_HEADS, qi, 0)),
            pl.BlockSpec((None, None, None, SEQ, HEAD_DIM),
                         lambda bh, qi: (KI, bh // N_HEADS, bh % N_HEADS, 0, 0)),
            pl.BlockSpec((None, None, HEAD_DIM, SEQ),
                         lambda bh, qi: (bh // N_HEADS, bh % N_HEADS, 0, 0)),
            pl.BlockSpec((None, SEQ, LANES), lambda bh, qi: (bh // N_HEADS, 0, 0)),
            slab_in(0),
            slab_in(FFN_SLABS),
        ],
        out_specs=[
            pl.BlockSpec((FOX_TQ, HEAD_DIM),
                         lambda bh, qi: ((bh // N_HEADS) * q_tiles + qi, bh % N_HEADS)),
            slab_out(0),
            slab_out(FFN_SLABS),
        ],
        out_shape=[jax.ShapeDtypeStruct((N_TOK, D_ATT), BF16), tiles, tiles],
        scratch_shapes=[
            pltpu.VMEM((SEQ, LANES), BF16),
            pltpu.VMEM((2, FOX_TK, FOX_TQ), F32),
            pltpu.VMEM((2, 1, FOX_TQ), F32),
            pltpu.VMEM((1, FOX_TQ), F32),
            pltpu.VMEM((FOX_VT_ROWS, FOX_TQ), F32),
        ],
        compiler_params=pltpu.CompilerParams(
            dimension_semantics=("arbitrary", "arbitrary"), vmem_limit_bytes=VMEM_ATTENTION),
        name="fox_attn",
    )(qk, qk, vt, c_gate, w_gate, w_up)


DIL_GROUP = 128
DIL_SUPER = DIL_GROUP * max(d for _, d in DIL_PATTERNS)
N_PAT = len(DIL_PATTERNS)
assert all(w == DIL_GROUP * d and DIL_SUPER % w == 0 for w, d in DIL_PATTERNS)


def _dil_group_order(d):
    cycle = DIL_RES // d
    run = DIL_GROUP // cycle
    i = np.arange(DIL_GROUP)
    return cycle * (i % run) + i // run


def _dil_band_bias():
    tables = []
    for _, d in DIL_PATTERNS:
        order = _dil_group_order(d)
        k = order[:, None]
        j = np.concatenate([order, DIL_GROUP + order])[None, :]
        tables.append(np.where((j >= k) & (j <= k + DIL_GROUP), 0.0, NEG_BIG))
    return jnp.asarray(np.stack(tables), F32)


def _dil_chunks(d, u, c):
    cycle = DIL_RES // d
    run = DIL_GROUP // cycle
    t_rel = u * DIL_GROUP * d
    out = []
    for b in range(cycle):
        for lo in range(0, run, ROWS_PER_RES):
            unit, off = divmod(t_rel + DIL_RES * lo, DIL_UNIT)
            out.append((unit * DIL_UNIT + ROWS_PER_RES * _res_slot(c + d * b) + off // DIL_RES,
                        min(run - lo, ROWS_PER_RES)))
    return out


def _dil_kernel(q_ref, k_ref, v_ref, band_ref, wd_ref, ga_ref, gb_ref, wo_ref,
                o_ref, wd_out, gates_out, wo_out, num_ref, den_ref, mx_ref, nat_ref):
    sb = pl.program_id(1)

    step = pl.program_id(0) * pl.num_programs(1) + sb

    @pl.when(step < FFN_SLABS)
    def _():
        wd_out[...] = wd_ref[...].astype(BF16)

    @pl.when((step >= FFN_SLABS) & (step < FFN_SLABS + N_GATE_TILES))
    def _():
        gates_out[...] = jnp.concatenate([ga_ref[GATE_ROW_OFF:, :], gb_ref[...]],
                                         axis=0).astype(BF16)

    @pl.when((step >= FFN_SLABS + N_GATE_TILES) & (step < FFN_SLABS + N_GATE_TILES + MRG_STEPS))
    def _():
        wo_out[...] = wo_ref[...].astype(BF16)

    base = sb * DIL_SUPER
    col = lax.broadcasted_iota(jnp.int32, (DIL_GROUP, 2 * DIL_GROUP), 1)
    ones = jnp.ones((2 * DIL_GROUP, HEAD_DIM), BF16)

    def gather(ref, pieces, dynamic):
        parts = []
        for start, n in pieces:
            if dynamic:
                row = base + start
                row = jnp.maximum(row, 0) if start < 0 else row
                parts.append(ref[pl.ds(pl.multiple_of(row, F32_ROWS), n), :])
            else:
                parts.append(ref[start:start + n, :])
        return jnp.concatenate(parts, axis=0)

    for g, (_, d) in enumerate(DIL_PATTERNS):
        band = band_ref[g]
        band_first = jnp.where((sb == 0) & (col < DIL_GROUP), NEG_BIG, band)
        for u in range(DIL_SUPER // (DIL_GROUP * d)):
            bias = band_first if u == 0 else band
            for c in range(d):
                own = _dil_chunks(d, u, c)
                window = _dil_chunks(d, u - 1, c) + own
                q = gather(q_ref, own, False).astype(BF16)
                k = gather(k_ref, window, True).astype(BF16)
                v = gather(v_ref, window, True).astype(BF16)
                s = lax.dot_general(q, k, NT_DIMS, preferred_element_type=F32) + bias
                m = jnp.max(s, axis=-1, keepdims=True)
                p = jnp.exp2(s - m).astype(BF16)
                r = jnp.dot(p, jnp.concatenate([v, ones], axis=-1), preferred_element_type=F32)
                m_rep = jnp.broadcast_to(m, (DIL_GROUP, HEAD_DIM))
                at = 0
                for start, n in own:
                    num_ref[g, start:start + n, :] = r[at:at + n, :HEAD_DIM]
                    den_ref[g, start:start + n, :] = r[at:at + n, HEAD_DIM:]
                    mx_ref[g, start:start + n, :] = m_rep[at:at + n, :]
                    at += n

    def combine(i, carry):
        r = pl.ds(pl.multiple_of(i * ROWS_PER_RES, ROWS_PER_RES), ROWS_PER_RES)
        mx = [mx_ref[g, r, :] for g in range(N_PAT)]
        top = functools.reduce(jnp.maximum, mx)
        wgt = [jnp.exp2(t - top) for t in mx]
        num = functools.reduce(jnp.add, [wgt[g] * num_ref[g, r, :] for g in range(N_PAT)])
        den = functools.reduce(jnp.add, [wgt[g] * den_ref[g, r, :] for g in range(N_PAT)])
        unit, res = i // DIL_RES, _res_slot(i % DIL_RES)
        nat_ref[pl.ds(unit * DIL_UNIT + res, ROWS_PER_RES, stride=DIL_RES), :] = num / den
        return carry

    lax.fori_loop(0, DIL_SUPER // ROWS_PER_RES, combine, 0, unroll=4)
    o_ref[...] = nat_ref[...].astype(BF16)


def _dil_attn(qkv, band, w_down, w_t, w_out):
    steps = SEQ // DIL_SUPER
    assert BATCH * N_HEADS * steps >= FFN_SLABS + N_GATE_TILES + MRG_STEPS
    assert GATE_ROW_OFF == F32_ROWS and C_QKV % MRG_TJ == 0
    down_rows = D_FF // FFN_SLABS
    down_slab = pl.BlockSpec((down_rows, D_MODEL),
                             lambda bh, sb: (jnp.minimum(bh * steps + sb, FFN_SLABS - 1), 0))

    def gate_tile(bh, sb):
        return jnp.clip(bh * steps + sb - FFN_SLABS, 0, N_GATE_TILES - 1)

    def out_slab(bh, sb):
        return jnp.clip(bh * steps + sb - FFN_SLABS - N_GATE_TILES, 0, MRG_STEPS - 1)

    wo_slab = pl.BlockSpec((MRG_TJ, D_MODEL), lambda bh, sb: (out_slab(bh, sb), 0))
    per_pattern = pltpu.VMEM((N_PAT, DIL_SUPER, HEAD_DIM), F32)
    return pl.pallas_call(
        _dil_kernel,
        grid=(BATCH * N_HEADS, steps),
        in_specs=[
            pl.BlockSpec((None, None, None, DIL_SUPER, HEAD_DIM),
                         lambda bh, sb: (QI, bh // N_HEADS, bh % N_HEADS, sb, 0)),
            pl.BlockSpec((None, None, None, SEQ, HEAD_DIM),
                         lambda bh, sb: (KI, bh // N_HEADS, bh % N_HEADS, 0, 0)),
            pl.BlockSpec((None, None, None, SEQ, HEAD_DIM),
                         lambda bh, sb: (VI, bh // N_HEADS, bh % N_HEADS, 0, 0)),
            pl.BlockSpec((N_PAT, DIL_GROUP, 2 * DIL_GROUP), lambda bh, sb: (0, 0, 0)),
            down_slab,
            pl.BlockSpec((MRG_TJ, D_MODEL), lambda bh, sb: (C_QKV // MRG_TJ + gate_tile(bh, sb), 0)),
            pl.BlockSpec((GATE_ROW_OFF, D_MODEL), lambda bh, sb: (
                (C_QKV + MRG_TJ * (gate_tile(bh, sb) + 1)) // GATE_ROW_OFF, 0)),
            wo_slab,
        ],
        out_specs=[
            pl.BlockSpec((DIL_SUPER, HEAD_DIM),
                         lambda bh, sb: ((bh // N_HEADS) * steps + sb, bh % N_HEADS)),
            down_slab,
            pl.BlockSpec((None, MRG_TJ, D_MODEL), lambda bh, sb: (gate_tile(bh, sb), 0, 0)),
            wo_slab,
        ],
        out_shape=[jax.ShapeDtypeStruct((N_TOK, D_ATT), BF16),
                   jax.ShapeDtypeStruct((D_FF, D_MODEL), BF16),
                   jax.ShapeDtypeStruct((N_GATE_TILES, MRG_TJ, D_MODEL), BF16),
                   jax.ShapeDtypeStruct((D_MODEL, D_MODEL), BF16)],
        scratch_shapes=[per_pattern, per_pattern, per_pattern,
                        pltpu.VMEM((DIL_SUPER, HEAD_DIM), F32)],
        compiler_params=pltpu.CompilerParams(
            dimension_semantics=("arbitrary", "arbitrary"), vmem_limit_bytes=VMEM_ATTENTION),
        name="dil_attn",
    )(qkv, qkv, qkv, band, w_down, w_t, w_t, w_out)


MRG_TM = 1024
MRG_TJ = 256


MRG_STEPS = D_MODEL // MRG_TJ
N_GATE_TILES = 2 * MRG_STEPS


def _merge_kernel(x_ref, gain_ref, yd_ref, yf_ref, wgd_ref, wgf_ref, bgd_ref, bgf_ref,
                  wpd_ref, wpf_ref, wo_ref, wo_last_ref, o_ref, h_ref, m_ref):
    j = pl.program_id(1)
    last = MRG_STEPS - 1

    def merge_tile():
        h = h_ref[...]
        gd = lax.dot_general(h, wgd_ref[...], NT_DIMS, preferred_element_type=F32) + bgd_ref[...]
        gf = lax.dot_general(h, wgf_ref[...], NT_DIMS, preferred_element_type=F32) + bgf_ref[...]
        pd = jnp.dot(yd_ref[...], wpd_ref[...], preferred_element_type=F32)
        pf = jnp.dot(yf_ref[...], wpf_ref[...], preferred_element_type=F32)
        m_ref[j % 2] = (jax.nn.sigmoid(gd) * pd + jax.nn.sigmoid(gf) * pf).astype(BF16)

    def project():
        o_ref[...] += jnp.dot(m_ref[(j + 1) % 2], wo_ref[...], preferred_element_type=F32)

    @pl.when(j == 0)
    def _():
        h_ref[...] = _rmsnorm(x_ref[...], gain_ref[...]).astype(BF16)
        o_ref[...] = jnp.zeros_like(o_ref)
        merge_tile()

    @pl.when((j > 0) & (j < last))
    def _():
        project()
        merge_tile()

    @pl.when(j == last)
    def _():
        project()
        merge_tile()
        o_ref[...] += jnp.dot(m_ref[last % 2], wo_last_ref[...], preferred_element_type=F32)
        o_ref[...] = x_ref[...] + o_ref[...]


def _merge(x, gain, yd, yf, w_gates, bgd, bgf, wpd, wpf, wo):
    grid = (N_TOK // MRG_TM, MRG_STEPS)
    last = MRG_STEPS - 1
    return pl.pallas_call(
        _merge_kernel,
        grid=grid,
        in_specs=[
            pl.BlockSpec((MRG_TM, D_MODEL), lambda i, j: (i, 0)),
            pl.BlockSpec((1, D_MODEL), lambda i, j: (0, 0)),
            pl.BlockSpec((MRG_TM, D_ATT), lambda i, j: (i, 0)),
            pl.BlockSpec((MRG_TM, D_ATT), lambda i, j: (i, 0)),
            pl.BlockSpec((None, MRG_TJ, D_MODEL), lambda i, j: (j, 0, 0)),
            pl.BlockSpec((None, MRG_TJ, D_MODEL), lambda i, j: (MRG_STEPS + j, 0, 0)),
            pl.BlockSpec((1, MRG_TJ), lambda i, j: (0, j)),
            pl.BlockSpec((1, MRG_TJ), lambda i, j: (0, j)),
            pl.BlockSpec((None, D_ATT, MRG_TJ), lambda i, j: (j, 0, 0)),
            pl.BlockSpec((None, D_ATT, MRG_TJ), lambda i, j: (j, 0, 0)),
            pl.BlockSpec((MRG_TJ, D_MODEL), lambda i, j: (jnp.maximum(j - 1, 0), 0)),
            pl.BlockSpec((MRG_TJ, D_MODEL), lambda i, j: (last, 0), pipeline_mode=pl.Buffered(1)),
        ],
        out_specs=pl.BlockSpec((MRG_TM, D_MODEL), lambda i, j: (i, 0)),
        out_shape=jax.ShapeDtypeStruct((N_TOK, D_MODEL), F32),
        scratch_shapes=[pltpu.VMEM((MRG_TM, D_MODEL), BF16),
                        pltpu.VMEM((2, MRG_TM, MRG_TJ), BF16)],
        compiler_params=pltpu.CompilerParams(
            dimension_semantics=("parallel", "arbitrary"), vmem_limit_bytes=VMEM_ROW_TILE_KERNELS),
        name="merge_proj",
    )(x, gain, yd, yf, w_gates, w_gates, bgd, bgf, wpd, wpf, wo, wo)


def _rope_tables():
    pos = np.arange(SEQ, dtype=np.float64)
    inv_freq = ROPE_THETA ** (-np.arange(0, ROPE_DIM, 2, dtype=np.float64) / ROPE_DIM)
    ang = pos[:, None] * inv_freq[None, :]
    cos, sin = np.cos(ang), np.sin(ang)
    gap = ROPE_PARTNER - ROPE_HALF
    tail = HEAD_DIM - ROPE_PARTNER - ROPE_HALF
    one, zero = (lambda n: np.ones((SEQ, n))), (lambda n: np.zeros((SEQ, n)))
    cos_full = np.concatenate([cos, one(gap), cos, one(tail)], axis=-1)
    sin_signed = np.concatenate([-sin, zero(gap), sin, zero(tail)], axis=-1)
    assert cos_full.shape == (SEQ, HEAD_DIM)

    def residue_major(t):
        t = t.reshape(SEQ // DIL_UNIT, ROWS_PER_RES, DIL_RES, HEAD_DIM).transpose(0, 2, 1, 3)
        t = t[:, [_res_slot(slot) for slot in range(DIL_RES)]]
        return jnp.asarray(t.reshape(SEQ, HEAD_DIM), F32)

    return residue_major(cos_full), residue_major(sin_signed)


PREP_ROWS = 512
C_QKV = 2 * N_MIX_QKV * D_ATT
C_GATES = C_QKV + N_HEADS
_PREP_PARAMS = pltpu.CompilerParams(dimension_semantics=("parallel",),
                                    vmem_limit_bytes=VMEM_WEIGHT_PREP)


def _cast_kernel(w_ref, o_ref):
    o_ref[...] = w_ref[...].astype(BF16)


def _cast_rows(w):
    rows, n_cols = w.shape
    return pl.pallas_call(
        _cast_kernel,
        grid=(rows // PREP_ROWS,),
        in_specs=[pl.BlockSpec((PREP_ROWS, n_cols), lambda i: (i, 0))],
        out_specs=pl.BlockSpec((PREP_ROWS, n_cols), lambda i: (i, 0)),
        out_shape=jax.ShapeDtypeStruct(w.shape, BF16),
        compiler_params=_PREP_PARAMS,
        name="cast_rows",
    )(w)


def _cast_column_tiles(w, tile):
    rows, n_cols = w.shape
    return pl.pallas_call(
        _cast_kernel,
        grid=(n_cols // tile,),
        in_specs=[pl.BlockSpec((rows, tile), lambda j: (0, j))],
        out_specs=pl.BlockSpec((None, rows, tile), lambda j: (j, 0, 0)),
        out_shape=jax.ShapeDtypeStruct((n_cols // tile, rows, tile), BF16),
        compiler_params=_PREP_PARAMS,
        name="cast_column_tiles",
    )(w)


def _qkv_weight_kernel(w_ref, o_ref):
    j = pl.program_id(0)
    rotary = (j % 2 == 0) & (j < 2 * (N_MIX_QKV - 1))

    @pl.when(rotary)
    def _():
        for hh in range(N_HEADS):
            base = hh * HEAD_DIM
            for dst, src, n in ((0, 0, ROPE_HALF),
                                (ROPE_HALF, ROPE_PARTNER, ROPE_HALF),
                                (ROPE_DIM, ROPE_DIM, ROPE_PARTNER - ROPE_DIM),
                                (ROPE_PARTNER, ROPE_HALF, ROPE_HALF),
                                (ROPE_PARTNER + ROPE_HALF, ROPE_PARTNER + ROPE_HALF,
                                 HEAD_DIM - ROPE_PARTNER - ROPE_HALF)):
                o_ref[base + dst:base + dst + n, :] = w_ref[base + src:base + src + n, :].astype(BF16)

    @pl.when(jnp.logical_not(rotary))
    def _():
        o_ref[...] = w_ref[...].astype(BF16)


def _qkv_weights(w_t):
    return pl.pallas_call(
        _qkv_weight_kernel,
        grid=(N_QKV_TILES,),
        in_specs=[pl.BlockSpec((D_ATT, D_MODEL), lambda j: ((j % 2) * N_MIX_QKV + j // 2, 0))],
        out_specs=pl.BlockSpec((None, D_ATT, D_MODEL), lambda j: (j, 0, 0)),
        out_shape=jax.ShapeDtypeStruct((N_QKV_TILES, D_ATT, D_MODEL), BF16),
        compiler_params=_PREP_PARAMS,
        name="qkv_weights",
    )(w_t)


F32_ROWS = 8
GATE_ROW_OFF = C_GATES % MRG_TJ


def _forget_weight_kernel(f_ref, wf_ref):
    pad = jnp.zeros((LANES - N_HEADS, D_MODEL), F32)
    wf_ref[...] = jnp.concatenate([f_ref[...], pad], axis=0).astype(BF16)


def _forget_weights(w_t):
    assert N_HEADS == F32_ROWS and C_QKV % N_HEADS == 0
    return pl.pallas_call(
        _forget_weight_kernel,
        grid=(1,),
        in_specs=[pl.BlockSpec((N_HEADS, D_MODEL), lambda i: (C_QKV // N_HEADS, 0))],
        out_specs=pl.BlockSpec((LANES, D_MODEL), lambda i: (0, 0)),
        out_shape=jax.ShapeDtypeStruct((LANES, D_MODEL), BF16),
        name="forget_weights",
    )(w_t)


def kernel(x, ffn1_norm, ffn1_w_gate, ffn1_w_up, ffn1_w_down, mix_norm, w_in, b_forget, b_gate_dil, b_gate_fox, w_proj_dil, w_proj_fox, w_out, ffn2_norm, ffn2_w_gate, ffn2_w_up, ffn2_w_down, final_norm):
    assert x.shape == (BATCH, SEQ, D_MODEL) and x.dtype == F32
    assert w_in.shape == (DEPTH, D_MODEL, C_GATES + 2 * D_MODEL)
    cos_full, sin_signed = _rope_tables()
    dil_band = _dil_band_bias()
    fgain = final_norm.reshape(1, D_MODEL)

    xt = x.reshape(N_TOK, D_MODEL)
    for l in range(DEPTH):
        last = l == DEPTH - 1
        w_t = jnp.swapaxes(w_in[l], 0, 1)
        w_qkv = _qkv_weights(w_t)
        w_f = _forget_weights(w_t)
        b_f = jnp.pad(b_forget[l], (0, LANES - N_HEADS)).reshape(1, LANES)

        xt = _ffn(xt, ffn1_norm[l].reshape(1, D_MODEL), _cast_column_tiles(ffn1_w_gate[l], FFN_TF),
                  _cast_column_tiles(ffn1_w_up[l], FFN_TF), _cast_rows(ffn1_w_down[l]), fgain, False)

        qkv_dil, qk_fox, vt_fox, logf = _qkv_proj(xt, mix_norm[l].reshape(1, D_MODEL), w_qkv, w_f,
                                                  b_f, cos_full, sin_signed)
        c_gate = _cumsum(logf.reshape(BATCH, SEQ, LANES))
        y_fox, w_gate2, w_up2 = _fox_attn(qk_fox, vt_fox, c_gate, ffn2_w_gate[l], ffn2_w_up[l])
        y_dil, w_down2, w_gates, w_out_b = _dil_attn(qkv_dil, dil_band, ffn2_w_down[l], w_t,
                                                     w_out[l])

        xt = _merge(xt, mix_norm[l].reshape(1, D_MODEL), y_dil, y_fox, w_gates,
                    b_gate_dil[l].reshape(1, D_MODEL), b_gate_fox[l].reshape(1, D_MODEL),
                    _cast_column_tiles(w_proj_dil[l], MRG_TJ),
                    _cast_column_tiles(w_proj_fox[l], MRG_TJ), w_out_b)

        xt = _ffn(xt, ffn2_norm[l].reshape(1, D_MODEL), w_gate2, w_up2, w_down2, fgain, last)
    return xt.reshape(BATCH, SEQ, D_MODEL)
```
